```python
import math
import jax, jax.numpy as jnp
from jax import lax
import numpy as np

D_MODEL = 1024
BATCH = 4
SEQ = 4096
DEPTH = 4
DEC_BATCH = 32
DEC_SEQ = 16
PAST_LEN = 1024

CHUNK = 64
N_META = 16
D_MIX = D_MODEL
DQK = 64
DVA = 2 * DQK
W_A = D_MIX // 2
H_A = W_A // DVA
C_CONV = D_MIX // 4
CONV_W = 31
DK_C = 64
DV_C = 64
H_C = (D_MIX // 4) // DV_C
D_FF = 2816
FFN_CONV_W = 3
N_BUCKETS = 32
MAX_DIST = 128
Q_BLOCK = 128
REC_BLOCK = 16
EPS = 1e-6
NEG = -1e30
IN_SIZES = (H_A * 2 * DQK, H_A * 2 * DQK, W_A, 2 * C_CONV, H_C * DK_C, H_C * DV_C, H_C * DK_C, H_C * DV_C)
IN_COLS = sum(IN_SIZES)

kernel_name = "hymba_diffattn_conformer_hgrn2_stream"


def rmsnorm(x, g):
    xf = x.astype(jnp.float32)
    y = xf * lax.rsqrt(jnp.mean(xf * xf, axis=-1, keepdims=True) + EPS)
    return (y * g.astype(jnp.float32)).astype(x.dtype)


def layernorm(x, g, b):
    xf = x.astype(jnp.float32)
    mu = jnp.mean(xf, axis=-1, keepdims=True)
    xc = xf - mu
    var = jnp.mean(xc * xc, axis=-1, keepdims=True)
    y = xc * lax.rsqrt(var + EPS) * g.astype(jnp.float32) + b.astype(jnp.float32)
    return y.astype(x.dtype)


def rel_bucket(rel):
    nb = N_BUCKETS // 2
    max_exact = nb // 2
    ret = jnp.where(rel > 0, nb, 0)
    n = jnp.abs(rel)
    nf = jnp.maximum(n, 1).astype(jnp.float32)
    large = max_exact + (jnp.log(nf / max_exact) / math.log(MAX_DIST / max_exact) * (nb - max_exact)).astype(jnp.int32)
    large = jnp.minimum(large, nb - 1)
    return ret + jnp.where(n < max_exact, n, large)


def diff_attention(q, k, v, rel_bias, lam, q_pos, k_pos, q_cid, k_cid):
    scale = DQK ** -0.5
    table = rel_bias.astype(jnp.float32)

    def block(args):
        qb, qp, qc = args
        s = jnp.einsum('bqhcd,bkhcd->bhcqk', qb, k).astype(jnp.float32) * scale
        bias = table[rel_bucket(k_pos[None, :] - qp[:, None])]
        s = s + jnp.transpose(bias, (2, 0, 1))[None, :, None]
        mask = k_cid[None, :] <= qc[:, None]
        s = jnp.where(mask, s, NEG)
        p = jax.nn.softmax(s, axis=-1)
        a = p[:, :, 0] - lam * p[:, :, 1]
        return jnp.einsum('bhqk,bkhd->bqhd', a.astype(v.dtype), v)

    B, Tq = q.shape[0], q.shape[1]
    if Tq <= Q_BLOCK:
        return block((q, q_pos, q_cid))
    nb = -(-Tq // Q_BLOCK)
    pad = nb * Q_BLOCK - Tq
    qpad = jnp.pad(q, ((0, 0), (0, pad), (0, 0), (0, 0), (0, 0)))
    qpos = jnp.pad(q_pos, (0, pad), mode='edge').reshape(nb, Q_BLOCK)
    qcid = jnp.pad(q_cid, (0, pad), mode='edge').reshape(nb, Q_BLOCK)
    qb = qpad.reshape((B, nb, Q_BLOCK) + q.shape[2:]).swapaxes(0, 1)
    out = lax.map(block, (qb, qpos, qcid))
    out = out.swapaxes(0, 1).reshape((B, nb * Q_BLOCK) + out.shape[3:])
    return out[:, :Tq]


def causal_dwconv(x, past, w, b):
    xc = jnp.concatenate([past.astype(x.dtype), x], axis=1)
    y = lax.conv_general_dilated(xc, w[:, None, :].astype(x.dtype), window_strides=(1,), padding='VALID',
                                 dimension_numbers=('NWC', 'WIO', 'NWC'), feature_group_count=x.shape[-1])
    return y + b.astype(x.dtype), xc[:, xc.shape[1] - (w.shape[0] - 1):]


def hgrn2(q, log_f, k, v, S0):
    B, T, H, _ = q.shape
    nb = -(-T // REC_BLOCK)
    pad = nb * REC_BLOCK - T

    def prep(a):
        a = jnp.pad(a, ((0, 0), (0, pad), (0, 0), (0, 0)))
        return a.reshape(B, nb, REC_BLOCK, H, a.shape[-1]).transpose(1, 0, 3, 2, 4)

    qs, fs, ks, vs = prep(q), prep(log_f), prep(k), prep(v)
    causal = jnp.tril(jnp.ones((REC_BLOCK, REC_BLOCK), dtype=bool))

    def step(S, inp):
        qc, fc, kc, vc = inp
        bcum = jnp.cumsum(fc, axis=2)
        diff = bcum[:, :, :, None, :] - bcum[:, :, None, :, :]
        decay = jnp.where(causal[:, :, None], jnp.exp(jnp.minimum(diff, 0.0)), 0.0)
        attn = jnp.einsum('bhtd,bhsd,bhtsd->bhts', qc, kc, decay)
        o = jnp.einsum('bhts,bhsv->bhtv', attn, vc) + jnp.einsum('bhtd,bhdv->bhtv', qc * jnp.exp(bcum), S)
        bL = bcum[:, :, -1:, :]
        S_new = jnp.exp(bL[:, :, 0, :])[..., None] * S + jnp.einsum('bhsd,bhsv->bhdv', kc * jnp.exp(bL - bcum), vc)
        return S_new, o

    S, o = lax.scan(step, S0, (qs, fs, ks, vs))
    o = o.transpose(1, 0, 3, 2, 4).reshape(B, nb * REC_BLOCK, H, -1)[:, :T]
    return o, S


def trunk_layer(x, p, rel_bias, k_past, v_past, conv_past, S0, ffn_past, q_pos, k_pos, q_cid, k_cid):
    B, T = x.shape[0], x.shape[1]
    h = rmsnorm(x, p['g_mix'])
    proj = h @ p['w_in'].astype(x.dtype)
    offs = [int(o) for o in np.cumsum(IN_SIZES)[:-1]]
    qa, ka, va, glu, zf, ic, qc, gc = jnp.split(proj, offs, axis=-1)

    qa = rmsnorm(qa.reshape(B, T, H_A, 2, DQK), p['g_q'])
    ka = rmsnorm(ka.reshape(B, T, H_A, 2, DQK), p['g_k'])
    va = va.reshape(B, T, H_A, DVA)
    k_all = ka if k_past is None else jnp.concatenate([k_past.astype(x.dtype), ka], axis=1)
    v_all = va if v_past is None else jnp.concatenate([v_past.astype(x.dtype), va], axis=1)
    f32 = jnp.float32
    lam = (jnp.exp(jnp.sum(p['lq1'].astype(f32) * p['lk1'].astype(f32)))
           - jnp.exp(jnp.sum(p['lq2'].astype(f32) * p['lk2'].astype(f32))) + p['lam_init'])
    oa = diff_attention(qa, k_all, v_all, rel_bias, lam, q_pos, k_pos, q_cid, k_cid)
    oa = rmsnorm(oa, p['g_diff'].reshape(H_A, DVA)) * (1.0 - p['lam_init'])

    ga, gb = jnp.split(glu, 2, axis=-1)
    u = ga * jax.nn.sigmoid(gb)
    cu, conv_new = causal_dwconv(u, conv_past, p['conv_w'], p['conv_b'])
    ob = jax.nn.silu(layernorm(cu, p['ln_g'], p['ln_b']))

    z = zf.astype(f32).reshape(B, T, H_C, DK_C)
    lb = p['lb'].reshape(H_C, DK_C)
    log_f = jnp.logaddexp(jnp.log(lb), jnp.log1p(-lb) + jax.nn.log_sigmoid(z))
    kc = (1.0 - lb) * jax.nn.sigmoid(-z)
    oc, S_new = hgrn2(qc.astype(f32).reshape(B, T, H_C, DK_C), log_f, kc,
                      ic.astype(f32).reshape(B, T, H_C, DV_C), S0.astype(f32))
    oc = rmsnorm(oc, p['g_hgrn'].reshape(H_C, DV_C)) * jax.nn.silu(gc.astype(f32).reshape(B, T, H_C, DV_C))

    mix = jnp.concatenate([oa.reshape(B, T, W_A).astype(x.dtype), ob.astype(x.dtype),
                           oc.reshape(B, T, H_C * DV_C).astype(x.dtype)], axis=-1)
    x = x + mix @ p['w_out'].astype(x.dtype)

    h2 = rmsnorm(x, p['g_ffn'])
    gate, val = jnp.split(h2 @ p['w_up'].astype(x.dtype), 2, axis=-1)
    gate_c, ffn_new = causal_dwconv(gate, ffn_past, p['ffn_conv_w'], p['ffn_conv_b'])
    x = x + (jax.nn.gelu(gate_c) * val) @ p['w_down'].astype(x.dtype)
    return x, ka, va, conv_new, S_new, ffn_new


def setup_inputs(seed: int = 0) -> dict:
    key = jax.random.key(seed)
    ks = jax.random.split(key, 40)

    def nrm(k, shape, s):
        return jax.random.normal(k, shape, jnp.float32) * s

    return {
        'x_prompt': nrm(ks[0], (BATCH, SEQ, D_MODEL), 1.0),
        'x_sample': nrm(ks[1], (DEC_BATCH, DEC_SEQ, D_MODEL), 1.0),
        'cache_k': nrm(ks[2], (DEPTH, DEC_BATCH, PAST_LEN, H_A, 2, DQK), 1.0),
        'cache_v': nrm(ks[3], (DEPTH, DEC_BATCH, PAST_LEN, H_A, DVA), 1.0),
        'state_conv': nrm(ks[4], (DEPTH, DEC_BATCH, CONV_W - 1, C_CONV), 0.5),
        'state_hgrn': nrm(ks[5], (DEPTH, DEC_BATCH, H_C, DK_C, DV_C), 0.3),
        'state_ffn': nrm(ks[6], (DEPTH, DEC_BATCH, FFN_CONV_W - 1, D_FF), 0.7),
        'meta_tokens': nrm(ks[7], (N_META, D_MODEL), 1.0),
        'rel_bias': nrm(ks[8], (N_BUCKETS, H_A), 0.5),
        'g_mix': 1.0 + nrm(ks[9], (DEPTH, D_MODEL), 0.02),
        'w_in': nrm(ks[10], (DEPTH, D_MODEL, IN_COLS), D_MODEL ** -0.5),
        'g_q': 1.0 + nrm(ks[11], (DEPTH, DQK), 0.02),
        'g_k': 1.0 + nrm(ks[12], (DEPTH, DQK), 0.02),
        'lam_q1': nrm(ks[13], (DEPTH, DQK), 0.1),
        'lam_k1': nrm(ks[14], (DEPTH, DQK), 0.1),
        'lam_q2': nrm(ks[15], (DEPTH, DQK), 0.1),
        'lam_k2': nrm(ks[16], (DEPTH, DQK), 0.1),
        'g_diff': 1.0 + nrm(ks[17], (DEPTH, W_A), 0.02),
        'conv_w': nrm(ks[18], (DEPTH, CONV_W, C_CONV), CONV_W ** -0.5),
        'conv_b': nrm(ks[19], (DEPTH, C_CONV), 0.02),
        'ln_g': 1.0 + nrm(ks[20], (DEPTH, C_CONV), 0.02),
        'ln_b': nrm(ks[21], (DEPTH, C_CONV), 0.02),
        'lb_logits': nrm(ks[22], (DEPTH, H_C * DK_C), 0.5),
        'g_hgrn': 1.0 + nrm(ks[23], (DEPTH, H_C * DV_C), 0.02),
        'w_out': nrm(ks[24], (DEPTH, D_MIX, D_MODEL), D_MIX ** -0.5),
        'g_ffn': 1.0 + nrm(ks[25], (DEPTH, D_MODEL), 0.02),
        'w_up': nrm(ks[26], (DEPTH, D_MODEL, 2 * D_FF), D_MODEL ** -0.5),
        'ffn_conv_w': nrm(ks[27], (DEPTH, FFN_CONV_W, D_FF), FFN_CONV_W ** -0.5),
        'ffn_conv_b': nrm(ks[28], (DEPTH, D_FF), 0.02),
        'w_down': nrm(ks[29], (DEPTH, D_FF, D_MODEL), D_FF ** -0.5),
        'g_final': 1.0 + nrm(ks[30], (D_MODEL,), 0.02),
    }


def reference(x_prompt, x_sample, cache_k, cache_v, state_conv, state_hgrn, state_ffn,
              meta_tokens, rel_bias, g_mix, w_in, g_q, g_k, lam_q1, lam_k1, lam_q2, lam_k2, g_diff,
              conv_w, conv_b, ln_g, ln_b, lb_logits, g_hgrn, w_out, g_ffn, w_up, ffn_conv_w, ffn_conv_b,
              w_down, g_final):
    lb_all = jnp.cumsum(jax.nn.softmax(lb_logits.astype(jnp.float32), axis=0), axis=0)
    lb_all = lb_all - lb_all[0:1]

    Bp = x_prompt.shape[0]
    xp = jnp.concatenate([jnp.broadcast_to(meta_tokens.astype(x_prompt.dtype)[None], (Bp, N_META, D_MODEL)), x_prompt], axis=1)
    Tp = xp.shape[1]
    pos_p = jnp.arange(Tp, dtype=jnp.int32)
    cid_p = (pos_p + (CHUNK - N_META)) // CHUNK

    Bs, Ss = x_sample.shape[0], x_sample.shape[1]
    P = cache_k.shape[2]
    kpos_s = jnp.arange(P + Ss, dtype=jnp.int32)
    qpos_s = kpos_s[P:]
    kcid_s = kpos_s // CHUNK
    qcid_s = qpos_s // CHUNK

    xs = x_sample
    kp_l, vp_l, cp_l, hp_l, fp_l = [], [], [], [], []
    ks_l, vs_l, cs_l, hs_l, fs_l = [], [], [], [], []
    for l in range(DEPTH):
        p = dict(g_mix=g_mix[l], w_in=w_in[l], g_q=g_q[l], g_k=g_k[l], lq1=lam_q1[l], lk1=lam_k1[l],
                 lq2=lam_q2[l], lk2=lam_k2[l], lam_init=0.8 - 0.6 * math.exp(-0.3 * l), g_diff=g_diff[l],
                 conv_w=conv_w[l], conv_b=conv_b[l], ln_g=ln_g[l], ln_b=ln_b[l], lb=lb_all[l],
                 g_hgrn=g_hgrn[l], w_out=w_out[l], g_ffn=g_ffn[l], w_up=w_up[l],
                 ffn_conv_w=ffn_conv_w[l], ffn_conv_b=ffn_conv_b[l], w_down=w_down[l])
        xp, kn, vn, cn, hn, fn = trunk_layer(
            xp, p, rel_bias, None, None,
            jnp.zeros((Bp, CONV_W - 1, C_CONV), xp.dtype),
            jnp.zeros((Bp, H_C, DK_C, DV_C), jnp.float32),
            jnp.zeros((Bp, FFN_CONV_W - 1, D_FF), xp.dtype),
            pos_p, pos_p, cid_p, cid_p)
        kp_l.append(kn); vp_l.append(vn); cp_l.append(cn); hp_l.append(hn.astype(xp.dtype)); fp_l.append(fn)
        xs, kn, vn, cn, hn, fn = trunk_layer(
            xs, p, rel_bias, cache_k[l], cache_v[l], state_conv[l], state_hgrn[l], state_ffn[l],
            qpos_s, kpos_s, qcid_s, kcid_s)
        ks_l.append(kn); vs_l.append(vn); cs_l.append(cn.astype(state_conv.dtype))
        hs_l.append(hn.astype(state_hgrn.dtype)); fs_l.append(fn.astype(state_ffn.dtype))

    y_prompt = rmsnorm(xp, g_final)[:, N_META:]
    y_sample = rmsnorm(xs, g_final)
    return (y_prompt, y_sample,
            jnp.stack(kp_l), jnp.stack(vp_l), jnp.stack(cp_l), jnp.stack(hp_l), jnp.stack(fp_l),
            jnp.stack(ks_l), jnp.stack(vs_l), jnp.stack(cs_l), jnp.stack(hs_l), jnp.stack(fs_l))
```

```python
import functools
import math

import jax
import jax.numpy as jnp
from jax import lax
from jax.experimental import pallas as pl
from jax.experimental.pallas import tpu as pltpu

F32 = jnp.float32
BF16 = jnp.bfloat16
I32 = jnp.int32

D_MODEL = 1024
N_META = 16
CHUNK = 64
DQK = 64
DVA = 2 * DQK
H_A = 4
W_A = H_A * DVA
C_CONV = 256
CONV_W = 31
H_C = 4
DK_C = 64
DV_C = 64
W_C = H_C * DV_C
D_FF = 2816
FFN_CONV_W = 3
N_BUCKETS = 32
MAX_DIST = 128
REC_BLOCK = 16
EPS = 1e-6
NEG = -1e30
IN_COLS = 2 * W_A + W_A + 2 * C_CONV + 4 * W_C

LANES = 128
SUBLANES = 8
VMEM_LIMIT = 56 * 1024 * 1024

_NB = N_BUCKETS // 2
_MAX_EXACT = _NB // 2
_BUCKET_THR = tuple(
    next(n for n in range(_MAX_EXACT, MAX_DIST + 1)
         if n ** (_NB - _MAX_EXACT) * _MAX_EXACT ** k >= _MAX_EXACT ** (_NB - _MAX_EXACT) * MAX_DIST ** k)
    for k in range(1, _NB - _MAX_EXACT))


def _rms(x, g):
    return x * lax.rsqrt(jnp.mean(x * x, axis=-1, keepdims=True) + EPS) * g


def _dot(a, b):
    return jnp.dot(a, b, preferred_element_type=F32)


def _dot_nt(a, b):
    return lax.dot_general(a, b, (((1,), (1,)), ((), ())), preferred_element_type=F32)


def _dot_tn(a, b):
    return lax.dot_general(a, b, (((0,), (0,)), ((), ())), preferred_element_type=F32)


def _sigmoid(x):
    return jax.nn.sigmoid(x)


def _const_spec(shape):
    nd = len(shape)
    return pl.BlockSpec(shape, lambda *_: (0,) * nd, pipeline_mode=pl.Buffered(1))


def _bias_kernel(tab_ref, o_ref, *, qpos0, kpos0, prompt_chunks):
    h = pl.program_id(0)
    shape = o_ref.shape[1:]
    r = lax.broadcasted_iota(I32, shape, 0) + qpos0
    c = lax.broadcasted_iota(I32, shape, 1) + kpos0
    rel = c - r
    n = jnp.abs(rel)
    large = jnp.full(shape, _MAX_EXACT, I32)
    for thr in _BUCKET_THR:
        large = large + (n >= thr).astype(I32)
    bucket = jnp.where(rel > 0, _NB, 0) + jnp.where(n < _MAX_EXACT, n, large)
    out = jnp.zeros(shape, F32)
    for b in range(N_BUCKETS):
        out = jnp.where(bucket == b, tab_ref[b, h], out)
    shift = CHUNK - N_META if prompt_chunks else 0
    log2_chunk = CHUNK.bit_length() - 1
    visible = ((c + shift) >> log2_chunk) <= ((r + shift) >> log2_chunk)
    o_ref[0] = jnp.where(visible, out, NEG)


def _bias_tile(rel_bias, rows, cols, qpos0, kpos0, prompt_chunks):
    return pl.pallas_call(
        functools.partial(_bias_kernel, qpos0=qpos0, kpos0=kpos0, prompt_chunks=prompt_chunks),
        grid=(H_A,),
        in_specs=[pl.BlockSpec(memory_space=pltpu.SMEM)],
        out_specs=pl.BlockSpec((1, rows, cols), lambda h: (h, 0, 0)),
        out_shape=jax.ShapeDtypeStruct((H_A, rows, cols), F32),
        name="relpos_bias",
    )(rel_bias)


def _inproj_kernel(x_ref, gmix_ref, win_ref, gq_ref, gk_ref, seg_ref,
                   q_ref, kf_ref, kb_ref, vf_ref, vb_ref, glu_ref, hraw_ref):
    h = _rms(x_ref[...], gmix_ref[...]).astype(BF16)

    def proj(lo, hi):
        return _dot(h, win_ref[:, lo:hi])

    def group_norm(a, g):
        ms = _dot((a * a).astype(BF16), seg_ref[...])
        return a * lax.rsqrt(ms + EPS) * g

    qa = group_norm(proj(0, W_A), gq_ref[...])
    q_ref[...] = (qa * DQK ** -0.5).astype(BF16)
    ka = group_norm(proj(W_A, 2 * W_A), gk_ref[...])
    kf_ref[...] = ka
    kb_ref[...] = ka.astype(BF16)
    va = proj(2 * W_A, 3 * W_A)
    vf_ref[...] = va
    vb_ref[...] = va.astype(BF16)
    glu_ref[...] = proj(3 * W_A, 3 * W_A + 2 * C_CONV)
    hraw_ref[...] = proj(3 * W_A + 2 * C_CONV, IN_COLS)


def _inproj(x2d, gmix, win, gq, gk, seg, tm):
    n = x2d.shape[0]
    row = lambda w: pl.BlockSpec((tm, w), lambda i: (i, 0))
    outs = [(W_A, BF16), (W_A, F32), (W_A, BF16), (W_A, F32), (W_A, BF16), (2 * C_CONV, F32), (4 * W_C, F32)]
    return pl.pallas_call(
        _inproj_kernel,
        grid=(n // tm,),
        in_specs=[row(D_MODEL), _const_spec((1, D_MODEL)), _const_spec((D_MODEL, IN_COLS)),
                  _const_spec((1, W_A)), _const_spec((1, W_A)), _const_spec((W_A, W_A))],
        out_specs=[row(w) for w, _ in outs],
        out_shape=[jax.ShapeDtypeStruct((n, w), dt) for w, dt in outs],
        compiler_params=pltpu.CompilerParams(dimension_semantics=("arbitrary",), vmem_limit_bytes=VMEM_LIMIT),
        name="inproj",
    )(x2d, gmix, win, gq, gk, seg)


_CONV_PAD = 32


def _conv_kernel(glu_ref, past_ref, w_ref, b_ref, lng_ref, lnb_ref, ob_ref, new_ref, xc_ref, *, tt, rc):
    halo = CONV_W - 1
    h0 = _CONV_PAD - halo

    @pl.when(pl.program_id(1) == 0)
    def _():
        xc_ref[h0:_CONV_PAD, :] = past_ref[0]

    glu = glu_ref[0]
    xc_ref[_CONV_PAD:_CONV_PAD + tt, :] = glu[:, :C_CONV] * _sigmoid(glu[:, C_CONV:])
    bias = b_ref[...]
    for c in range(tt // rc):
        acc = jnp.zeros((rc, C_CONV), F32) + bias
        for j in range(CONV_W):
            acc = acc + w_ref[j:j + 1, :] * xc_ref[c * rc + j + h0:c * rc + j + h0 + rc, :]
        mu = jnp.mean(acc, axis=-1, keepdims=True)
        xm = acc - mu
        var = jnp.mean(xm * xm, axis=-1, keepdims=True)
        y = xm * lax.rsqrt(var + EPS) * lng_ref[...] + lnb_ref[...]
        ob_ref[0, c * rc:(c + 1) * rc, :] = (y * _sigmoid(y)).astype(BF16)
    new = xc_ref[h0 + tt:_CONV_PAD + tt, :]
    new_ref[0] = new
    xc_ref[h0:_CONV_PAD, :] = new


def _conv_module(glu3, past, w, b, lng, lnb, tt, shared_past):
    nb, t, _ = glu3.shape
    rc = min(tt, 128)
    pidx = (lambda bi, ti: (0, 0, 0)) if shared_past else (lambda bi, ti: (bi, 0, 0))
    return pl.pallas_call(
        functools.partial(_conv_kernel, tt=tt, rc=rc),
        grid=(nb, t // tt),
        in_specs=[pl.BlockSpec((1, tt, 2 * C_CONV), lambda bi, ti: (bi, ti, 0)),
                  pl.BlockSpec((1, CONV_W - 1, C_CONV), pidx),
                  _const_spec((CONV_W, C_CONV)), _const_spec((1, C_CONV)),
                  _const_spec((1, C_CONV)), _const_spec((1, C_CONV))],
        out_specs=[pl.BlockSpec((1, tt, C_CONV), lambda bi, ti: (bi, ti, 0)),
                   pl.BlockSpec((1, CONV_W - 1, C_CONV), lambda bi, ti: (bi, 0, 0))],
        out_shape=[jax.ShapeDtypeStruct((nb, t, C_CONV), BF16),
                   jax.ShapeDtypeStruct((nb, CONV_W - 1, C_CONV), F32)],
        scratch_shapes=[pltpu.VMEM((_CONV_PAD + tt, C_CONV), F32)],
        compiler_params=pltpu.CompilerParams(dimension_semantics=("arbitrary", "arbitrary")),
        name="conv_module",
    )(glu3, past, w, b, lng, lnb)


def _hgrn_kernel(hraw_ref, lbc_ref, ghg_ref, segm_ref, segs_ref, bd_ref, s0_ref,
                 oc_ref, sout_ref, s_scr, qe_scr, ke_scr, v_scr, dl_scr, oi_scr, *, tc):
    nblk = tc // REC_BLOCK
    blk3 = (nblk, REC_BLOCK, W_C)

    @pl.when(pl.program_id(1) == 0)
    def _():
        s_scr[...] = s0_ref[0]

    hr = hraw_ref[0]
    z = hr[:, 0:W_C]
    v = hr[:, W_C:2 * W_C]
    q = hr[:, 2 * W_C:3 * W_C]
    gc = hr[:, 3 * W_C:4 * W_C]
    log_lb = lbc_ref[0:1, :]
    log_1m_lb = lbc_ref[1:2, :]
    one_m_lb = lbc_ref[2:3, :]

    log_sig = jnp.minimum(z, 0.0) - jnp.log1p(jnp.exp(-jnp.abs(z)))
    b_arg = log_1m_lb + log_sig
    log_f = jnp.maximum(log_lb, b_arg) + jnp.log1p(jnp.exp(-jnp.abs(log_lb - b_arg)))
    kc = one_m_lb * _sigmoid(-z)

    pos = lax.broadcasted_iota(I32, (tc, W_C), 0) & (REC_BLOCK - 1)
    bcum = log_f
    sh = 1
    while sh < REC_BLOCK:
        bcum = bcum + jnp.where(pos >= sh, pltpu.roll(bcum, sh, axis=0), 0.0)
        sh *= 2

    b3 = bcum.reshape(blk3)
    k3 = kc.reshape(blk3)
    q3 = q.reshape(blk3)
    v3 = v.reshape(blk3)
    b_last = b3[:, REC_BLOCK - 1:REC_BLOCK, :]
    qe_scr[...] = (q * jnp.exp(bcum)).astype(BF16)
    ke_scr[...] = (k3 * jnp.exp(b_last - b3)).reshape(tc, W_C).astype(BF16)
    v_scr[...] = v.astype(BF16)
    dl_scr[...] = jnp.exp(b_last).reshape(nblk, W_C)

    sidx = lax.broadcasted_iota(I32, blk3, 1)
    o_intra = jnp.zeros(blk3, F32)
    for t in range(REC_BLOCK):
        bt = b3[:, t:t + 1, :]
        qt = q3[:, t:t + 1, :]
        decay = jnp.where(sidx <= t, jnp.exp(jnp.minimum(bt - b3, 0.0)), 0.0)
        x = (decay * k3 * qt).reshape(tc, W_C).astype(BF16)
        attn = _dot(x, segs_ref[...]).reshape(blk3)
        ot = jnp.sum(attn * v3, axis=1, keepdims=True)
        o_intra = jnp.where(sidx == t, ot, o_intra)

    def block(j, carry):
        r0 = pl.multiple_of(j * REC_BLOCK, REC_BLOCK)
        s = s_scr[...]
        oi_scr[pl.ds(r0, REC_BLOCK), :] = _dot_nt(qe_scr[pl.ds(r0, REC_BLOCK), :], s.astype(BF16))
        u = _dot_tn(v_scr[pl.ds(r0, REC_BLOCK), :], ke_scr[pl.ds(r0, REC_BLOCK), :])
        s_scr[...] = dl_scr[pl.ds(j, 1), :] * s + u * bd_ref[...]
        return carry

    lax.fori_loop(0, nblk, block, 0)

    o = o_intra.reshape(tc, W_C) + oi_scr[...]
    ms = _dot((o * o).astype(BF16), segm_ref[...])
    oc_ref[0] = (o * lax.rsqrt(ms + EPS) * ghg_ref[...] * (gc * _sigmoid(gc))).astype(BF16)

    @pl.when(pl.program_id(1) == pl.num_programs(1) - 1)
    def _():
        sout_ref[0] = s_scr[...]


def _hgrn(hraw3, lbc, ghg, segm, segs, bd, s0, tc, shared_past):
    nb, t, _ = hraw3.shape
    sidx = (lambda bi, ti: (0, 0, 0)) if shared_past else (lambda bi, ti: (bi, 0, 0))
    return pl.pallas_call(
        functools.partial(_hgrn_kernel, tc=tc),
        grid=(nb, t // tc),
        in_specs=[pl.BlockSpec((1, tc, 4 * W_C), lambda bi, ti: (bi, ti, 0)),
                  _const_spec((SUBLANES, W_C)), _const_spec((1, W_C)),
                  _const_spec((W_C, W_C)), _const_spec((W_C, W_C)), _const_spec((W_C, W_C)),
                  pl.BlockSpec((1, W_C, W_C), sidx)],
        out_specs=[pl.BlockSpec((1, tc, W_C), lambda bi, ti: (bi, ti, 0)),
                   pl.BlockSpec((1, W_C, W_C), lambda bi, ti: (bi, 0, 0))],
        out_shape=[jax.ShapeDtypeStruct((nb, t, W_C), BF16),
                   jax.ShapeDtypeStruct((nb, W_C, W_C), F32)],
        scratch_shapes=[pltpu.VMEM((W_C, W_C), F32), pltpu.VMEM((tc, W_C), BF16), pltpu.VMEM((tc, W_C), BF16),
                        pltpu.VMEM((tc, W_C), BF16), pltpu.VMEM((tc // REC_BLOCK, W_C), F32),
                        pltpu.VMEM((tc, W_C), F32)],
        compiler_params=pltpu.CompilerParams(dimension_semantics=("arbitrary", "arbitrary")),
        name="hgrn2",
    )(hraw3, lbc, ghg, segm, segs, bd, s0)


def _split_halves(q):
    lane = lax.broadcasted_iota(I32, q.shape, 1)
    zero = jnp.zeros_like(q)
    return jnp.concatenate([jnp.where(lane < DQK, q, zero), jnp.where(lane >= DQK, q, zero)], axis=0)


def _twice(b):
    return jnp.concatenate([b, b], axis=0)


def _diff_combine(acc, l, t, lam_init, lq1, lk1, lq2, lk2, gd):
    o = acc / l
    lam = (jnp.exp(jnp.sum(lq1 * lk1, axis=1, keepdims=True))
           - jnp.exp(jnp.sum(lq2 * lk2, axis=1, keepdims=True)) + lam_init)
    a = o[:t] - lam * o[t:]
    return (_rms(a, gd) * (1.0 - lam_init)).astype(BF16)


def _attn_main_kernel(sc_ref, farc_ref, q_ref, k_ref, v_ref, km_ref, vm_ref, bt_ref, btm_ref,
                      lq1_ref, lk1_ref, lq2_ref, lk2_ref, gd_ref, o_ref, m_scr, l_scr, acc_scr, *, tq):
    h = pl.program_id(1)
    i = pl.program_id(2)
    qq = _split_halves(q_ref[0])
    farc = farc_ref[h]

    s = _dot_nt(qq, km_ref[0]) + _twice(jnp.where(i == 0, btm_ref[0], farc))
    m0 = jnp.max(s, axis=1, keepdims=True)
    p = jnp.exp(s - m0)
    m_scr[...] = m0
    l_scr[...] = jnp.sum(p, axis=1, keepdims=True)
    acc_scr[...] = _dot(p.astype(BF16), vm_ref[0])

    def step(j, bias):
        r0 = pl.multiple_of(j * tq, tq)
        s = _dot_nt(qq, k_ref[0, pl.ds(r0, tq), :]) + bias
        m_prev = m_scr[...]
        m_new = jnp.maximum(m_prev, jnp.max(s, axis=1, keepdims=True))
        alpha = jnp.exp(m_prev - m_new)
        p = jnp.exp(s - m_new)
        l_scr[...] = alpha * l_scr[...] + jnp.sum(p, axis=1, keepdims=True)
        acc_scr[...] = alpha * acc_scr[...] + _dot(p.astype(BF16), v_ref[0, pl.ds(r0, tq), :])
        m_scr[...] = m_new

    def far(j, carry):
        step(j, farc)
        return carry

    lax.fori_loop(0, jnp.maximum(i - 1, 0), far, 0)

    @pl.when(i >= 1)
    def _():
        step(i - 1, _twice(bt_ref[1, 0]))

    step(i, _twice(bt_ref[0, 0]))

    o_ref[0] = _diff_combine(acc_scr[...], l_scr[...], tq, sc_ref[0], lq1_ref[...], lk1_ref[...],
                             lq2_ref[...], lk2_ref[...], gd_ref[...])


def _attn_main(q3, k3, v3, km, vm, bt, btm, farc, lam_init, lq1, lk1, lq2, lk2, gdiff, tq):
    nb, t, _ = q3.shape
    p = km.shape[1]
    assert tq >= MAX_DIST and tq % CHUNK == 0, "tiles two or more below the diagonal must be all-far"
    smem = pl.BlockSpec(memory_space=pltpu.SMEM)
    vec = lambda: _const_spec((1, DQK))
    return pl.pallas_call(
        functools.partial(_attn_main_kernel, tq=tq),
        grid=(nb, H_A, t // tq),
        in_specs=[smem, smem,
                  pl.BlockSpec((1, tq, DVA), lambda b, h, i: (b, i, h)),
                  pl.BlockSpec((1, t, DVA), lambda b, h, i: (b, 0, h)),
                  pl.BlockSpec((1, t, DVA), lambda b, h, i: (b, 0, h)),
                  pl.BlockSpec((1, p, DVA), lambda b, h, i: (0, 0, h)),
                  pl.BlockSpec((1, p, DVA), lambda b, h, i: (0, 0, h)),
                  pl.BlockSpec((2, 1, tq, tq), lambda b, h, i: (0, h, 0, 0)),
                  pl.BlockSpec((1, tq, p), lambda b, h, i: (h, 0, 0)),
                  vec(), vec(), vec(), vec(),
                  pl.BlockSpec((1, DVA), lambda b, h, i: (0, h))],
        out_specs=pl.BlockSpec((1, tq, DVA), lambda b, h, i: (b, i, h)),
        out_shape=jax.ShapeDtypeStruct((nb, t, W_A), BF16),
        scratch_shapes=[pltpu.VMEM((2 * tq, 1), F32), pltpu.VMEM((2 * tq, 1), F32),
                        pltpu.VMEM((2 * tq, DVA), F32)],
        compiler_params=pltpu.CompilerParams(dimension_semantics=("arbitrary",) * 3),
        name="attn_main",
    )(lam_init, farc, q3, k3, v3, km, vm, bt, btm, lq1, lk1, lq2, lk2, gdiff)


def _attn_small_kernel(sc_ref, q_ref, kc_ref, vc_ref, kn_ref, vn_ref, bp_ref, bn_ref,
                       lq1_ref, lk1_ref, lq2_ref, lk2_ref, gd_ref, o_ref, *, t, n_cached):
    b = pl.program_id(0)
    qq = _split_halves(q_ref[0])
    past_bias = jnp.where(b < n_cached, 0.0, NEG)
    s_p = _dot_nt(qq, kc_ref[0, 0].astype(BF16)) + _twice(bp_ref[0]) + past_bias
    s_n = _dot_nt(qq, kn_ref[0]) + _twice(bn_ref[0])
    m = jnp.maximum(jnp.max(s_p, axis=1, keepdims=True), jnp.max(s_n, axis=1, keepdims=True))
    p_p = jnp.exp(s_p - m)
    p_n = jnp.exp(s_n - m)
    l = jnp.sum(p_p, axis=1, keepdims=True) + jnp.sum(p_n, axis=1, keepdims=True)
    acc = _dot(p_p.astype(BF16), vc_ref[0, 0].astype(BF16)) + _dot(p_n.astype(BF16), vn_ref[0])
    o_ref[0] = _diff_combine(acc, l, t, sc_ref[0], lq1_ref[...], lk1_ref[...],
                             lq2_ref[...], lk2_ref[...], gd_ref[...])


def _attn_small(q3, cache_k, cache_v, layer, kn, vn, bp, bn, lam_init, lq1, lk1, lq2, lk2, gdiff):
    nb, t, _ = q3.shape
    n_cached, p = cache_k.shape[1], cache_k.shape[2]
    smem = pl.BlockSpec(memory_space=pltpu.SMEM)
    vec = lambda: _const_spec((1, DQK))
    cache = pl.BlockSpec((1, 1, p, DVA), lambda b, h: (layer, jnp.minimum(b, n_cached - 1), 0, h))
    new = pl.BlockSpec((1, t, DVA), lambda b, h: (b, 0, h))
    return pl.pallas_call(
        functools.partial(_attn_small_kernel, t=t, n_cached=n_cached),
        grid=(nb, H_A),
        in_specs=[smem, new, cache, cache, new, new,
                  pl.BlockSpec((1, t, p), lambda b, h: (h, 0, 0)),
                  pl.BlockSpec((1, t, t), lambda b, h: (h, 0, 0)),
                  vec(), vec(), vec(), vec(),
                  pl.BlockSpec((1, DVA), lambda b, h: (0, h))],
        out_specs=new,
        out_shape=jax.ShapeDtypeStruct((nb, t, W_A), BF16),
        compiler_params=pltpu.CompilerParams(dimension_semantics=("arbitrary", "arbitrary")),
        name="attn_small",
    )(lam_init, q3, cache_k, cache_v, kn, vn, bp, bn, lq1, lk1, lq2, lk2, gdiff)


_FF_CHUNK = 256


def _ffn_kernel(x_ref, oa_ref, ob_ref, oc_ref, wout_ref, gffn_ref, wup_ref, cw_ref, cb_ref, wdn_ref,
                past_ref, gfin_ref, y_ref, new_ref, halo_scr, *, bb, tt, final_norm):
    rows = bb * tt
    taps = FFN_CONV_W - 1

    @pl.when(pl.program_id(1) == 0)
    def _():
        halo_scr[...] = past_ref[...]

    x1 = (x_ref[...].reshape(rows, D_MODEL)
          + _dot(oa_ref[...].reshape(rows, W_A), wout_ref[0:W_A, :])
          + _dot(ob_ref[...].reshape(rows, C_CONV), wout_ref[W_A:W_A + C_CONV, :])
          + _dot(oc_ref[...].reshape(rows, W_C), wout_ref[W_A + C_CONV:D_MODEL, :]))
    h2 = _rms(x1, gffn_ref[...]).astype(BF16)

    tpos = lax.broadcasted_iota(I32, (bb, tt, _FF_CHUNK), 1)
    acc = jnp.zeros((rows, D_MODEL), F32)
    for c in range(D_FF // _FF_CHUNK):
        lo, hi = c * _FF_CHUNK, (c + 1) * _FF_CHUNK
        gate = _dot(h2, wup_ref[:, lo:hi])
        val = _dot(h2, wup_ref[:, D_FF + lo:D_FF + hi])
        gate3 = gate.reshape(bb, tt, _FF_CHUNK)
        conv = cw_ref[taps:taps + 1, lo:hi] * gate + cb_ref[:, lo:hi]
        for d in range(1, taps + 1):
            shifted = pltpu.roll(gate, d, axis=0).reshape(bb, tt, _FF_CHUNK)
            for e in range(d):
                shifted = jnp.where(tpos == e, halo_scr[:, taps - d + e:taps - d + e + 1, lo:hi], shifted)
            conv = conv + cw_ref[taps - d:taps - d + 1, lo:hi] * shifted.reshape(rows, _FF_CHUNK)
        halo_scr[:, :, lo:hi] = gate3[:, tt - taps:tt, :]
        gelu = 0.5 * conv * (1.0 + jnp.tanh(math.sqrt(2.0 / math.pi) * (conv + 0.044715 * (conv * conv * conv))))
        acc = acc + _dot((gelu * val).astype(BF16), wdn_ref[lo:hi, :])

    y = x1 + acc
    if final_norm:
        y = _rms(y, gfin_ref[...])
    y_ref[...] = y.reshape(bb, tt, D_MODEL)
    new_ref[...] = halo_scr[...]


def _ffn(x3, oa3, ob3, oc3, wout, gffn, wup, cw, cb, wdn, past, gfin, bb, tt, shared_past, final_norm):
    nb, t, _ = x3.shape
    taps = FFN_CONV_W - 1
    assert tt >= taps
    tok = lambda w: pl.BlockSpec((bb, tt, w), lambda bi, ti: (bi, ti, 0))
    pidx = (lambda bi, ti: (0, 0, 0)) if shared_past else (lambda bi, ti: (bi, 0, 0))
    assert not (shared_past and bb != 1)
    return pl.pallas_call(
        functools.partial(_ffn_kernel, bb=bb, tt=tt, final_norm=final_norm),
        grid=(nb // bb, t // tt),
        in_specs=[tok(D_MODEL), tok(W_A), tok(C_CONV), tok(W_C),
                  _const_spec((D_MODEL, D_MODEL)), _const_spec((1, D_MODEL)),
                  _const_spec((D_MODEL, 2 * D_FF)), _const_spec((FFN_CONV_W, D_FF)), _const_spec((1, D_FF)),
                  _const_spec((D_FF, D_MODEL)),
                  pl.BlockSpec((bb, taps, D_FF), pidx), _const_spec((1, D_MODEL))],
        out_specs=[tok(D_MODEL), pl.BlockSpec((bb, taps, D_FF), lambda bi, ti: (bi, 0, 0))],
        out_shape=[jax.ShapeDtypeStruct((nb, t, D_MODEL), F32),
                   jax.ShapeDtypeStruct((nb, taps, D_FF), F32)],
        scratch_shapes=[pltpu.VMEM((bb, taps, D_FF), F32)],
        compiler_params=pltpu.CompilerParams(dimension_semantics=("arbitrary", "arbitrary"),
                                             vmem_limit_bytes=VMEM_LIMIT),
        name="outproj_ffn",
    )(x3, oa3, ob3, oc3, wout, gffn, wup, cw, cb, wdn, past, gfin)


def _seg_matrix(width, group, value, dtype):
    g = jnp.arange(width, dtype=I32) // group
    return jnp.where(g[:, None] == g[None, :], value, 0.0).astype(dtype)


def _state_to_kernel(s):
    n = s.shape[0]
    eye = jnp.eye(H_C, dtype=s.dtype)
    full = jnp.einsum('nhdv,hg->nhvgd', s, eye)
    return full.reshape(n, W_C, H_C * DK_C)


def _state_from_kernel(sf):
    n = sf.shape[0]
    s5 = sf.reshape(n, H_C, DV_C, H_C, DK_C)
    diag = jnp.stack([s5[:, h, :, h, :] for h in range(H_C)], axis=1)
    return jnp.swapaxes(diag, 2, 3)


def _pick_tile(total, want):
    t = min(total, want)
    assert total % t == 0, (total, want)
    return t


def kernel(x_prompt, x_sample, cache_k, cache_v, state_conv, state_hgrn, state_ffn, meta_tokens, rel_bias, g_mix, w_in, g_q, g_k, lam_q1, lam_k1, lam_q2, lam_k2, g_diff, conv_w, conv_b, ln_g, ln_b, lb_logits, g_hgrn, w_out, g_ffn, w_up, ffn_conv_w, ffn_conv_b, w_down, g_final):
    depth = g_mix.shape[0]
    bp, seq, _ = x_prompt.shape
    bs, ts, _ = x_sample.shape
    past_len = cache_k.shape[2]
    assert ts == N_META == REC_BLOCK and meta_tokens.shape[0] == N_META
    assert seq % CHUNK == 0 and past_len % CHUNK == 0

    lb_all = jnp.cumsum(jax.nn.softmax(lb_logits.astype(F32), axis=0), axis=0)
    lb_all = lb_all - lb_all[0:1]
    lb_consts = jnp.stack([jnp.log(lb_all), jnp.log1p(-lb_all), 1.0 - lb_all]
                          + [jnp.zeros_like(lb_all)] * (SUBLANES - 3), axis=1)

    seg_qk = _seg_matrix(W_A, DQK, 1.0 / DQK, BF16)
    seg_mean = _seg_matrix(W_C, DV_C, 1.0 / DV_C, BF16)
    seg_sum = _seg_matrix(W_C, DK_C, 1.0, BF16)
    blockdiag = _seg_matrix(W_C, DK_C, 1.0, F32)

    tq = _pick_tile(seq, 256)
    bt_main = jnp.stack([_bias_tile(rel_bias, tq, tq, N_META, N_META, True),
                         _bias_tile(rel_bias, tq, tq, N_META + tq, N_META, True)])
    bt_meta = _bias_tile(rel_bias, tq, N_META, N_META, 0, True)
    far_bias = rel_bias[_NB - 1, :]
    bias_past = _bias_tile(rel_bias, ts, past_len, past_len, 0, False)
    bias_new = _bias_tile(rel_bias, ts, ts, past_len, past_len, False)

    xs = jnp.concatenate([x_sample, meta_tokens.astype(x_sample.dtype)[None]], axis=0)
    xm = x_prompt
    nbs = bs + 1
    zrow = lambda a: jnp.zeros((1,) + a.shape[1:], a.dtype)

    tm_main = _pick_tile(bp * seq, 512)
    outs = {k: [] for k in ("kp", "vp", "cp", "hp", "fp", "ks", "vs", "cs", "hs", "fs")}
    for l in range(depth):
        last = l == depth - 1
        lam_init = jnp.full((1,), 0.8 - 0.6 * math.exp(-0.3 * l), F32)
        win = w_in[l].astype(BF16)
        wout = w_out[l].astype(BF16)
        wup = w_up[l].astype(BF16)
        wdn = w_down[l].astype(BF16)
        gq = jnp.tile(g_q[l], 2 * H_A)[None]
        gk = jnp.tile(g_k[l], 2 * H_A)[None]
        row = lambda a: a[l][None]
        common_attn = (lam_init, row(lam_q1), row(lam_k1), row(lam_q2), row(lam_k2), row(g_diff))

        def mixer_front(x3, tm):
            nb, t, _ = x3.shape
            q, kf, kb, vf, vb, glu, hraw = _inproj(x3.reshape(nb * t, D_MODEL), row(g_mix), win, gq, gk, seg_qk, tm)
            r3 = lambda a: a.reshape(nb, t, a.shape[-1])
            return r3(q), r3(kf), r3(kb), r3(vf), r3(vb), r3(glu), r3(hraw)

        q, kf_s, kb, vf_s, vb, glu, hraw = mixer_front(xs, nbs * ts)
        oa = _attn_small(q, cache_k.reshape(depth, bs, past_len, W_A), cache_v.reshape(depth, bs, past_len, W_A),
                         l, kb, vb, bias_past, bias_new, *common_attn)
        ob, conv_s = _conv_module(glu, jnp.concatenate([state_conv[l], zrow(state_conv[l])]),
                                  conv_w[l], row(conv_b), row(ln_g), row(ln_b), ts, False)
        oc, hg_s = _hgrn(hraw, lb_consts[l], row(g_hgrn), seg_mean, seg_sum, blockdiag,
                         _state_to_kernel(jnp.concatenate([state_hgrn[l], zrow(state_hgrn[l])]).astype(F32)),
                         ts, False)
        xs, ffn_s = _ffn(xs, oa, ob, oc, wout, row(g_ffn), wup, ffn_conv_w[l], row(ffn_conv_b), wdn,
                         jnp.concatenate([state_ffn[l], zrow(state_ffn[l])]), g_final[None],
                         nbs, ts, False, last)
        meta_k, meta_v = kb[bs:], vb[bs:]

        q, kf_m, kb, vf_m, vb, glu, hraw = mixer_front(xm, tm_main)
        oa = _attn_main(q, kb, vb, meta_k, meta_v, bt_main, bt_meta, far_bias, *common_attn, tq)
        ob, conv_m = _conv_module(glu, conv_s[bs:], conv_w[l], row(conv_b), row(ln_g), row(ln_b),
                                  _pick_tile(seq, 512), True)
        oc, hg_m = _hgrn(hraw, lb_consts[l], row(g_hgrn), seg_mean, seg_sum, blockdiag, hg_s[bs:],
                         _pick_tile(seq, 256), True)
        xm, ffn_m = _ffn(xm, oa, ob, oc, wout, row(g_ffn), wup, ffn_conv_w[l], row(ffn_conv_b), wdn,
                         ffn_s[bs:], g_final[None], 1, _pick_tile(seq, 512), True, last)

        meta_rows = lambda a: jnp.broadcast_to(a[bs:], (bp,) + a.shape[1:])
        outs["kp"].append(jnp.concatenate([meta_rows(kf_s), kf_m], axis=1).reshape(bp, N_META + seq, H_A, 2, DQK))
        outs["vp"].append(jnp.concatenate([meta_rows(vf_s), vf_m], axis=1).reshape(bp, N_META + seq, H_A, DVA))
        outs["cp"].append(conv_m)
        outs["hp"].append(_state_from_kernel(hg_m))
        outs["fp"].append(ffn_m)
        outs["ks"].append(kf_s[:bs].reshape(bs, ts, H_A, 2, DQK))
        outs["vs"].append(vf_s[:bs].reshape(bs, ts, H_A, DVA))
        outs["cs"].append(conv_s[:bs])
        outs["hs"].append(_state_from_kernel(hg_s[:bs]))
        outs["fs"].append(ffn_s[:bs])

    st = {k: jnp.stack(v) for k, v in outs.items()}
    return (xm, xs[:bs], st["kp"], st["vp"], st["cp"], st["hp"], st["fp"],
            st["ks"], st["vs"], st["cs"], st["hs"], st["fs"])
```

```python
import functools
import math

import jax
import jax.numpy as jnp
from jax import lax
from jax.experimental import pallas as pl
from jax.experimental.pallas import tpu as pltpu

F32 = jnp.float32
BF16 = jnp.bfloat16
I32 = jnp.int32

D_MODEL = 1024
N_META = 16
CHUNK = 64
DQK = 64
DVA = 2 * DQK
H_A = 4
W_A = H_A * DVA
C_CONV = 256
CONV_W = 31
H_C = 4
DK_C = 64
DV_C = 64
W_C = H_C * DV_C
D_FF = 2816
FFN_CONV_W = 3
N_BUCKETS = 32
MAX_DIST = 128
REC_BLOCK = 16
EPS = 1e-6
NEG = -1e30
LOG2E = math.log2(math.e)
IN_COLS = 2 * W_A + W_A + 2 * C_CONV + 4 * W_C

LANES = 128
SUBLANES = 8
VMEM_LIMIT = 56 * 1024 * 1024

_NB = N_BUCKETS // 2
_MAX_EXACT = _NB // 2
_BUCKET_THR = tuple(
    next(n for n in range(_MAX_EXACT, MAX_DIST + 1)
         if n ** (_NB - _MAX_EXACT) * _MAX_EXACT ** k >= _MAX_EXACT ** (_NB - _MAX_EXACT) * MAX_DIST ** k)
    for k in range(1, _NB - _MAX_EXACT))


def _rms(x, g):
    return x * lax.rsqrt(jnp.mean(x * x, axis=-1, keepdims=True) + EPS) * g


def _dot(a, b):
    return jnp.dot(a, b, preferred_element_type=F32)


def _dot_nt(a, b):
    return lax.dot_general(a, b, (((1,), (1,)), ((), ())), preferred_element_type=F32)


def _dot_tn(a, b):
    return lax.dot_general(a, b, (((0,), (0,)), ((), ())), preferred_element_type=F32)


def _sigmoid(x):
    return jax.nn.sigmoid(x)


def _const_spec(shape):
    nd = len(shape)
    return pl.BlockSpec(shape, lambda *_: (0,) * nd, pipeline_mode=pl.Buffered(1))


def _bias_kernel(tab_ref, o_ref, *, qpos0, kpos0, prompt_chunks, keys_on_rows):
    h = pl.program_id(0)
    shape = o_ref.shape[1:]
    r = lax.broadcasted_iota(I32, shape, 1 if keys_on_rows else 0) + qpos0
    c = lax.broadcasted_iota(I32, shape, 0 if keys_on_rows else 1) + kpos0
    rel = c - r
    n = jnp.abs(rel)
    large = jnp.full(shape, _MAX_EXACT, I32)
    for thr in _BUCKET_THR:
        large = large + (n >= thr).astype(I32)
    bucket = jnp.where(rel > 0, _NB, 0) + jnp.where(n < _MAX_EXACT, n, large)
    out = jnp.zeros(shape, F32)
    for b in range(N_BUCKETS):
        out = jnp.where(bucket == b, tab_ref[b, h] * LOG2E, out)
    shift = CHUNK - N_META if prompt_chunks else 0
    log2_chunk = CHUNK.bit_length() - 1
    visible = ((c + shift) >> log2_chunk) <= ((r + shift) >> log2_chunk)
    o_ref[0] = jnp.where(visible, out, NEG)


def _bias_tile(rel_bias, rows, cols, qpos0, kpos0, prompt_chunks, keys_on_rows=False):
    return pl.pallas_call(
        functools.partial(_bias_kernel, qpos0=qpos0, kpos0=kpos0, prompt_chunks=prompt_chunks,
                          keys_on_rows=keys_on_rows),
        grid=(H_A,),
        in_specs=[pl.BlockSpec(memory_space=pltpu.SMEM)],
        out_specs=pl.BlockSpec((1, rows, cols), lambda h: (h, 0, 0)),
        out_shape=jax.ShapeDtypeStruct((H_A, rows, cols), F32),
        name="relpos_bias",
    )(rel_bias)


def _inproj_kernel(x_ref, gmix_ref, win_ref, gq_ref, gk_ref, seg_ref,
                   q_ref, kf_ref, kb_ref, vf_ref, vb_ref, glu_ref, hraw_ref, vt_ref=None):
    h = _rms(x_ref[...], gmix_ref[...]).astype(BF16)

    def proj(lo, hi):
        return _dot(h, win_ref[:, lo:hi])

    def group_norm(a, g):
        ms = _dot((a * a).astype(BF16), seg_ref[...])
        return a * lax.rsqrt(ms + EPS) * g

    qa = group_norm(proj(0, W_A), gq_ref[...])
    q_ref[...] = (qa * (DQK ** -0.5 * LOG2E)).astype(BF16)
    ka = group_norm(proj(W_A, 2 * W_A), gk_ref[...])
    kf_ref[...] = ka
    kb_ref[...] = ka.astype(BF16)
    va = proj(2 * W_A, 3 * W_A)
    vf_ref[...] = va
    vb_ref[...] = va.astype(BF16)
    if vt_ref is not None:
        vt_ref[...] = va.T.astype(BF16)
    glu_ref[...] = proj(3 * W_A, 3 * W_A + 2 * C_CONV)
    hraw_ref[...] = proj(3 * W_A + 2 * C_CONV, IN_COLS)


def _inproj(x2d, gmix, win, gq, gk, seg, tm, with_vt):
    n = x2d.shape[0]
    row = lambda w: pl.BlockSpec((tm, w), lambda i: (i, 0))
    outs = [(W_A, BF16), (W_A, F32), (W_A, BF16), (W_A, F32), (W_A, BF16), (2 * C_CONV, F32), (4 * W_C, F32)]
    out_specs = [row(w) for w, _ in outs]
    out_shape = [jax.ShapeDtypeStruct((n, w), dt) for w, dt in outs]
    if with_vt:
        out_specs.append(pl.BlockSpec((W_A, tm), lambda i: (0, i)))
        out_shape.append(jax.ShapeDtypeStruct((W_A, n), BF16))
    return pl.pallas_call(
        _inproj_kernel,
        grid=(n // tm,),
        in_specs=[row(D_MODEL), _const_spec((1, D_MODEL)), _const_spec((D_MODEL, IN_COLS)),
                  _const_spec((1, W_A)), _const_spec((1, W_A)), _const_spec((W_A, W_A))],
        out_specs=out_specs,
        out_shape=out_shape,
        compiler_params=pltpu.CompilerParams(dimension_semantics=("arbitrary",), vmem_limit_bytes=VMEM_LIMIT),
        name="inproj",
    )(x2d, gmix, win, gq, gk, seg)


_CONV_PAD = 32


def _conv_kernel(glu_ref, past_ref, w_ref, b_ref, lng_ref, lnb_ref, ob_ref, new_ref, xc_ref, *, tt, rc):
    halo = CONV_W - 1
    h0 = _CONV_PAD - halo

    @pl.when(pl.program_id(1) == 0)
    def _():
        xc_ref[h0:_CONV_PAD, :] = past_ref[0]

    glu = glu_ref[0]
    xc_ref[_CONV_PAD:_CONV_PAD + tt, :] = glu[:, :C_CONV] * _sigmoid(glu[:, C_CONV:])
    bias = b_ref[...]
    for c in range(tt // rc):
        acc = jnp.zeros((rc, C_CONV), F32) + bias
        for j in range(CONV_W):
            acc = acc + w_ref[j:j + 1, :] * xc_ref[c * rc + j + h0:c * rc + j + h0 + rc, :]
        mu = jnp.mean(acc, axis=-1, keepdims=True)
        xm = acc - mu
        var = jnp.mean(xm * xm, axis=-1, keepdims=True)
        y = xm * lax.rsqrt(var + EPS) * lng_ref[...] + lnb_ref[...]
        ob_ref[0, c * rc:(c + 1) * rc, :] = (y * _sigmoid(y)).astype(BF16)
    new = xc_ref[h0 + tt:_CONV_PAD + tt, :]
    new_ref[0] = new
    xc_ref[h0:_CONV_PAD, :] = new


def _conv_module(glu3, past, w, b, lng, lnb, tt, shared_past):
    nb, t, _ = glu3.shape
    rc = min(tt, 128)
    pidx = (lambda bi, ti: (0, 0, 0)) if shared_past else (lambda bi, ti: (bi, 0, 0))
    return pl.pallas_call(
        functools.partial(_conv_kernel, tt=tt, rc=rc),
        grid=(nb, t // tt),
        in_specs=[pl.BlockSpec((1, tt, 2 * C_CONV), lambda bi, ti: (bi, ti, 0)),
                  pl.BlockSpec((1, CONV_W - 1, C_CONV), pidx),
                  _const_spec((CONV_W, C_CONV)), _const_spec((1, C_CONV)),
                  _const_spec((1, C_CONV)), _const_spec((1, C_CONV))],
        out_specs=[pl.BlockSpec((1, tt, C_CONV), lambda bi, ti: (bi, ti, 0)),
                   pl.BlockSpec((1, CONV_W - 1, C_CONV), lambda bi, ti: (bi, 0, 0))],
        out_shape=[jax.ShapeDtypeStruct((nb, t, C_CONV), BF16),
                   jax.ShapeDtypeStruct((nb, CONV_W - 1, C_CONV), F32)],
        scratch_shapes=[pltpu.VMEM((_CONV_PAD + tt, C_CONV), F32)],
        compiler_params=pltpu.CompilerParams(dimension_semantics=("arbitrary", "arbitrary")),
        name="conv_module",
    )(glu3, past, w, b, lng, lnb)


def _hgrn_kernel(hraw_ref, lbc_ref, ghg_ref, segm_ref, segs_ref, bd_ref, s0_ref,
                 oc_ref, sout_ref, s_scr, qe_scr, ke_scr, v_scr, dl_scr, oi_scr, *, tc):
    nblk = tc // REC_BLOCK
    blk3 = (nblk, REC_BLOCK, W_C)

    @pl.when(pl.program_id(1) == 0)
    def _():
        s_scr[...] = s0_ref[0]

    hr = hraw_ref[0]
    z = hr[:, 0:W_C]
    v = hr[:, W_C:2 * W_C]
    q = hr[:, 2 * W_C:3 * W_C]
    gc = hr[:, 3 * W_C:4 * W_C]
    log_lb = lbc_ref[0:1, :]
    log_1m_lb = lbc_ref[1:2, :]
    one_m_lb = lbc_ref[2:3, :]

    log_sig = jnp.minimum(z, 0.0) - jnp.log1p(jnp.exp(-jnp.abs(z)))
    b_arg = log_1m_lb + log_sig
    log_f = jnp.maximum(log_lb, b_arg) + jnp.log1p(jnp.exp(-jnp.abs(log_lb - b_arg)))
    kc = one_m_lb * _sigmoid(-z)

    pos = lax.broadcasted_iota(I32, (tc, W_C), 0) & (REC_BLOCK - 1)
    bcum = log_f
    sh = 1
    while sh < REC_BLOCK:
        bcum = bcum + jnp.where(pos >= sh, pltpu.roll(bcum, sh, axis=0), 0.0)
        sh *= 2

    b3 = bcum.reshape(blk3)
    k3 = kc.reshape(blk3)
    q3 = q.reshape(blk3)
    v3 = v.reshape(blk3)
    b_last = b3[:, REC_BLOCK - 1:REC_BLOCK, :]
    qe_scr[...] = (q * jnp.exp(bcum)).astype(BF16)
    ke_scr[...] = (k3 * jnp.exp(b_last - b3)).reshape(tc, W_C).astype(BF16)
    v_scr[...] = v.astype(BF16)
    dl_scr[...] = jnp.exp(b_last).reshape(nblk, W_C)

    sidx = lax.broadcasted_iota(I32, blk3, 1)
    o_intra = jnp.zeros(blk3, F32)
    for t in range(REC_BLOCK):
        bt = b3[:, t:t + 1, :]
        qt = q3[:, t:t + 1, :]
        decay = jnp.where(sidx <= t, jnp.exp(jnp.minimum(bt - b3, 0.0)), 0.0)
        x = (decay * k3 * qt).reshape(tc, W_C).astype(BF16)
        attn = _dot(x, segs_ref[...]).reshape(blk3)
        ot = jnp.sum(attn * v3, axis=1, keepdims=True)
        o_intra = jnp.where(sidx == t, ot, o_intra)

    def block(j, carry):
        r0 = pl.multiple_of(j * REC_BLOCK, REC_BLOCK)
        s = s_scr[...]
        oi_scr[pl.ds(r0, REC_BLOCK), :] = _dot_nt(qe_scr[pl.ds(r0, REC_BLOCK), :], s.astype(BF16))
        u = _dot_tn(v_scr[pl.ds(r0, REC_BLOCK), :], ke_scr[pl.ds(r0, REC_BLOCK), :])
        s_scr[...] = dl_scr[pl.ds(j, 1), :] * s + u * bd_ref[...]
        return carry

    lax.fori_loop(0, nblk, block, 0)

    o = o_intra.reshape(tc, W_C) + oi_scr[...]
    ms = _dot((o * o).astype(BF16), segm_ref[...])
    oc_ref[0] = (o * lax.rsqrt(ms + EPS) * ghg_ref[...] * (gc * _sigmoid(gc))).astype(BF16)

    @pl.when(pl.program_id(1) == pl.num_programs(1) - 1)
    def _():
        sout_ref[0] = s_scr[...]


def _hgrn(hraw3, lbc, ghg, segm, segs, bd, s0, tc, shared_past):
    nb, t, _ = hraw3.shape
    sidx = (lambda bi, ti: (0, 0, 0)) if shared_past else (lambda bi, ti: (bi, 0, 0))
    return pl.pallas_call(
        functools.partial(_hgrn_kernel, tc=tc),
        grid=(nb, t // tc),
        in_specs=[pl.BlockSpec((1, tc, 4 * W_C), lambda bi, ti: (bi, ti, 0)),
                  _const_spec((SUBLANES, W_C)), _const_spec((1, W_C)),
                  _const_spec((W_C, W_C)), _const_spec((W_C, W_C)), _const_spec((W_C, W_C)),
                  pl.BlockSpec((1, W_C, W_C), sidx)],
        out_specs=[pl.BlockSpec((1, tc, W_C), lambda bi, ti: (bi, ti, 0)),
                   pl.BlockSpec((1, W_C, W_C), lambda bi, ti: (bi, 0, 0))],
        out_shape=[jax.ShapeDtypeStruct((nb, t, W_C), BF16),
                   jax.ShapeDtypeStruct((nb, W_C, W_C), F32)],
        scratch_shapes=[pltpu.VMEM((W_C, W_C), F32), pltpu.VMEM((tc, W_C), BF16), pltpu.VMEM((tc, W_C), BF16),
                        pltpu.VMEM((tc, W_C), BF16), pltpu.VMEM((tc // REC_BLOCK, W_C), F32),
                        pltpu.VMEM((tc, W_C), F32)],
        compiler_params=pltpu.CompilerParams(dimension_semantics=("arbitrary", "arbitrary")),
        name="hgrn2",
    )(hraw3, lbc, ghg, segm, segs, bd, s0)


def _split_halves(q):
    lane = lax.broadcasted_iota(I32, q.shape, 1)
    zero = jnp.zeros_like(q)
    return jnp.concatenate([jnp.where(lane < DQK, q, zero), jnp.where(lane >= DQK, q, zero)], axis=0)


def _twice(b):
    return jnp.concatenate([b, b], axis=0)


def _diff_combine(acc, l, t, lam_init, lq1, lk1, lq2, lk2, gd):
    o = acc / l
    lam = (jnp.exp(jnp.sum(lq1 * lk1, axis=1, keepdims=True))
           - jnp.exp(jnp.sum(lq2 * lk2, axis=1, keepdims=True)) + lam_init)
    a = o[:t] - lam * o[t:]
    return (_rms(a, gd) * (1.0 - lam_init)).astype(BF16)


def _attn_main_kernel(sc_ref, farc_ref, q_ref, k_ref, vt_ref, km_ref, vmt_ref, bt_ref, btm_ref,
                      lq1_ref, lk1_ref, lq2_ref, lk2_ref, gdt_ref, o_ref, acc_scr, *, tq):
    h = pl.program_id(1)
    i = pl.program_id(2)
    q = q_ref[0]
    lane = lax.broadcasted_iota(I32, q.shape, 1)
    zero = jnp.zeros_like(q)
    q_half = (jnp.where(lane < DQK, q, zero), jnp.where(lane >= DQK, q, zero))

    def scores(j):
        r0 = pl.multiple_of(j * tq, tq)
        bias = bt_ref[jnp.minimum(i - j, 2), 0]
        k_tile = k_ref[0, pl.ds(r0, tq), :]
        return tuple(_dot_nt(k_tile, qh) + bias for qh in q_half)

    def update(j, c, s, m_prev, l_prev, s_extra=None):
        r0 = pl.multiple_of(j * tq, tq)
        m_new = jnp.maximum(m_prev, jnp.max(s, axis=0, keepdims=True))
        if s_extra is not None:
            m_new = jnp.maximum(m_new, jnp.max(s_extra, axis=0, keepdims=True))
        alpha = jnp.exp2(m_prev - m_new)
        p = jnp.exp2(s - m_new)
        l_new = alpha * l_prev + jnp.sum(p, axis=0, keepdims=True)
        pv = _dot(vt_ref[:, pl.ds(r0, tq)], p.astype(BF16))
        if s_extra is not None:
            p_extra = jnp.exp2(s_extra - m_new)
            l_new = l_new + jnp.sum(p_extra, axis=0, keepdims=True)
            pv = pv + _dot(vmt_ref[...], p_extra.astype(BF16))
        acc_scr[c] = alpha * acc_scr[c] + pv
        return m_new, l_new

    def body(j, carry):
        s_cur, m_prev, l_prev = carry
        s_next = scores(j + 1)
        stats = [update(j, c, s_cur[c], m_prev[c], l_prev[c]) for c in range(2)]
        return s_next, tuple(st[0] for st in stats), tuple(st[1] for st in stats)

    acc_scr[...] = jnp.zeros(acc_scr.shape, F32)
    m0 = (jnp.full((1, tq), -jnp.inf, F32),) * 2
    l0 = (jnp.zeros((1, tq), F32),) * 2
    s_last, m_prev, l_prev = lax.fori_loop(0, i, body, (scores(0), m0, l0))
    bias_meta = jnp.where(i == 0, btm_ref[0], farc_ref[h])
    o = []
    for c in range(2):
        s_meta = _dot_nt(km_ref[0], q_half[c]) + bias_meta
        _, l_fin = update(i, c, s_last[c], m_prev[c], l_prev[c], s_extra=s_meta)
        o.append(acc_scr[c] * (1.0 / l_fin))

    lam_init = sc_ref[0]
    lam = (jnp.exp(jnp.sum(lq1_ref[...] * lk1_ref[...], axis=1, keepdims=True))
           - jnp.exp(jnp.sum(lq2_ref[...] * lk2_ref[...], axis=1, keepdims=True)) + lam_init)
    a = o[0] - lam * o[1]
    gain = jnp.concatenate([gdt_ref[...]] * (tq // LANES), axis=1)
    y = a * lax.rsqrt(jnp.mean(a * a, axis=0, keepdims=True) + EPS) * gain * (1.0 - lam_init)
    o_ref[0] = y.T.astype(BF16)


def _attn_main(q3, k3, vt, km, vmt, bt, btm, farc, lam_init, lq1, lk1, lq2, lk2, gdt, tq):
    nb, t, _ = q3.shape
    p = km.shape[1]
    assert tq >= MAX_DIST and tq % CHUNK == 0, "tiles two or more below the diagonal must be all-far"
    smem = pl.BlockSpec(memory_space=pltpu.SMEM)
    vec = lambda: _const_spec((1, DQK))
    return pl.pallas_call(
        functools.partial(_attn_main_kernel, tq=tq),
        grid=(nb, H_A, t // tq),
        in_specs=[smem, smem,
                  pl.BlockSpec((1, tq, DVA), lambda b, h, i: (b, i, h)),
                  pl.BlockSpec((1, t, DVA), lambda b, h, i: (b, 0, h)),
                  pl.BlockSpec((DVA, t), lambda b, h, i: (h, b)),
                  pl.BlockSpec((1, p, DVA), lambda b, h, i: (0, 0, h)),
                  pl.BlockSpec((DVA, p), lambda b, h, i: (h, 0)),
                  pl.BlockSpec((3, 1, tq, tq), lambda b, h, i: (0, h, 0, 0)),
                  pl.BlockSpec((1, p, tq), lambda b, h, i: (h, 0, 0)),
                  vec(), vec(), vec(), vec(),
                  pl.BlockSpec((DVA, LANES), lambda b, h, i: (h, 0))],
        out_specs=pl.BlockSpec((1, tq, DVA), lambda b, h, i: (b, i, h)),
        out_shape=jax.ShapeDtypeStruct((nb, t, W_A), BF16),
        scratch_shapes=[pltpu.VMEM((2, DVA, tq), F32)],
        compiler_params=pltpu.CompilerParams(dimension_semantics=("arbitrary",) * 3),
        name="attn_main",
    )(lam_init, farc, q3, k3, vt, km, vmt, bt, btm, lq1, lk1, lq2, lk2, gdt)


def _attn_small_kernel(sc_ref, q_ref, kc_ref, vc_ref, kn_ref, vn_ref, bp_ref, bn_ref,
                       lq1_ref, lk1_ref, lq2_ref, lk2_ref, gd_ref, o_ref, *, t, n_cached):
    b = pl.program_id(0)
    qq = _split_halves(q_ref[0])
    past_bias = jnp.where(b < n_cached, 0.0, NEG)
    s_p = _dot_nt(qq, kc_ref[0, 0].astype(BF16)) + _twice(bp_ref[0]) + past_bias
    s_n = _dot_nt(qq, kn_ref[0]) + _twice(bn_ref[0])
    m = jnp.maximum(jnp.max(s_p, axis=1, keepdims=True), jnp.max(s_n, axis=1, keepdims=True))
    p_p = jnp.exp2(s_p - m)
    p_n = jnp.exp2(s_n - m)
    l = jnp.sum(p_p, axis=1, keepdims=True) + jnp.sum(p_n, axis=1, keepdims=True)
    acc = _dot(p_p.astype(BF16), vc_ref[0, 0].astype(BF16)) + _dot(p_n.astype(BF16), vn_ref[0])
    o_ref[0] = _diff_combine(acc, l, t, sc_ref[0], lq1_ref[...], lk1_ref[...],
                             lq2_ref[...], lk2_ref[...], gd_ref[...])


def _attn_small(q3, cache_k, cache_v, layer, kn, vn, bp, bn, lam_init, lq1, lk1, lq2, lk2, gdiff):
    nb, t, _ = q3.shape
    n_cached, p = cache_k.shape[1], cache_k.shape[2]
    smem = pl.BlockSpec(memory_space=pltpu.SMEM)
    vec = lambda: _const_spec((1, DQK))
    cache = pl.BlockSpec((1, 1, p, DVA), lambda b, h: (layer, jnp.minimum(b, n_cached - 1), 0, h))
    new = pl.BlockSpec((1, t, DVA), lambda b, h: (b, 0, h))
    return pl.pallas_call(
        functools.partial(_attn_small_kernel, t=t, n_cached=n_cached),
        grid=(nb, H_A),
        in_specs=[smem, new, cache, cache, new, new,
                  pl.BlockSpec((1, t, p), lambda b, h: (h, 0, 0)),
                  pl.BlockSpec((1, t, t), lambda b, h: (h, 0, 0)),
                  vec(), vec(), vec(), vec(),
                  pl.BlockSpec((1, DVA), lambda b, h: (0, h))],
        out_specs=new,
        out_shape=jax.ShapeDtypeStruct((nb, t, W_A), BF16),
        compiler_params=pltpu.CompilerParams(dimension_semantics=("arbitrary", "arbitrary")),
        name="attn_small",
    )(lam_init, q3, cache_k, cache_v, kn, vn, bp, bn, lq1, lk1, lq2, lk2, gdiff)


_FF_CHUNK = 256


def _ffn_kernel(x_ref, oa_ref, ob_ref, oc_ref, wout_ref, gffn_ref, wup_ref, cw_ref, cb_ref, wdn_ref,
                past_ref, gfin_ref, y_ref, new_ref, halo_scr, *, bb, tt, final_norm):
    rows = bb * tt
    taps = FFN_CONV_W - 1

    @pl.when(pl.program_id(1) == 0)
    def _():
        halo_scr[...] = past_ref[...]

    x1 = (x_ref[...].reshape(rows, D_MODEL)
          + _dot(oa_ref[...].reshape(rows, W_A), wout_ref[0:W_A, :])
          + _dot(ob_ref[...].reshape(rows, C_CONV), wout_ref[W_A:W_A + C_CONV, :])
          + _dot(oc_ref[...].reshape(rows, W_C), wout_ref[W_A + C_CONV:D_MODEL, :]))
    h2 = _rms(x1, gffn_ref[...]).astype(BF16)

    tpos = lax.broadcasted_iota(I32, (bb, tt, _FF_CHUNK), 1)
    acc = jnp.zeros((rows, D_MODEL), F32)
    for c in range(D_FF // _FF_CHUNK):
        lo, hi = c * _FF_CHUNK, (c + 1) * _FF_CHUNK
        gate = _dot(h2, wup_ref[:, lo:hi])
        val = _dot(h2, wup_ref[:, D_FF + lo:D_FF + hi])
        gate3 = gate.reshape(bb, tt, _FF_CHUNK)
        conv = cw_ref[taps:taps + 1, lo:hi] * gate + cb_ref[:, lo:hi]
        for d in range(1, taps + 1):
            shifted = pltpu.roll(gate, d, axis=0).reshape(bb, tt, _FF_CHUNK)
            for e in range(d):
                shifted = jnp.where(tpos == e, halo_scr[:, taps - d + e:taps - d + e + 1, lo:hi], shifted)
            conv = conv + cw_ref[taps - d:taps - d + 1, lo:hi] * shifted.reshape(rows, _FF_CHUNK)
        halo_scr[:, :, lo:hi] = gate3[:, tt - taps:tt, :]
        gelu = 0.5 * conv * (1.0 + jnp.tanh(math.sqrt(2.0 / math.pi) * (conv + 0.044715 * (conv * conv * conv))))
        acc = acc + _dot((gelu * val).astype(BF16), wdn_ref[lo:hi, :])

    y = x1 + acc
    if final_norm:
        y = _rms(y, gfin_ref[...])
    y_ref[...] = y.reshape(bb, tt, D_MODEL)
    new_ref[...] = halo_scr[...]


def _ffn(x3, oa3, ob3, oc3, wout, gffn, wup, cw, cb, wdn, past, gfin, bb, tt, shared_past, final_norm):
    nb, t, _ = x3.shape
    taps = FFN_CONV_W - 1
    assert tt >= taps
    tok = lambda w: pl.BlockSpec((bb, tt, w), lambda bi, ti: (bi, ti, 0))
    pidx = (lambda bi, ti: (0, 0, 0)) if shared_past else (lambda bi, ti: (bi, 0, 0))
    assert not (shared_past and bb != 1)
    return pl.pallas_call(
        functools.partial(_ffn_kernel, bb=bb, tt=tt, final_norm=final_norm),
        grid=(nb // bb, t // tt),
        in_specs=[tok(D_MODEL), tok(W_A), tok(C_CONV), tok(W_C),
                  _const_spec((D_MODEL, D_MODEL)), _const_spec((1, D_MODEL)),
                  _const_spec((D_MODEL, 2 * D_FF)), _const_spec((FFN_CONV_W, D_FF)), _const_spec((1, D_FF)),
                  _const_spec((D_FF, D_MODEL)),
                  pl.BlockSpec((bb, taps, D_FF), pidx), _const_spec((1, D_MODEL))],
        out_specs=[tok(D_MODEL), pl.BlockSpec((bb, taps, D_FF), lambda bi, ti: (bi, 0, 0))],
        out_shape=[jax.ShapeDtypeStruct((nb, t, D_MODEL), F32),
                   jax.ShapeDtypeStruct((nb, taps, D_FF), F32)],
        scratch_shapes=[pltpu.VMEM((bb, taps, D_FF), F32)],
        compiler_params=pltpu.CompilerParams(dimension_semantics=("arbitrary", "arbitrary"),
                                             vmem_limit_bytes=VMEM_LIMIT),
        name="outproj_ffn",
    )(x3, oa3, ob3, oc3, wout, gffn, wup, cw, cb, wdn, past, gfin)


def _seg_matrix(width, group, value, dtype):
    g = jnp.arange(width, dtype=I32) // group
    return jnp.where(g[:, None] == g[None, :], value, 0.0).astype(dtype)


def _state_to_kernel(s):
    n = s.shape[0]
    eye = jnp.eye(H_C, dtype=s.dtype)
    full = jnp.einsum('nhdv,hg->nhvgd', s, eye)
    return full.reshape(n, W_C, H_C * DK_C)


def _state_from_kernel(sf):
    n = sf.shape[0]
    s5 = sf.reshape(n, H_C, DV_C, H_C, DK_C)
    diag = jnp.stack([s5[:, h, :, h, :] for h in range(H_C)], axis=1)
    return jnp.swapaxes(diag, 2, 3)


def _pick_tile(total, want):
    t = min(total, want)
    assert total % t == 0, (total, want)
    return t


def kernel(x_prompt, x_sample, cache_k, cache_v, state_conv, state_hgrn, state_ffn, meta_tokens, rel_bias, g_mix, w_in, g_q, g_k, lam_q1, lam_k1, lam_q2, lam_k2, g_diff, conv_w, conv_b, ln_g, ln_b, lb_logits, g_hgrn, w_out, g_ffn, w_up, ffn_conv_w, ffn_conv_b, w_down, g_final):
    depth = g_mix.shape[0]
    bp, seq, _ = x_prompt.shape
    bs, ts, _ = x_sample.shape
    past_len = cache_k.shape[2]
    assert ts == N_META == REC_BLOCK and meta_tokens.shape[0] == N_META
    assert seq % CHUNK == 0 and past_len % CHUNK == 0

    lb_all = jnp.cumsum(jax.nn.softmax(lb_logits.astype(F32), axis=0), axis=0)
    lb_all = lb_all - lb_all[0:1]
    lb_consts = jnp.stack([jnp.log(lb_all), jnp.log1p(-lb_all), 1.0 - lb_all]
                          + [jnp.zeros_like(lb_all)] * (SUBLANES - 3), axis=1)

    seg_qk = _seg_matrix(W_A, DQK, 1.0 / DQK, BF16)
    seg_mean = _seg_matrix(W_C, DV_C, 1.0 / DV_C, BF16)
    seg_sum = _seg_matrix(W_C, DK_C, 1.0, BF16)
    blockdiag = _seg_matrix(W_C, DK_C, 1.0, F32)

    tq = _pick_tile(seq, 512)
    bt_main = jnp.stack([_bias_tile(rel_bias, tq, tq, N_META + d * tq, N_META, True, keys_on_rows=True)
                         for d in range(3)])
    bt_meta = _bias_tile(rel_bias, N_META, tq, N_META, 0, True, keys_on_rows=True)
    far_bias = rel_bias[_NB - 1, :] * LOG2E
    bias_past = _bias_tile(rel_bias, ts, past_len, past_len, 0, False)
    bias_new = _bias_tile(rel_bias, ts, ts, past_len, past_len, False)

    xs = jnp.concatenate([x_sample, meta_tokens.astype(x_sample.dtype)[None]], axis=0)
    xm = x_prompt
    nbs = bs + 1
    zrow = lambda a: jnp.zeros((1,) + a.shape[1:], a.dtype)

    tm_main = _pick_tile(bp * seq, 512)
    outs = {k: [] for k in ("kp", "vp", "cp", "hp", "fp", "ks", "vs", "cs", "hs", "fs")}
    for l in range(depth):
        last = l == depth - 1
        lam_init = jnp.full((1,), 0.8 - 0.6 * math.exp(-0.3 * l), F32)
        win = w_in[l].astype(BF16)
        wout = w_out[l].astype(BF16)
        wup = w_up[l].astype(BF16)
        wdn = w_down[l].astype(BF16)
        gq = jnp.tile(g_q[l], 2 * H_A)[None]
        gk = jnp.tile(g_k[l], 2 * H_A)[None]
        row = lambda a: a[l][None]
        common_attn = (lam_init, row(lam_q1), row(lam_k1), row(lam_q2), row(lam_k2), row(g_diff))

        def mixer_front(x3, tm, with_vt):
            nb, t, _ = x3.shape
            res = _inproj(x3.reshape(nb * t, D_MODEL), row(g_mix), win, gq, gk, seg_qk, tm, with_vt)
            r3 = lambda a: a.reshape(nb, t, a.shape[-1])
            return tuple(r3(a) for a in res[:7]) + tuple(res[7:])

        q, kf_s, kb, vf_s, vb, glu, hraw = mixer_front(xs, nbs * ts, False)
        oa = _attn_small(q, cache_k.reshape(depth, bs, past_len, W_A), cache_v.reshape(depth, bs, past_len, W_A),
                         l, kb, vb, bias_past, bias_new, *common_attn)
        ob, conv_s = _conv_module(glu, jnp.concatenate([state_conv[l], zrow(state_conv[l])]),
                                  conv_w[l], row(conv_b), row(ln_g), row(ln_b), ts, False)
        oc, hg_s = _hgrn(hraw, lb_consts[l], row(g_hgrn), seg_mean, seg_sum, blockdiag,
                         _state_to_kernel(jnp.concatenate([state_hgrn[l], zrow(state_hgrn[l])]).astype(F32)),
                         ts, False)
        xs, ffn_s = _ffn(xs, oa, ob, oc, wout, row(g_ffn), wup, ffn_conv_w[l], row(ffn_conv_b), wdn,
                         jnp.concatenate([state_ffn[l], zrow(state_ffn[l])]), g_final[None],
                         nbs, ts, False, last)
        meta_k, meta_vt = kb[bs:], vb[bs].T
        gdt = jnp.broadcast_to(g_diff[l][:, None], (W_A, LANES))

        q, kf_m, kb, vf_m, vb, glu, hraw, vt = mixer_front(xm, tm_main, True)
        oa = _attn_main(q, kb, vt, meta_k, meta_vt, bt_main, bt_meta, far_bias, *common_attn[:-1], gdt, tq)
        ob, conv_m = _conv_module(glu, conv_s[bs:], conv_w[l], row(conv_b), row(ln_g), row(ln_b),
                                  _pick_tile(seq, 512), True)
        oc, hg_m = _hgrn(hraw, lb_consts[l], row(g_hgrn), seg_mean, seg_sum, blockdiag, hg_s[bs:],
                         _pick_tile(seq, 256), True)
        xm, ffn_m = _ffn(xm, oa, ob, oc, wout, row(g_ffn), wup, ffn_conv_w[l], row(ffn_conv_b), wdn,
                         ffn_s[bs:], g_final[None], 1, _pick_tile(seq, 512), True, last)

        meta_rows = lambda a: jnp.broadcast_to(a[bs:], (bp,) + a.shape[1:])
        outs["kp"].append(jnp.concatenate([meta_rows(kf_s), kf_m], axis=1).reshape(bp, N_META + seq, H_A, 2, DQK))
        outs["vp"].append(jnp.concatenate([meta_rows(vf_s), vf_m], axis=1).reshape(bp, N_META + seq, H_A, DVA))
        outs["cp"].append(conv_m)
        outs["hp"].append(_state_from_kernel(hg_m))
        outs["fp"].append(ffn_m)
        outs["ks"].append(kf_s[:bs].reshape(bs, ts, H_A, 2, DQK))
        outs["vs"].append(vf_s[:bs].reshape(bs, ts, H_A, DVA))
        outs["cs"].append(conv_s[:bs])
        outs["hs"].append(_state_from_kernel(hg_s[:bs]))
        outs["fs"].append(ffn_s[:bs])

    st = {k: jnp.stack(v) for k, v in outs.items()}
    return (xm, xs[:bs], st["kp"], st["vp"], st["cp"], st["hp"], st["fp"],
            st["ks"], st["vs"], st["cs"], st["hs"], st["fs"])
```

```python
import functools
import math

import jax
import jax.numpy as jnp
from jax import lax
from jax.experimental import pallas as pl
from jax.experimental.pallas import tpu as pltpu

F32 = jnp.float32
BF16 = jnp.bfloat16
I32 = jnp.int32

D_MODEL = 1024
N_META = 16
CHUNK = 64
DQK = 64
DVA = 2 * DQK
H_A = 4
W_A = H_A * DVA
C_CONV = 256
CONV_W = 31
H_C = 4
DK_C = 64
DV_C = 64
W_C = H_C * DV_C
D_FF = 2816
FFN_CONV_W = 3
N_BUCKETS = 32
MAX_DIST = 128
REC_BLOCK = 16
EPS = 1e-6
NEG = -1e30
LOG2E = math.log2(math.e)
IN_COLS = 2 * W_A + W_A + 2 * C_CONV + 4 * W_C

LANES = 128
SUBLANES = 8
VMEM_LIMIT = 56 * 1024 * 1024

_NB = N_BUCKETS // 2
_MAX_EXACT = _NB // 2
_BUCKET_THR = tuple(
    next(n for n in range(_MAX_EXACT, MAX_DIST + 1)
         if n ** (_NB - _MAX_EXACT) * _MAX_EXACT ** k >= _MAX_EXACT ** (_NB - _MAX_EXACT) * MAX_DIST ** k)
    for k in range(1, _NB - _MAX_EXACT))


def _rms(x, g):
    return x * lax.rsqrt(jnp.mean(x * x, axis=-1, keepdims=True) + EPS) * g


def _dot(a, b):
    return jnp.dot(a, b, preferred_element_type=F32)


def _dot_nt(a, b):
    return lax.dot_general(a, b, (((1,), (1,)), ((), ())), preferred_element_type=F32)


def _dot_tn(a, b):
    return lax.dot_general(a, b, (((0,), (0,)), ((), ())), preferred_element_type=F32)


def _sigmoid(x):
    return jax.nn.sigmoid(x)


def _const_spec(shape):
    nd = len(shape)
    return pl.BlockSpec(shape, lambda *_: (0,) * nd, pipeline_mode=pl.Buffered(1))


def _bias_kernel(tab_ref, o_ref, *, qpos0, kpos0, prompt_chunks, keys_on_rows):
    h = pl.program_id(0)
    shape = o_ref.shape[1:]
    r = lax.broadcasted_iota(I32, shape, 1 if keys_on_rows else 0) + qpos0
    c = lax.broadcasted_iota(I32, shape, 0 if keys_on_rows else 1) + kpos0
    rel = c - r
    n = jnp.abs(rel)
    large = jnp.full(shape, _MAX_EXACT, I32)
    for thr in _BUCKET_THR:
        large = large + (n >= thr).astype(I32)
    bucket = jnp.where(rel > 0, _NB, 0) + jnp.where(n < _MAX_EXACT, n, large)
    out = jnp.zeros(shape, F32)
    for b in range(N_BUCKETS):
        out = jnp.where(bucket == b, tab_ref[b, h] * LOG2E, out)
    shift = CHUNK - N_META if prompt_chunks else 0
    log2_chunk = CHUNK.bit_length() - 1
    visible = ((c + shift) >> log2_chunk) <= ((r + shift) >> log2_chunk)
    o_ref[0] = jnp.where(visible, out, NEG)


def _bias_tile(rel_bias, rows, cols, qpos0, kpos0, prompt_chunks, keys_on_rows=False):
    return pl.pallas_call(
        functools.partial(_bias_kernel, qpos0=qpos0, kpos0=kpos0, prompt_chunks=prompt_chunks,
                          keys_on_rows=keys_on_rows),
        grid=(H_A,),
        in_specs=[pl.BlockSpec(memory_space=pltpu.SMEM)],
        out_specs=pl.BlockSpec((1, rows, cols), lambda h: (h, 0, 0)),
        out_shape=jax.ShapeDtypeStruct((H_A, rows, cols), F32),
        name="relpos_bias",
    )(rel_bias)


def _inproj_kernel(x_ref, gmix_ref, win_ref, gq_ref, gk_ref, seg_ref,
                   q_ref, kb_ref, vf4_ref, glu_ref, hraw_ref, k_extra_ref, v_extra_ref, *, feature_major):
    tm = x_ref.shape[0]
    h = _rms(x_ref[...], gmix_ref[...]).astype(BF16)

    def proj(lo, hi):
        return _dot(h, win_ref[:, lo:hi])

    def group_norm(a, g):
        ms = _dot((a * a).astype(BF16), seg_ref[...])
        return a * lax.rsqrt(ms + EPS) * g

    qa = group_norm(proj(0, W_A), gq_ref[...])
    q_ref[...] = (qa * (DQK ** -0.5 * LOG2E)).astype(BF16)
    ka = group_norm(proj(W_A, 2 * W_A), gk_ref[...])
    kb_ref[...] = ka.astype(BF16)
    va = proj(2 * W_A, 3 * W_A)
    for hd in range(H_A):
        vf4_ref[pl.ds(hd, tm, stride=H_A), :] = va[:, hd * DVA:(hd + 1) * DVA]
    if feature_major:
        k_extra_ref[0] = ka.T
        v_extra_ref[...] = va.T.astype(BF16)
    else:
        k_extra_ref[...] = ka
        v_extra_ref[...] = va.astype(BF16)
    glu_ref[...] = proj(3 * W_A, 3 * W_A + 2 * C_CONV)
    hraw_ref[...] = proj(3 * W_A + 2 * C_CONV, IN_COLS)


def _inproj(x2d, gmix, win, gq, gk, seg, tm, rows_per_seq, feature_major):
    n = x2d.shape[0]
    row = lambda w: pl.BlockSpec((tm, w), lambda i: (i, 0))
    outs = [(W_A, BF16), (W_A, BF16), None, (2 * C_CONV, F32), (4 * W_C, F32)]
    out_specs = [row(o[0]) if o else pl.BlockSpec((tm * H_A, DVA), lambda i: (i, 0)) for o in outs]
    out_shape = [jax.ShapeDtypeStruct((n, o[0]), o[1]) if o else jax.ShapeDtypeStruct((n * H_A, DVA), F32)
                 for o in outs]
    if feature_major:
        nt = rows_per_seq // tm
        out_specs += [pl.BlockSpec((1, W_A, tm), lambda i: (i // nt, 0, i % nt)),
                      pl.BlockSpec((W_A, tm), lambda i: (0, i))]
        out_shape += [jax.ShapeDtypeStruct((n // rows_per_seq, W_A, rows_per_seq), F32),
                      jax.ShapeDtypeStruct((W_A, n), BF16)]
    else:
        out_specs += [row(W_A), row(W_A)]
        out_shape += [jax.ShapeDtypeStruct((n, W_A), F32), jax.ShapeDtypeStruct((n, W_A), BF16)]
    return pl.pallas_call(
        functools.partial(_inproj_kernel, feature_major=feature_major),
        grid=(n // tm,),
        in_specs=[row(D_MODEL), _const_spec((1, D_MODEL)), _const_spec((D_MODEL, IN_COLS)),
                  _const_spec((1, W_A)), _const_spec((1, W_A)), _const_spec((W_A, W_A))],
        out_specs=out_specs,
        out_shape=out_shape,
        compiler_params=pltpu.CompilerParams(dimension_semantics=("arbitrary",), vmem_limit_bytes=VMEM_LIMIT),
        name="inproj",
    )(x2d, gmix, win, gq, gk, seg)


_CONV_PAD = 32


def _conv_kernel(glu_ref, past_ref, w_ref, b_ref, lng_ref, lnb_ref, ob_ref, new_ref, xc_ref, *, tt, rc):
    halo = CONV_W - 1
    h0 = _CONV_PAD - halo

    @pl.when(pl.program_id(1) == 0)
    def _():
        xc_ref[h0:_CONV_PAD, :] = past_ref[0]

    glu = glu_ref[0]
    xc_ref[_CONV_PAD:_CONV_PAD + tt, :] = glu[:, :C_CONV] * _sigmoid(glu[:, C_CONV:])
    bias = b_ref[...]
    for c in range(tt // rc):
        acc = jnp.zeros((rc, C_CONV), F32) + bias
        for j in range(CONV_W):
            acc = acc + w_ref[j:j + 1, :] * xc_ref[c * rc + j + h0:c * rc + j + h0 + rc, :]
        mu = jnp.mean(acc, axis=-1, keepdims=True)
        xm = acc - mu
        var = jnp.mean(xm * xm, axis=-1, keepdims=True)
        y = xm * lax.rsqrt(var + EPS) * lng_ref[...] + lnb_ref[...]
        ob_ref[0, c * rc:(c + 1) * rc, :] = (y * _sigmoid(y)).astype(BF16)
    new = xc_ref[h0 + tt:_CONV_PAD + tt, :]
    new_ref[0] = new
    xc_ref[h0:_CONV_PAD, :] = new


def _conv_module(glu3, past, w, b, lng, lnb, tt, shared_past):
    nb, t, _ = glu3.shape
    rc = min(tt, 128)
    pidx = (lambda bi, ti: (0, 0, 0)) if shared_past else (lambda bi, ti: (bi, 0, 0))
    return pl.pallas_call(
        functools.partial(_conv_kernel, tt=tt, rc=rc),
        grid=(nb, t // tt),
        in_specs=[pl.BlockSpec((1, tt, 2 * C_CONV), lambda bi, ti: (bi, ti, 0)),
                  pl.BlockSpec((1, CONV_W - 1, C_CONV), pidx),
                  _const_spec((CONV_W, C_CONV)), _const_spec((1, C_CONV)),
                  _const_spec((1, C_CONV)), _const_spec((1, C_CONV))],
        out_specs=[pl.BlockSpec((1, tt, C_CONV), lambda bi, ti: (bi, ti, 0)),
                   pl.BlockSpec((1, CONV_W - 1, C_CONV), lambda bi, ti: (bi, 0, 0))],
        out_shape=[jax.ShapeDtypeStruct((nb, t, C_CONV), BF16),
                   jax.ShapeDtypeStruct((nb, CONV_W - 1, C_CONV), F32)],
        scratch_shapes=[pltpu.VMEM((_CONV_PAD + tt, C_CONV), F32)],
        compiler_params=pltpu.CompilerParams(dimension_semantics=("arbitrary", "arbitrary")),
        name="conv_module",
    )(glu3, past, w, b, lng, lnb)


def _hgrn_kernel(hraw_ref, lbc_ref, ghg_ref, segm_ref, segs_ref, bd_ref, rsum_ref, s0_ref,
                 oc_ref, sout_ref, s_scr, *, tc):
    nblk = tc // REC_BLOCK
    blk3 = (nblk, REC_BLOCK, W_C)

    @pl.when(pl.program_id(1) == 0)
    def _():
        s_scr[...] = s0_ref[0]

    hr = hraw_ref[0]
    z = hr[:, 0:W_C]
    v = hr[:, W_C:2 * W_C]
    q = hr[:, 2 * W_C:3 * W_C]
    gc = hr[:, 3 * W_C:4 * W_C]
    log_lb = lbc_ref[0:1, :]
    log_1m_lb = lbc_ref[1:2, :]
    one_m_lb = lbc_ref[2:3, :]

    log_sig = jnp.minimum(z, 0.0) - jnp.log(1.0 + jnp.exp(-jnp.abs(z)))
    b_arg = log_1m_lb + log_sig
    log_f = jnp.maximum(log_lb, b_arg) + jnp.log(1.0 + jnp.exp(-jnp.abs(log_lb - b_arg)))
    kc = one_m_lb * _sigmoid(-z)

    pos = lax.broadcasted_iota(I32, (tc, W_C), 0) & (REC_BLOCK - 1)
    bcum = log_f
    sh = 1
    while sh < REC_BLOCK:
        bcum = bcum + jnp.where(pos >= sh, pltpu.roll(bcum, sh, axis=0), 0.0)
        sh *= 2

    b3 = bcum.reshape(blk3)
    b_last = b3[:, REC_BLOCK - 1:REC_BLOCK, :]
    qe = (q * jnp.exp(bcum)).astype(BF16)
    ke = (kc.reshape(blk3) * jnp.exp(b_last - b3)).reshape(tc, W_C).astype(BF16)
    v_bf = v.astype(BF16)
    dl = jnp.exp(b_last)

    half = REC_BLOCK // 2
    slab = (nblk, 2, half, W_C)
    hrows = nblk * half
    b_lo, b_hi = (bcum.reshape(slab)[:, i] for i in range(2))
    k_lo, k_hi = (kc.reshape(slab)[:, i] for i in range(2))
    q_lo, q_hi = (q.reshape(slab)[:, i] for i in range(2))
    v_lo, v_hi = (v.reshape(slab)[:, i] for i in range(2))
    sidx = lax.broadcasted_iota(I32, (nblk, half, W_C), 1)

    def head_sum(x):
        return _dot(x.reshape(hrows, W_C).astype(BF16), segs_ref[...]).reshape(nblk, half, W_C)

    def slab_sum(x):
        return _dot(rsum_ref[...], x.reshape(hrows, W_C).astype(BF16)).reshape(nblk, half, W_C)

    o_lo = jnp.zeros((nblk, half, W_C), F32)
    o_hi = jnp.zeros((nblk, half, W_C), F32)
    for t in range(half):
        bt, qt = b_lo[:, t:t + 1, :], q_lo[:, t:t + 1, :]
        decay = jnp.where(sidx <= t, jnp.exp(jnp.minimum(bt - b_lo, 0.0)), 0.0)
        o_lo = jnp.where(sidx == t, slab_sum(head_sum(decay * k_lo * qt) * v_lo), o_lo)
    for t in range(half):
        bt, qt = b_hi[:, t:t + 1, :], q_hi[:, t:t + 1, :]
        d_lo = jnp.exp(jnp.minimum(bt - b_lo, 0.0))
        d_hi = jnp.where(sidx <= t, jnp.exp(jnp.minimum(bt - b_hi, 0.0)), 0.0)
        av = head_sum(d_lo * k_lo * qt) * v_lo + head_sum(d_hi * k_hi * qt) * v_hi
        o_hi = jnp.where(sidx == t, slab_sum(av), o_hi)
    o_intra = jnp.stack([o_lo, o_hi], axis=1).reshape(tc, W_C)

    blk = lambda a, j: a[j * REC_BLOCK:(j + 1) * REC_BLOCK, :]
    upd = [_dot_tn(blk(v_bf, j), blk(ke, j)) * bd_ref[...] for j in range(nblk)]
    s = s_scr[...]
    s_before = []
    for j in range(nblk):
        s_before.append(s.astype(BF16))
        s = dl[j] * s + upd[j]
    s_scr[...] = s
    o_inter = jnp.concatenate([_dot_nt(blk(qe, j), s_before[j]) for j in range(nblk)], axis=0)

    o = o_intra + o_inter
    ms = _dot((o * o).astype(BF16), segm_ref[...])
    oc_ref[0] = (o * lax.rsqrt(ms + EPS) * ghg_ref[...] * (gc * _sigmoid(gc))).astype(BF16)

    @pl.when(pl.program_id(1) == pl.num_programs(1) - 1)
    def _():
        sout_ref[0] = s_scr[...]


def _hgrn(hraw3, lbc, ghg, segm, segs, bd, s0, tc, shared_past):
    nb, t, _ = hraw3.shape
    sidx = (lambda bi, ti: (0, 0, 0)) if shared_past else (lambda bi, ti: (bi, 0, 0))
    hrows = tc // 2
    rsum = _seg_matrix(hrows, REC_BLOCK // 2, 1.0, BF16)
    return pl.pallas_call(
        functools.partial(_hgrn_kernel, tc=tc),
        grid=(nb, t // tc),
        in_specs=[pl.BlockSpec((1, tc, 4 * W_C), lambda bi, ti: (bi, ti, 0)),
                  _const_spec((SUBLANES, W_C)), _const_spec((1, W_C)),
                  _const_spec((W_C, W_C)), _const_spec((W_C, W_C)), _const_spec((W_C, W_C)),
                  _const_spec((hrows, hrows)),
                  pl.BlockSpec((1, W_C, W_C), sidx)],
        out_specs=[pl.BlockSpec((1, tc, W_C), lambda bi, ti: (bi, ti, 0)),
                   pl.BlockSpec((1, W_C, W_C), lambda bi, ti: (bi, 0, 0))],
        out_shape=[jax.ShapeDtypeStruct((nb, t, W_C), BF16),
                   jax.ShapeDtypeStruct((nb, W_C, W_C), F32)],
        scratch_shapes=[pltpu.VMEM((W_C, W_C), F32)],
        compiler_params=pltpu.CompilerParams(dimension_semantics=("arbitrary", "arbitrary")),
        name="hgrn2",
    )(hraw3, lbc, ghg, segm, segs, bd, rsum, s0)


def _split_halves(q):
    lane = lax.broadcasted_iota(I32, q.shape, 1)
    zero = jnp.zeros_like(q)
    return jnp.concatenate([jnp.where(lane < DQK, q, zero), jnp.where(lane >= DQK, q, zero)], axis=0)


def _twice(b):
    return jnp.concatenate([b, b], axis=0)


def _diff_combine(acc, l, t, lam_init, lq1, lk1, lq2, lk2, gd):
    o = acc / l
    lam = (jnp.exp(jnp.sum(lq1 * lk1, axis=1, keepdims=True))
           - jnp.exp(jnp.sum(lq2 * lk2, axis=1, keepdims=True)) + lam_init)
    a = o[:t] - lam * o[t:]
    return (_rms(a, gd) * (1.0 - lam_init)).astype(BF16)


def _attn_main_kernel(sc_ref, farc_ref, q_ref, k_ref, vt_ref, km_ref, vmt_ref, bt_ref, btm_ref,
                      lq1_ref, lk1_ref, lq2_ref, lk2_ref, gdt_ref, o_ref, acc_scr, *, tq):
    h = pl.program_id(1)
    i = pl.program_id(2)
    q = q_ref[0]
    lane = lax.broadcasted_iota(I32, q.shape, 1)
    zero = jnp.zeros_like(q)
    q_half = (jnp.where(lane < DQK, q, zero), jnp.where(lane >= DQK, q, zero))

    def scores(j):
        r0 = pl.multiple_of(j * tq, tq)
        bias = bt_ref[jnp.minimum(i - j, 2), 0]
        k_tile = k_ref[0, pl.ds(r0, tq), :]
        return tuple(_dot_nt(k_tile, qh) + bias for qh in q_half)

    def update(j, c, s, m_prev, l_prev, s_extra=None):
        r0 = pl.multiple_of(j * tq, tq)
        m_new = jnp.maximum(m_prev, jnp.max(s, axis=0, keepdims=True))
        if s_extra is not None:
            m_new = jnp.maximum(m_new, jnp.max(s_extra, axis=0, keepdims=True))
        alpha = jnp.exp2(m_prev - m_new)
        p = jnp.exp2(s - m_new)
        l_new = alpha * l_prev + jnp.sum(p, axis=0, keepdims=True)
        pv = _dot(vt_ref[:, pl.ds(r0, tq)], p.astype(BF16))
        if s_extra is not None:
            p_extra = jnp.exp2(s_extra - m_new)
            l_new = l_new + jnp.sum(p_extra, axis=0, keepdims=True)
            pv = pv + _dot(vmt_ref[...], p_extra.astype(BF16))
        acc_scr[c] = alpha * acc_scr[c] + pv
        return m_new, l_new

    def body(j, carry):
        s_cur, m_prev, l_prev = carry
        s_next = scores(j + 1)
        stats = [update(j, c, s_cur[c], m_prev[c], l_prev[c]) for c in range(2)]
        return s_next, tuple(st[0] for st in stats), tuple(st[1] for st in stats)

    acc_scr[...] = jnp.zeros(acc_scr.shape, F32)
    m0 = (jnp.full((1, tq), -jnp.inf, F32),) * 2
    l0 = (jnp.zeros((1, tq), F32),) * 2
    s_last, m_prev, l_prev = lax.fori_loop(0, i, body, (scores(0), m0, l0))
    bias_meta = jnp.where(i == 0, btm_ref[0], farc_ref[h])
    o = []
    for c in range(2):
        s_meta = _dot_nt(km_ref[0], q_half[c]) + bias_meta
        _, l_fin = update(i, c, s_last[c], m_prev[c], l_prev[c], s_extra=s_meta)
        o.append(acc_scr[c] * (1.0 / l_fin))

    lam_init = sc_ref[0]
    lam = (jnp.exp(jnp.sum(lq1_ref[...] * lk1_ref[...], axis=1, keepdims=True))
           - jnp.exp(jnp.sum(lq2_ref[...] * lk2_ref[...], axis=1, keepdims=True)) + lam_init)
    a = o[0] - lam * o[1]
    gain = jnp.concatenate([gdt_ref[...]] * (tq // LANES), axis=1)
    y = a * lax.rsqrt(jnp.mean(a * a, axis=0, keepdims=True) + EPS) * gain * (1.0 - lam_init)
    o_ref[0] = y.T.astype(BF16)


def _attn_main(q3, k3, vt, km, vmt, bt, btm, farc, lam_init, lq1, lk1, lq2, lk2, gdt, tq):
    nb, t, _ = q3.shape
    p = km.shape[1]
    assert tq >= MAX_DIST and tq % CHUNK == 0, "tiles two or more below the diagonal must be all-far"
    smem = pl.BlockSpec(memory_space=pltpu.SMEM)
    vec = lambda: _const_spec((1, DQK))
    return pl.pallas_call(
        functools.partial(_attn_main_kernel, tq=tq),
        grid=(nb, H_A, t // tq),
        in_specs=[smem, smem,
                  pl.BlockSpec((1, tq, DVA), lambda b, h, i: (b, i, h)),
                  pl.BlockSpec((1, t, DVA), lambda b, h, i: (b, 0, h)),
                  pl.BlockSpec((DVA, t), lambda b, h, i: (h, b)),
                  pl.BlockSpec((1, p, DVA), lambda b, h, i: (0, 0, h)),
                  pl.BlockSpec((DVA, p), lambda b, h, i: (h, 0)),
                  pl.BlockSpec((3, 1, tq, tq), lambda b, h, i: (0, h, 0, 0)),
                  pl.BlockSpec((1, p, tq), lambda b, h, i: (h, 0, 0)),
                  vec(), vec(), vec(), vec(),
                  pl.BlockSpec((DVA, LANES), lambda b, h, i: (h, 0))],
        out_specs=pl.BlockSpec((1, tq, DVA), lambda b, h, i: (b, i, h)),
        out_shape=jax.ShapeDtypeStruct((nb, t, W_A), BF16),
        scratch_shapes=[pltpu.VMEM((2, DVA, tq), F32)],
        compiler_params=pltpu.CompilerParams(dimension_semantics=("arbitrary",) * 3),
        name="attn_main",
    )(lam_init, farc, q3, k3, vt, km, vmt, bt, btm, lq1, lk1, lq2, lk2, gdt)


def _attn_small_kernel(sc_ref, q_ref, kc_ref, vc_ref, kn_ref, vn_ref, bp_ref, bn_ref,
                       lq1_ref, lk1_ref, lq2_ref, lk2_ref, gd_ref, o_ref, *, t, n_cached):
    b = pl.program_id(0)
    p = vc_ref.shape[2] // H_A
    past_bias = jnp.where(b < n_cached, 0.0, NEG)
    q_all = q_ref[0]
    kn_all = kn_ref[0]
    vn_all = vn_ref[0]
    outs = []
    for hd in range(H_A):
        cols = slice(hd * DVA, (hd + 1) * DVA)
        qq = _split_halves(q_all[:, cols])
        s_p = _dot(qq, kc_ref[0, 0, cols, :].astype(BF16)) + _twice(bp_ref[hd]) + past_bias
        s_n = _dot_nt(qq, kn_all[:, cols]) + _twice(bn_ref[hd])
        m = jnp.maximum(jnp.max(s_p, axis=1, keepdims=True), jnp.max(s_n, axis=1, keepdims=True))
        p_p = jnp.exp2(s_p - m)
        p_n = jnp.exp2(s_n - m)
        l = jnp.sum(p_p, axis=1, keepdims=True) + jnp.sum(p_n, axis=1, keepdims=True)
        v_past = vc_ref[0, 0, pl.ds(hd, p, stride=H_A), :].astype(BF16)
        acc = _dot(p_p.astype(BF16), v_past) + _dot(p_n.astype(BF16), vn_all[:, cols])
        outs.append(_diff_combine(acc, l, t, sc_ref[0], lq1_ref[...], lk1_ref[...],
                                  lq2_ref[...], lk2_ref[...], gd_ref[:, cols]))
    o_ref[0] = jnp.concatenate(outs, axis=1)


def _attn_small(q3, cache_kt, cache_v4, layer, kn, vn, bp, bn, lam_init, lq1, lk1, lq2, lk2, gdiff):
    nb, t, _ = q3.shape
    n_cached, p = cache_kt.shape[1], cache_kt.shape[3]
    smem = pl.BlockSpec(memory_space=pltpu.SMEM)
    vec = lambda: _const_spec((1, DQK))
    cached = lambda b: (layer, jnp.minimum(b, n_cached - 1), 0, 0)
    new = pl.BlockSpec((1, t, W_A), lambda b: (b, 0, 0))
    return pl.pallas_call(
        functools.partial(_attn_small_kernel, t=t, n_cached=n_cached),
        grid=(nb,),
        in_specs=[smem, new,
                  pl.BlockSpec((1, 1, W_A, p), cached), pl.BlockSpec((1, 1, p * H_A, DVA), cached),
                  new, new, _const_spec((H_A, t, p)), _const_spec((H_A, t, t)),
                  vec(), vec(), vec(), vec(), _const_spec((1, W_A))],
        out_specs=new,
        out_shape=jax.ShapeDtypeStruct((nb, t, W_A), BF16),
        compiler_params=pltpu.CompilerParams(dimension_semantics=("arbitrary",)),
        name="attn_small",
    )(lam_init, q3, cache_kt, cache_v4, kn, vn, bp, bn, lq1, lk1, lq2, lk2, gdiff)


_FF_CHUNK = 256


def _ffn_kernel(x_ref, oa_ref, ob_ref, oc_ref, wout_ref, gffn_ref, wup_ref, cw_ref, cb_ref, wdn_ref,
                past_ref, gfin_ref, y_ref, new_ref, halo_scr, *, bb, tt, final_norm):
    rows = bb * tt
    taps = FFN_CONV_W - 1

    @pl.when(pl.program_id(1) == 0)
    def _():
        halo_scr[...] = past_ref[...]

    x1 = (x_ref[...].reshape(rows, D_MODEL)
          + _dot(oa_ref[...].reshape(rows, W_A), wout_ref[0:W_A, :])
          + _dot(ob_ref[...].reshape(rows, C_CONV), wout_ref[W_A:W_A + C_CONV, :])
          + _dot(oc_ref[...].reshape(rows, W_C), wout_ref[W_A + C_CONV:D_MODEL, :]))
    h2 = _rms(x1, gffn_ref[...]).astype(BF16)

    tpos = lax.broadcasted_iota(I32, (bb, tt, _FF_CHUNK), 1)
    acc = jnp.zeros((rows, D_MODEL), F32)
    for c in range(D_FF // _FF_CHUNK):
        lo, hi = c * _FF_CHUNK, (c + 1) * _FF_CHUNK
        gate = _dot(h2, wup_ref[:, lo:hi])
        val = _dot(h2, wup_ref[:, D_FF + lo:D_FF + hi])
        gate3 = gate.reshape(bb, tt, _FF_CHUNK)
        conv = cw_ref[taps:taps + 1, lo:hi] * gate + cb_ref[:, lo:hi]
        for d in range(1, taps + 1):
            shifted = pltpu.roll(gate, d, axis=0).reshape(bb, tt, _FF_CHUNK)
            for e in range(d):
                shifted = jnp.where(tpos == e, halo_scr[:, taps - d + e:taps - d + e + 1, lo:hi], shifted)
            conv = conv + cw_ref[taps - d:taps - d + 1, lo:hi] * shifted.reshape(rows, _FF_CHUNK)
        halo_scr[:, :, lo:hi] = gate3[:, tt - taps:tt, :]
        gelu = 0.5 * conv * (1.0 + jnp.tanh(math.sqrt(2.0 / math.pi) * (conv + 0.044715 * (conv * conv * conv))))
        acc = acc + _dot((gelu * val).astype(BF16), wdn_ref[lo:hi, :])

    y = x1 + acc
    if final_norm:
        y = _rms(y, gfin_ref[...])
    y_ref[...] = y.reshape(bb, tt, D_MODEL)
    new_ref[...] = halo_scr[...]


def _ffn(x3, oa3, ob3, oc3, wout, gffn, wup, cw, cb, wdn, past, gfin, bb, tt, shared_past, final_norm):
    nb, t, _ = x3.shape
    taps = FFN_CONV_W - 1
    assert tt >= taps
    tok = lambda w: pl.BlockSpec((bb, tt, w), lambda bi, ti: (bi, ti, 0))
    pidx = (lambda bi, ti: (0, 0, 0)) if shared_past else (lambda bi, ti: (bi, 0, 0))
    assert not (shared_past and bb != 1)
    return pl.pallas_call(
        functools.partial(_ffn_kernel, bb=bb, tt=tt, final_norm=final_norm),
        grid=(nb // bb, t // tt),
        in_specs=[tok(D_MODEL), tok(W_A), tok(C_CONV), tok(W_C),
                  _const_spec((D_MODEL, D_MODEL)), _const_spec((1, D_MODEL)),
                  _const_spec((D_MODEL, 2 * D_FF)), _const_spec((FFN_CONV_W, D_FF)), _const_spec((1, D_FF)),
                  _const_spec((D_FF, D_MODEL)),
                  pl.BlockSpec((bb, taps, D_FF), pidx), _const_spec((1, D_MODEL))],
        out_specs=[tok(D_MODEL), pl.BlockSpec((bb, taps, D_FF), lambda bi, ti: (bi, 0, 0))],
        out_shape=[jax.ShapeDtypeStruct((nb, t, D_MODEL), F32),
                   jax.ShapeDtypeStruct((nb, taps, D_FF), F32)],
        scratch_shapes=[pltpu.VMEM((bb, taps, D_FF), F32)],
        compiler_params=pltpu.CompilerParams(dimension_semantics=("arbitrary", "arbitrary"),
                                             vmem_limit_bytes=VMEM_LIMIT),
        name="outproj_ffn",
    )(x3, oa3, ob3, oc3, wout, gffn, wup, cw, cb, wdn, past, gfin)


def _seg_matrix(width, group, value, dtype):
    g = jnp.arange(width, dtype=I32) // group
    return jnp.where(g[:, None] == g[None, :], value, 0.0).astype(dtype)


def _state_to_kernel(s):
    n = s.shape[0]
    eye = jnp.eye(H_C, dtype=s.dtype)
    full = jnp.einsum('nhdv,hg->nhvgd', s, eye)
    return full.reshape(n, W_C, H_C * DK_C)


def _state_from_kernel(sf):
    n = sf.shape[0]
    s5 = sf.reshape(n, H_C, DV_C, H_C, DK_C)
    diag = jnp.stack([s5[:, h, :, h, :] for h in range(H_C)], axis=1)
    return jnp.swapaxes(diag, 2, 3)


def _pick_tile(total, want):
    t = min(total, want)
    assert total % t == 0, (total, want)
    return t


def kernel(x_prompt, x_sample, cache_k, cache_v, state_conv, state_hgrn, state_ffn, meta_tokens, rel_bias, g_mix, w_in, g_q, g_k, lam_q1, lam_k1, lam_q2, lam_k2, g_diff, conv_w, conv_b, ln_g, ln_b, lb_logits, g_hgrn, w_out, g_ffn, w_up, ffn_conv_w, ffn_conv_b, w_down, g_final):
    depth = g_mix.shape[0]
    bp, seq, _ = x_prompt.shape
    bs, ts, _ = x_sample.shape
    past_len = cache_k.shape[2]
    assert ts == N_META == REC_BLOCK and meta_tokens.shape[0] == N_META
    assert seq % CHUNK == 0 and past_len % CHUNK == 0

    lb_all = jnp.cumsum(jax.nn.softmax(lb_logits.astype(F32), axis=0), axis=0)
    lb_all = lb_all - lb_all[0:1]
    lb_consts = jnp.stack([jnp.log(lb_all), jnp.log1p(-lb_all), 1.0 - lb_all]
                          + [jnp.zeros_like(lb_all)] * (SUBLANES - 3), axis=1)

    seg_qk = _seg_matrix(W_A, DQK, 1.0 / DQK, BF16)
    seg_mean = _seg_matrix(W_C, DV_C, 1.0 / DV_C, BF16)
    seg_sum = _seg_matrix(W_C, DK_C, 1.0, BF16)
    blockdiag = _seg_matrix(W_C, DK_C, 1.0, F32)

    tq = _pick_tile(seq, 512)
    bt_main = jnp.stack([_bias_tile(rel_bias, tq, tq, N_META + d * tq, N_META, True, keys_on_rows=True)
                         for d in range(3)])
    bt_meta = _bias_tile(rel_bias, N_META, tq, N_META, 0, True, keys_on_rows=True)
    far_bias = rel_bias[_NB - 1, :] * LOG2E
    bias_past = _bias_tile(rel_bias, ts, past_len, past_len, 0, False)
    bias_new = _bias_tile(rel_bias, ts, ts, past_len, past_len, False)

    cache_kt = jnp.swapaxes(cache_k.reshape(depth, bs, past_len, W_A), 2, 3)
    cache_v4 = cache_v.reshape(depth, bs, past_len * H_A, DVA)

    xs = jnp.concatenate([x_sample, meta_tokens.astype(x_sample.dtype)[None]], axis=0)
    xm = x_prompt
    nbs = bs + 1
    zrow = lambda a: jnp.zeros((1,) + a.shape[1:], a.dtype)

    tm_main = _pick_tile(bp * seq, 512)
    outs = {k: [] for k in ("kp", "vp", "cp", "hp", "fp", "ks", "vs", "cs", "hs", "fs")}
    for l in range(depth):
        last = l == depth - 1
        lam_init = jnp.full((1,), 0.8 - 0.6 * math.exp(-0.3 * l), F32)
        win = w_in[l].astype(BF16)
        wout = w_out[l].astype(BF16)
        wup = w_up[l].astype(BF16)
        wdn = w_down[l].astype(BF16)
        gq = jnp.tile(g_q[l], 2 * H_A)[None]
        gk = jnp.tile(g_k[l], 2 * H_A)[None]
        row = lambda a: a[l][None]
        common_attn = (lam_init, row(lam_q1), row(lam_k1), row(lam_q2), row(lam_k2), row(g_diff))

        def mixer_front(x3, tm, feature_major):
            nb, t, _ = x3.shape
            q, kb, vf4, glu, hraw, k_extra, v_extra = _inproj(
                x3.reshape(nb * t, D_MODEL), row(g_mix), win, gq, gk, seg_qk, tm, t, feature_major)
            r3 = lambda a: a.reshape(nb, t, a.shape[-1])
            return (r3(q), r3(kb), vf4.reshape(nb, t, H_A, DVA), r3(glu), r3(hraw),
                    k_extra if feature_major else r3(k_extra), v_extra if feature_major else r3(v_extra))

        q, kb, vf_s, glu, hraw, kf_s, vb = mixer_front(xs, nbs * ts, False)
        oa = _attn_small(q, cache_kt, cache_v4, l, kb, vb, bias_past, bias_new, *common_attn)
        ob, conv_s = _conv_module(glu, jnp.concatenate([state_conv[l], zrow(state_conv[l])]),
                                  conv_w[l], row(conv_b), row(ln_g), row(ln_b), ts, False)
        oc, hg_s = _hgrn(hraw, lb_consts[l], row(g_hgrn), seg_mean, seg_sum, blockdiag,
                         _state_to_kernel(jnp.concatenate([state_hgrn[l], zrow(state_hgrn[l])]).astype(F32)),
                         ts, False)
        xs, ffn_s = _ffn(xs, oa, ob, oc, wout, row(g_ffn), wup, ffn_conv_w[l], row(ffn_conv_b), wdn,
                         jnp.concatenate([state_ffn[l], zrow(state_ffn[l])]), g_final[None],
                         nbs, ts, False, last)
        meta_k, meta_vt = kb[bs:], vb[bs].T
        gdt = jnp.broadcast_to(g_diff[l][:, None], (W_A, LANES))

        q, kb, vf_m, glu, hraw, kft_m, vt = mixer_front(xm, tm_main, True)
        oa = _attn_main(q, kb, vt, meta_k, meta_vt, bt_main, bt_meta, far_bias, *common_attn[:-1], gdt, tq)
        ob, conv_m = _conv_module(glu, conv_s[bs:], conv_w[l], row(conv_b), row(ln_g), row(ln_b),
                                  _pick_tile(seq, 512), True)
        oc, hg_m = _hgrn(hraw, lb_consts[l], row(g_hgrn), seg_mean, seg_sum, blockdiag, hg_s[bs:],
                         _pick_tile(seq, 256), True)
        xm, ffn_m = _ffn(xm, oa, ob, oc, wout, row(g_ffn), wup, ffn_conv_w[l], row(ffn_conv_b), wdn,
                         ffn_s[bs:], g_final[None], 1, _pick_tile(seq, 512), True, last)

        meta_rows = lambda a: jnp.broadcast_to(a[bs:], (bp,) + a.shape[1:])
        kt_p = jnp.concatenate([meta_rows(jnp.swapaxes(kf_s, 1, 2)), kft_m], axis=2)
        outs["kp"].append(jnp.swapaxes(kt_p, 1, 2).reshape(bp, N_META + seq, H_A, 2, DQK))
        outs["vp"].append(jnp.concatenate([meta_rows(vf_s), vf_m], axis=1))
        outs["cp"].append(conv_m)
        outs["hp"].append(_state_from_kernel(hg_m))
        outs["fp"].append(ffn_m)
        outs["ks"].append(kf_s[:bs].reshape(bs, ts, H_A, 2, DQK))
        outs["vs"].append(vf_s[:bs])
        outs["cs"].append(conv_s[:bs])
        outs["hs"].append(_state_from_kernel(hg_s[:bs]))
        outs["fs"].append(ffn_s[:bs])

    st = {k: jnp.stack(v) for k, v in outs.items()}
    return (xm, xs[:bs], st["kp"], st["vp"], st["cp"], st["hp"], st["fp"],
            st["ks"], st["vs"], st["cs"], st["hs"], st["fs"])
```

```python
import functools
import math

import jax
import jax.numpy as jnp
from jax import lax
from jax.experimental import pallas as pl
from jax.experimental.pallas import tpu as pltpu

F32 = jnp.float32
BF16 = jnp.bfloat16
I32 = jnp.int32

D_MODEL = 1024
N_META = 16
CHUNK = 64
DQK = 64
DVA = 2 * DQK
H_A = 4
W_A = H_A * DVA
C_CONV = 256
CONV_W = 31
H_C = 4
DK_C = 64
DV_C = 64
W_C = H_C * DV_C
D_FF = 2816
FFN_CONV_W = 3
N_BUCKETS = 32
MAX_DIST = 128
REC_BLOCK = 16
EPS = 1e-6
NEG = -1e30
LOG2E = math.log2(math.e)
IN_COLS = 2 * W_A + W_A + 2 * C_CONV + 4 * W_C

LANES = 128
SUBLANES = 8
VMEM_LIMIT = 56 * 1024 * 1024

_NB = N_BUCKETS // 2
_MAX_EXACT = _NB // 2
_BUCKET_THR = tuple(
    next(n for n in range(_MAX_EXACT, MAX_DIST + 1)
         if n ** (_NB - _MAX_EXACT) * _MAX_EXACT ** k >= _MAX_EXACT ** (_NB - _MAX_EXACT) * MAX_DIST ** k)
    for k in range(1, _NB - _MAX_EXACT))


def _rms(x, g):
    return x * lax.rsqrt(jnp.mean(x * x, axis=-1, keepdims=True) + EPS) * g


def _dot(a, b):
    return jnp.dot(a, b, preferred_element_type=F32)


def _dot_nt(a, b):
    return lax.dot_general(a, b, (((1,), (1,)), ((), ())), preferred_element_type=F32)


def _dot_tn(a, b):
    return lax.dot_general(a, b, (((0,), (0,)), ((), ())), preferred_element_type=F32)


def _sigmoid(x):
    return jax.nn.sigmoid(x)


def _const_spec(shape):
    nd = len(shape)
    return pl.BlockSpec(shape, lambda *_: (0,) * nd, pipeline_mode=pl.Buffered(1))


def _bias_kernel(tab_ref, o_ref, *, qpos0, kpos0, prompt_chunks, keys_on_rows):
    h = pl.program_id(0)
    shape = o_ref.shape[1:]
    r = lax.broadcasted_iota(I32, shape, 1 if keys_on_rows else 0) + qpos0
    c = lax.broadcasted_iota(I32, shape, 0 if keys_on_rows else 1) + kpos0
    rel = c - r
    n = jnp.abs(rel)
    large = jnp.full(shape, _MAX_EXACT, I32)
    for thr in _BUCKET_THR:
        large = large + (n >= thr).astype(I32)
    bucket = jnp.where(rel > 0, _NB, 0) + jnp.where(n < _MAX_EXACT, n, large)
    out = jnp.zeros(shape, F32)
    for b in range(N_BUCKETS):
        out = jnp.where(bucket == b, tab_ref[b, h] * LOG2E, out)
    shift = CHUNK - N_META if prompt_chunks else 0
    log2_chunk = CHUNK.bit_length() - 1
    visible = ((c + shift) >> log2_chunk) <= ((r + shift) >> log2_chunk)
    o_ref[0] = jnp.where(visible, out, NEG)


def _bias_tile(rel_bias, rows, cols, qpos0, kpos0, prompt_chunks, keys_on_rows=False):
    return pl.pallas_call(
        functools.partial(_bias_kernel, qpos0=qpos0, kpos0=kpos0, prompt_chunks=prompt_chunks,
                          keys_on_rows=keys_on_rows),
        grid=(H_A,),
        in_specs=[pl.BlockSpec(memory_space=pltpu.SMEM)],
        out_specs=pl.BlockSpec((1, rows, cols), lambda h: (h, 0, 0)),
        out_shape=jax.ShapeDtypeStruct((H_A, rows, cols), F32),
        name="relpos_bias",
    )(rel_bias)


def _inproj_kernel(x_ref, gmix_ref, win_ref, gq_ref, gk_ref, seg_ref,
                   q_ref, kb_ref, vf4_ref, glu_ref, hraw_ref, k_extra_ref, v_extra_ref, *, feature_major):
    tm = x_ref.shape[0]
    h = _rms(x_ref[...], gmix_ref[...]).astype(BF16)

    def proj(lo, hi):
        return _dot(h, win_ref[:, lo:hi])

    def group_norm(a, g):
        ms = _dot((a * a).astype(BF16), seg_ref[...])
        return a * lax.rsqrt(ms + EPS) * g

    qa = group_norm(proj(0, W_A), gq_ref[...])
    q_ref[...] = (qa * (DQK ** -0.5 * LOG2E)).astype(BF16)
    ka = group_norm(proj(W_A, 2 * W_A), gk_ref[...])
    kb_ref[...] = ka.astype(BF16)
    va = proj(2 * W_A, 3 * W_A)
    for hd in range(H_A):
        vf4_ref[pl.ds(hd, tm, stride=H_A), :] = va[:, hd * DVA:(hd + 1) * DVA]
    if feature_major:
        k_extra_ref[0] = ka.T
        v_extra_ref[...] = va.T.astype(BF16)
    else:
        k_extra_ref[...] = ka
        v_extra_ref[...] = va.astype(BF16)
    glu_ref[...] = proj(3 * W_A, 3 * W_A + 2 * C_CONV)
    hraw_ref[...] = proj(3 * W_A + 2 * C_CONV, IN_COLS)


def _inproj(x2d, gmix, win, gq, gk, seg, tm, rows_per_seq, feature_major):
    n = x2d.shape[0]
    row = lambda w: pl.BlockSpec((tm, w), lambda i: (i, 0))
    outs = [(W_A, BF16), (W_A, BF16), None, (2 * C_CONV, F32), (4 * W_C, F32)]
    out_specs = [row(o[0]) if o else pl.BlockSpec((tm * H_A, DVA), lambda i: (i, 0)) for o in outs]
    out_shape = [jax.ShapeDtypeStruct((n, o[0]), o[1]) if o else jax.ShapeDtypeStruct((n * H_A, DVA), F32)
                 for o in outs]
    if feature_major:
        nt = rows_per_seq // tm
        out_specs += [pl.BlockSpec((1, W_A, tm), lambda i: (i // nt, 0, i % nt)),
                      pl.BlockSpec((W_A, tm), lambda i: (0, i))]
        out_shape += [jax.ShapeDtypeStruct((n // rows_per_seq, W_A, rows_per_seq), F32),
                      jax.ShapeDtypeStruct((W_A, n), BF16)]
    else:
        out_specs += [row(W_A), row(W_A)]
        out_shape += [jax.ShapeDtypeStruct((n, W_A), F32), jax.ShapeDtypeStruct((n, W_A), BF16)]
    return pl.pallas_call(
        functools.partial(_inproj_kernel, feature_major=feature_major),
        grid=(n // tm,),
        in_specs=[row(D_MODEL), _const_spec((1, D_MODEL)), _const_spec((D_MODEL, IN_COLS)),
                  _const_spec((1, W_A)), _const_spec((1, W_A)), _const_spec((W_A, W_A))],
        out_specs=out_specs,
        out_shape=out_shape,
        compiler_params=pltpu.CompilerParams(dimension_semantics=("arbitrary",), vmem_limit_bytes=VMEM_LIMIT),
        name="inproj",
    )(x2d, gmix, win, gq, gk, seg)


_CONV_PAD = 32


def _conv_kernel(glu_ref, past_ref, w_ref, b_ref, lng_ref, lnb_ref, ob_ref, new_ref, xc_ref, *, tt, rc):
    halo = CONV_W - 1
    h0 = _CONV_PAD - halo

    @pl.when(pl.program_id(1) == 0)
    def _():
        xc_ref[h0:_CONV_PAD, :] = past_ref[0]

    glu = glu_ref[0]
    xc_ref[_CONV_PAD:_CONV_PAD + tt, :] = glu[:, :C_CONV] * _sigmoid(glu[:, C_CONV:])
    bias = b_ref[...]
    for c in range(tt // rc):
        acc = jnp.zeros((rc, C_CONV), F32) + bias
        for j in range(CONV_W):
            acc = acc + w_ref[j:j + 1, :] * xc_ref[c * rc + j + h0:c * rc + j + h0 + rc, :]
        mu = jnp.mean(acc, axis=-1, keepdims=True)
        xm = acc - mu
        var = jnp.mean(xm * xm, axis=-1, keepdims=True)
        y = xm * lax.rsqrt(var + EPS) * lng_ref[...] + lnb_ref[...]
        ob_ref[0, c * rc:(c + 1) * rc, :] = (y * _sigmoid(y)).astype(BF16)
    new = xc_ref[h0 + tt:_CONV_PAD + tt, :]
    new_ref[0] = new
    xc_ref[h0:_CONV_PAD, :] = new


def _conv_module(glu3, past, w, b, lng, lnb, tt, shared_past):
    nb, t, _ = glu3.shape
    rc = min(tt, 128)
    pidx = (lambda bi, ti: (0, 0, 0)) if shared_past else (lambda bi, ti: (bi, 0, 0))
    return pl.pallas_call(
        functools.partial(_conv_kernel, tt=tt, rc=rc),
        grid=(nb, t // tt),
        in_specs=[pl.BlockSpec((1, tt, 2 * C_CONV), lambda bi, ti: (bi, ti, 0)),
                  pl.BlockSpec((1, CONV_W - 1, C_CONV), pidx),
                  _const_spec((CONV_W, C_CONV)), _const_spec((1, C_CONV)),
                  _const_spec((1, C_CONV)), _const_spec((1, C_CONV))],
        out_specs=[pl.BlockSpec((1, tt, C_CONV), lambda bi, ti: (bi, ti, 0)),
                   pl.BlockSpec((1, CONV_W - 1, C_CONV), lambda bi, ti: (bi, 0, 0))],
        out_shape=[jax.ShapeDtypeStruct((nb, t, C_CONV), BF16),
                   jax.ShapeDtypeStruct((nb, CONV_W - 1, C_CONV), F32)],
        scratch_shapes=[pltpu.VMEM((_CONV_PAD + tt, C_CONV), F32)],
        compiler_params=pltpu.CompilerParams(dimension_semantics=("arbitrary", "arbitrary")),
        name="conv_module",
    )(glu3, past, w, b, lng, lnb)


def _hgrn_kernel(hraw_ref, lbc_ref, ghg_ref, segm_ref, segs_ref, bd_ref, rsum_ref, s0_ref,
                 oc_ref, sout_ref, s_scr, *, tc):
    nblk = tc // REC_BLOCK
    blk3 = (nblk, REC_BLOCK, W_C)

    @pl.when(pl.program_id(1) == 0)
    def _():
        s_scr[...] = s0_ref[0]

    hr = hraw_ref[0]
    z = hr[:, 0:W_C]
    v = hr[:, W_C:2 * W_C]
    q = hr[:, 2 * W_C:3 * W_C]
    gc = hr[:, 3 * W_C:4 * W_C]
    log_lb = lbc_ref[0:1, :]
    log_1m_lb = lbc_ref[1:2, :]
    one_m_lb = lbc_ref[2:3, :]

    log_sig = jnp.minimum(z, 0.0) - jnp.log(1.0 + jnp.exp(-jnp.abs(z)))
    b_arg = log_1m_lb + log_sig
    log_f = jnp.maximum(log_lb, b_arg) + jnp.log(1.0 + jnp.exp(-jnp.abs(log_lb - b_arg)))
    kc = one_m_lb * _sigmoid(-z)

    pos = lax.broadcasted_iota(I32, (tc, W_C), 0) & (REC_BLOCK - 1)
    bcum = log_f
    sh = 1
    while sh < REC_BLOCK:
        bcum = bcum + jnp.where(pos >= sh, pltpu.roll(bcum, sh, axis=0), 0.0)
        sh *= 2

    b3 = bcum.reshape(blk3)
    b_last = b3[:, REC_BLOCK - 1:REC_BLOCK, :]
    qe = (q * jnp.exp(bcum)).astype(BF16)
    ke = (kc.reshape(blk3) * jnp.exp(b_last - b3)).reshape(tc, W_C).astype(BF16)
    v_bf = v.astype(BF16)
    dl = jnp.exp(b_last)

    half = REC_BLOCK // 2
    slab = (nblk, 2, half, W_C)
    hrows = nblk * half
    b_lo, b_hi = (bcum.reshape(slab)[:, i] for i in range(2))
    k_lo, k_hi = (kc.reshape(slab)[:, i] for i in range(2))
    q_lo, q_hi = (q.reshape(slab)[:, i] for i in range(2))
    v_lo, v_hi = (v.reshape(slab)[:, i] for i in range(2))
    sidx = lax.broadcasted_iota(I32, (nblk, half, W_C), 1)

    def head_sum(x):
        return _dot(x.reshape(hrows, W_C).astype(BF16), segs_ref[...]).reshape(nblk, half, W_C)

    def slab_sum(x):
        return _dot(rsum_ref[...], x.reshape(hrows, W_C).astype(BF16)).reshape(nblk, half, W_C)

    o_lo = jnp.zeros((nblk, half, W_C), F32)
    o_hi = jnp.zeros((nblk, half, W_C), F32)
    for t in range(half):
        bt, qt = b_lo[:, t:t + 1, :], q_lo[:, t:t + 1, :]
        decay = jnp.where(sidx <= t, jnp.exp(jnp.minimum(bt - b_lo, 0.0)), 0.0)
        o_lo = jnp.where(sidx == t, slab_sum(head_sum(decay * k_lo * qt) * v_lo), o_lo)
    for t in range(half):
        bt, qt = b_hi[:, t:t + 1, :], q_hi[:, t:t + 1, :]
        d_lo = jnp.exp(jnp.minimum(bt - b_lo, 0.0))
        d_hi = jnp.where(sidx <= t, jnp.exp(jnp.minimum(bt - b_hi, 0.0)), 0.0)
        av = head_sum(d_lo * k_lo * qt) * v_lo + head_sum(d_hi * k_hi * qt) * v_hi
        o_hi = jnp.where(sidx == t, slab_sum(av), o_hi)
    o_intra = jnp.stack([o_lo, o_hi], axis=1).reshape(tc, W_C)

    blk = lambda a, j: a[j * REC_BLOCK:(j + 1) * REC_BLOCK, :]
    upd = [_dot_tn(blk(v_bf, j), blk(ke, j)) * bd_ref[...] for j in range(nblk)]
    s = s_scr[...]
    s_before = []
    for j in range(nblk):
        s_before.append(s.astype(BF16))
        s = dl[j] * s + upd[j]
    s_scr[...] = s
    o_inter = jnp.concatenate([_dot_nt(blk(qe, j), s_before[j]) for j in range(nblk)], axis=0)

    o = o_intra + o_inter
    ms = _dot((o * o).astype(BF16), segm_ref[...])
    oc_ref[0] = (o * lax.rsqrt(ms + EPS) * ghg_ref[...] * (gc * _sigmoid(gc))).astype(BF16)

    @pl.when(pl.program_id(1) == pl.num_programs(1) - 1)
    def _():
        sout_ref[0] = s_scr[...]


def _hgrn(hraw3, lbc, ghg, segm, segs, bd, s0, tc, shared_past):
    nb, t, _ = hraw3.shape
    sidx = (lambda bi, ti: (0, 0, 0)) if shared_past else (lambda bi, ti: (bi, 0, 0))
    hrows = tc // 2
    rsum = _seg_matrix(hrows, REC_BLOCK // 2, 1.0, BF16)
    return pl.pallas_call(
        functools.partial(_hgrn_kernel, tc=tc),
        grid=(nb, t // tc),
        in_specs=[pl.BlockSpec((1, tc, 4 * W_C), lambda bi, ti: (bi, ti, 0)),
                  _const_spec((SUBLANES, W_C)), _const_spec((1, W_C)),
                  _const_spec((W_C, W_C)), _const_spec((W_C, W_C)), _const_spec((W_C, W_C)),
                  _const_spec((hrows, hrows)),
                  pl.BlockSpec((1, W_C, W_C), sidx)],
        out_specs=[pl.BlockSpec((1, tc, W_C), lambda bi, ti: (bi, ti, 0)),
                   pl.BlockSpec((1, W_C, W_C), lambda bi, ti: (bi, 0, 0))],
        out_shape=[jax.ShapeDtypeStruct((nb, t, W_C), BF16),
                   jax.ShapeDtypeStruct((nb, W_C, W_C), F32)],
        scratch_shapes=[pltpu.VMEM((W_C, W_C), F32)],
        compiler_params=pltpu.CompilerParams(dimension_semantics=("arbitrary", "arbitrary")),
        name="hgrn2",
    )(hraw3, lbc, ghg, segm, segs, bd, rsum, s0)


def _split_halves(q):
    lane = lax.broadcasted_iota(I32, q.shape, 1)
    zero = jnp.zeros_like(q)
    return jnp.concatenate([jnp.where(lane < DQK, q, zero), jnp.where(lane >= DQK, q, zero)], axis=0)


def _twice(b):
    return jnp.concatenate([b, b], axis=0)


def _diff_combine(acc, l, t, lam_init, lq1, lk1, lq2, lk2, gd):
    o = acc / l
    lam = (jnp.exp(jnp.sum(lq1 * lk1, axis=1, keepdims=True))
           - jnp.exp(jnp.sum(lq2 * lk2, axis=1, keepdims=True)) + lam_init)
    a = o[:t] - lam * o[t:]
    return (_rms(a, gd) * (1.0 - lam_init)).astype(BF16)


def _attn_main_kernel(sc_ref, farc_ref, q_ref, k_ref, vt_ref, km_ref, vmt_ref, bt_ref, btm_ref,
                      lq1_ref, lk1_ref, lq2_ref, lk2_ref, gdt_ref, o_ref,
                      acc_scr, s_scr, *, tq, tk):
    h = pl.program_id(1)
    i = pl.program_id(2)
    q = q_ref[0]
    lane = lax.broadcasted_iota(I32, q.shape, 1)
    zero = jnp.zeros_like(q)
    q_half = (jnp.where(lane < DQK, q, zero), jnp.where(lane >= DQK, q, zero))
    n_tiles = (i + 1) * (tq // tk)

    def scores(j, slot):
        k_tile = k_ref[0, pl.ds(pl.multiple_of(j * tk, tk), tk), :]
        for c in range(2):
            s_scr[slot, c] = _dot_nt(k_tile, q_half[c])

    def softmax_pv(j, slot, stats, extra=None):
        bidx = jnp.minimum(n_tiles - 1 - j, bt_ref.shape[0] - 1)
        v_tile = vt_ref[:, pl.ds(pl.multiple_of(j * tk, tk), tk)]
        new_stats = []
        for c in range(2):
            m_prev, l_prev = stats[c]
            strips = []
            for w in range(tq // LANES):
                cols = slice(w * LANES, (w + 1) * LANES)
                s = s_scr[slot, c, :, cols] + bt_ref[bidx, 0, :, cols]
                m_new = jnp.maximum(m_prev[:, cols], jnp.max(s, axis=0, keepdims=True))
                if extra is not None:
                    s_x = extra[c][:, cols]
                    m_new = jnp.maximum(m_new, jnp.max(s_x, axis=0, keepdims=True))
                alpha = jnp.exp2(m_prev[:, cols] - m_new)
                p = jnp.exp2(s - m_new)
                l_new = alpha * l_prev[:, cols] + jnp.sum(p, axis=0, keepdims=True)
                p_x = None
                if extra is not None:
                    p_x = jnp.exp2(s_x - m_new)
                    l_new = l_new + jnp.sum(p_x, axis=0, keepdims=True)
                    p_x = p_x.astype(BF16)
                strips.append((p.astype(BF16), m_new, l_new, alpha, p_x))
            cat = lambda k: jnp.concatenate([st[k] for st in strips], axis=1)
            pv = _dot(v_tile, cat(0))
            if extra is not None:
                pv = pv + _dot(vmt_ref[...], cat(4))
            acc_scr[c] = cat(3) * acc_scr[c] + pv
            new_stats.append((cat(1), cat(2)))
        return tuple(new_stats)

    def pair(n, stats):
        scores(2 * n + 1, 1)
        stats = softmax_pv(2 * n, 0, stats)
        scores(2 * n + 2, 0)
        return softmax_pv(2 * n + 1, 1, stats)

    assert (tq // tk) % 2 == 0
    acc_scr[...] = jnp.zeros(acc_scr.shape, F32)
    stat0 = (jnp.full((1, tq), -jnp.inf, F32), jnp.zeros((1, tq), F32))
    scores(0, 0)
    stats = lax.fori_loop(0, n_tiles // 2 - 1, pair, (stat0, stat0))
    scores(n_tiles - 1, 1)
    stats = softmax_pv(n_tiles - 2, 0, stats)
    bias_meta = jnp.where(i == 0, btm_ref[0], farc_ref[h])
    s_meta = [_dot_nt(km_ref[0], q_half[c]) + bias_meta for c in range(2)]
    stats = softmax_pv(n_tiles - 1, 1, stats, s_meta)
    o = [acc_scr[c] * (1.0 / stats[c][1]) for c in range(2)]

    lam_init = sc_ref[0]
    lam = (jnp.exp(jnp.sum(lq1_ref[...] * lk1_ref[...], axis=1, keepdims=True))
           - jnp.exp(jnp.sum(lq2_ref[...] * lk2_ref[...], axis=1, keepdims=True)) + lam_init)
    a = o[0] - lam * o[1]
    gain = jnp.concatenate([gdt_ref[...]] * (tq // LANES), axis=1)
    y = a * lax.rsqrt(jnp.mean(a * a, axis=0, keepdims=True) + EPS) * gain * (1.0 - lam_init)
    o_ref[0] = y.T.astype(BF16)


def _attn_bias_tiles(rel_bias, tq, tk):
    n_var = -(-(tq + MAX_DIST - 1) // tk) + 1
    base = N_META + n_var * tk
    return jnp.stack([_bias_tile(rel_bias, tk, tq, base, base + tq - (d + 1) * tk, True, keys_on_rows=True)
                      for d in range(n_var)])


def _attn_main(q3, k3, vt, km, vmt, bt, btm, farc, lam_init, lq1, lk1, lq2, lk2, gdt, tq, tk):
    nb, t, _ = q3.shape
    p = km.shape[1]
    assert tq % tk == 0 and tk % CHUNK == 0 and bt.shape[2:] == (tk, tq)
    smem = pl.BlockSpec(memory_space=pltpu.SMEM)
    vec = lambda: _const_spec((1, DQK))
    return pl.pallas_call(
        functools.partial(_attn_main_kernel, tq=tq, tk=tk),
        grid=(nb, H_A, t // tq),
        in_specs=[smem, smem,
                  pl.BlockSpec((1, tq, DVA), lambda b, h, i: (b, i, h)),
                  pl.BlockSpec((1, t, DVA), lambda b, h, i: (b, 0, h)),
                  pl.BlockSpec((DVA, t), lambda b, h, i: (h, b)),
                  pl.BlockSpec((1, p, DVA), lambda b, h, i: (0, 0, h)),
                  pl.BlockSpec((DVA, p), lambda b, h, i: (h, 0)),
                  pl.BlockSpec((bt.shape[0], 1, tk, tq), lambda b, h, i: (0, h, 0, 0)),
                  pl.BlockSpec((1, p, tq), lambda b, h, i: (h, 0, 0)),
                  vec(), vec(), vec(), vec(),
                  pl.BlockSpec((DVA, LANES), lambda b, h, i: (h, 0))],
        out_specs=pl.BlockSpec((1, tq, DVA), lambda b, h, i: (b, i, h)),
        out_shape=jax.ShapeDtypeStruct((nb, t, W_A), BF16),
        scratch_shapes=[pltpu.VMEM((2, DVA, tq), F32), pltpu.VMEM((2, 2, tk, tq), F32)],
        compiler_params=pltpu.CompilerParams(dimension_semantics=("arbitrary",) * 3),
        name="attn_main",
    )(lam_init, farc, q3, k3, vt, km, vmt, bt, btm, lq1, lk1, lq2, lk2, gdt)


def _attn_small_kernel(sc_ref, q_ref, kc_ref, vc_ref, kn_ref, vn_ref, bp_ref, bn_ref,
                       lq1_ref, lk1_ref, lq2_ref, lk2_ref, gd_ref, o_ref, *, t, n_cached):
    b = pl.program_id(0)
    p = vc_ref.shape[2] // H_A
    past_bias = jnp.where(b < n_cached, 0.0, NEG)
    q_all = q_ref[0]
    kn_all = kn_ref[0]
    vn_all = vn_ref[0]
    outs = []
    for hd in range(H_A):
        cols = slice(hd * DVA, (hd + 1) * DVA)
        qq = _split_halves(q_all[:, cols])
        s_p = _dot(qq, kc_ref[0, 0, cols, :].astype(BF16)) + _twice(bp_ref[hd]) + past_bias
        s_n = _dot_nt(qq, kn_all[:, cols]) + _twice(bn_ref[hd])
        m = jnp.maximum(jnp.max(s_p, axis=1, keepdims=True), jnp.max(s_n, axis=1, keepdims=True))
        p_p = jnp.exp2(s_p - m)
        p_n = jnp.exp2(s_n - m)
        l = jnp.sum(p_p, axis=1, keepdims=True) + jnp.sum(p_n, axis=1, keepdims=True)
        v_past = vc_ref[0, 0, pl.ds(hd, p, stride=H_A), :].astype(BF16)
        acc = _dot(p_p.astype(BF16), v_past) + _dot(p_n.astype(BF16), vn_all[:, cols])
        outs.append(_diff_combine(acc, l, t, sc_ref[0], lq1_ref[...], lk1_ref[...],
                                  lq2_ref[...], lk2_ref[...], gd_ref[:, cols]))
    o_ref[0] = jnp.concatenate(outs, axis=1)


def _attn_small(q3, cache_kt, cache_v4, layer, kn, vn, bp, bn, lam_init, lq1, lk1, lq2, lk2, gdiff):
    nb, t, _ = q3.shape
    n_cached, p = cache_kt.shape[1], cache_kt.shape[3]
    smem = pl.BlockSpec(memory_space=pltpu.SMEM)
    vec = lambda: _const_spec((1, DQK))
    cached = lambda b: (layer, jnp.minimum(b, n_cached - 1), 0, 0)
    new = pl.BlockSpec((1, t, W_A), lambda b: (b, 0, 0))
    return pl.pallas_call(
        functools.partial(_attn_small_kernel, t=t, n_cached=n_cached),
        grid=(nb,),
        in_specs=[smem, new,
                  pl.BlockSpec((1, 1, W_A, p), cached), pl.BlockSpec((1, 1, p * H_A, DVA), cached),
                  new, new, _const_spec((H_A, t, p)), _const_spec((H_A, t, t)),
                  vec(), vec(), vec(), vec(), _const_spec((1, W_A))],
        out_specs=new,
        out_shape=jax.ShapeDtypeStruct((nb, t, W_A), BF16),
        compiler_params=pltpu.CompilerParams(dimension_semantics=("arbitrary",)),
        name="attn_small",
    )(lam_init, q3, cache_kt, cache_v4, kn, vn, bp, bn, lq1, lk1, lq2, lk2, gdiff)


_FF_CHUNK = 256


def _ffn_kernel(x_ref, oa_ref, ob_ref, oc_ref, wout_ref, gffn_ref, wup_ref, cw_ref, cb_ref, wdn_ref,
                past_ref, gfin_ref, y_ref, new_ref, halo_scr, *, bb, tt, final_norm):
    rows = bb * tt
    taps = FFN_CONV_W - 1

    @pl.when(pl.program_id(1) == 0)
    def _():
        halo_scr[...] = past_ref[...]

    x1 = (x_ref[...].reshape(rows, D_MODEL)
          + _dot(oa_ref[...].reshape(rows, W_A), wout_ref[0:W_A, :])
          + _dot(ob_ref[...].reshape(rows, C_CONV), wout_ref[W_A:W_A + C_CONV, :])
          + _dot(oc_ref[...].reshape(rows, W_C), wout_ref[W_A + C_CONV:D_MODEL, :]))
    h2 = _rms(x1, gffn_ref[...]).astype(BF16)

    tpos = lax.broadcasted_iota(I32, (bb, tt, _FF_CHUNK), 1)
    acc = jnp.zeros((rows, D_MODEL), F32)
    for c in range(D_FF // _FF_CHUNK):
        lo, hi = c * _FF_CHUNK, (c + 1) * _FF_CHUNK
        gate = _dot(h2, wup_ref[:, lo:hi])
        val = _dot(h2, wup_ref[:, D_FF + lo:D_FF + hi])
        gate3 = gate.reshape(bb, tt, _FF_CHUNK)
        conv = cw_ref[taps:taps + 1, lo:hi] * gate + cb_ref[:, lo:hi]
        for d in range(1, taps + 1):
            shifted = pltpu.roll(gate, d, axis=0).reshape(bb, tt, _FF_CHUNK)
            for e in range(d):
                shifted = jnp.where(tpos == e, halo_scr[:, taps - d + e:taps - d + e + 1, lo:hi], shifted)
            conv = conv + cw_ref[taps - d:taps - d + 1, lo:hi] * shifted.reshape(rows, _FF_CHUNK)
        halo_scr[:, :, lo:hi] = gate3[:, tt - taps:tt, :]
        gelu = 0.5 * conv * (1.0 + jnp.tanh(math.sqrt(2.0 / math.pi) * (conv + 0.044715 * (conv * conv * conv))))
        acc = acc + _dot((gelu * val).astype(BF16), wdn_ref[lo:hi, :])

    y = x1 + acc
    if final_norm:
        y = _rms(y, gfin_ref[...])
    y_ref[...] = y.reshape(bb, tt, D_MODEL)
    new_ref[...] = halo_scr[...]


def _ffn(x3, oa3, ob3, oc3, wout, gffn, wup, cw, cb, wdn, past, gfin, bb, tt, shared_past, final_norm):
    nb, t, _ = x3.shape
    taps = FFN_CONV_W - 1
    assert tt >= taps
    tok = lambda w: pl.BlockSpec((bb, tt, w), lambda bi, ti: (bi, ti, 0))
    pidx = (lambda bi, ti: (0, 0, 0)) if shared_past else (lambda bi, ti: (bi, 0, 0))
    assert not (shared_past and bb != 1)
    return pl.pallas_call(
        functools.partial(_ffn_kernel, bb=bb, tt=tt, final_norm=final_norm),
        grid=(nb // bb, t // tt),
        in_specs=[tok(D_MODEL), tok(W_A), tok(C_CONV), tok(W_C),
                  _const_spec((D_MODEL, D_MODEL)), _const_spec((1, D_MODEL)),
                  _const_spec((D_MODEL, 2 * D_FF)), _const_spec((FFN_CONV_W, D_FF)), _const_spec((1, D_FF)),
                  _const_spec((D_FF, D_MODEL)),
                  pl.BlockSpec((bb, taps, D_FF), pidx), _const_spec((1, D_MODEL))],
        out_specs=[tok(D_MODEL), pl.BlockSpec((bb, taps, D_FF), lambda bi, ti: (bi, 0, 0))],
        out_shape=[jax.ShapeDtypeStruct((nb, t, D_MODEL), F32),
                   jax.ShapeDtypeStruct((nb, taps, D_FF), F32)],
        scratch_shapes=[pltpu.VMEM((bb, taps, D_FF), F32)],
        compiler_params=pltpu.CompilerParams(dimension_semantics=("arbitrary", "arbitrary"),
                                             vmem_limit_bytes=VMEM_LIMIT),
        name="outproj_ffn",
    )(x3, oa3, ob3, oc3, wout, gffn, wup, cw, cb, wdn, past, gfin)


def _seg_matrix(width, group, value, dtype):
    g = jnp.arange(width, dtype=I32) // group
    return jnp.where(g[:, None] == g[None, :], value, 0.0).astype(dtype)


def _state_to_kernel(s):
    n = s.shape[0]
    eye = jnp.eye(H_C, dtype=s.dtype)
    full = jnp.einsum('nhdv,hg->nhvgd', s, eye)
    return full.reshape(n, W_C, H_C * DK_C)


def _state_from_kernel(sf):
    n = sf.shape[0]
    s5 = sf.reshape(n, H_C, DV_C, H_C, DK_C)
    diag = jnp.stack([s5[:, h, :, h, :] for h in range(H_C)], axis=1)
    return jnp.swapaxes(diag, 2, 3)


def _pick_tile(total, want):
    t = min(total, want)
    assert total % t == 0, (total, want)
    return t


def kernel(x_prompt, x_sample, cache_k, cache_v, state_conv, state_hgrn, state_ffn, meta_tokens, rel_bias, g_mix, w_in, g_q, g_k, lam_q1, lam_k1, lam_q2, lam_k2, g_diff, conv_w, conv_b, ln_g, ln_b, lb_logits, g_hgrn, w_out, g_ffn, w_up, ffn_conv_w, ffn_conv_b, w_down, g_final):
    depth = g_mix.shape[0]
    bp, seq, _ = x_prompt.shape
    bs, ts, _ = x_sample.shape
    past_len = cache_k.shape[2]
    assert ts == N_META == REC_BLOCK and meta_tokens.shape[0] == N_META
    assert seq % CHUNK == 0 and past_len % CHUNK == 0

    lb_all = jnp.cumsum(jax.nn.softmax(lb_logits.astype(F32), axis=0), axis=0)
    lb_all = lb_all - lb_all[0:1]
    lb_consts = jnp.stack([jnp.log(lb_all), jnp.log1p(-lb_all), 1.0 - lb_all]
                          + [jnp.zeros_like(lb_all)] * (SUBLANES - 3), axis=1)

    seg_qk = _seg_matrix(W_A, DQK, 1.0 / DQK, BF16)
    seg_mean = _seg_matrix(W_C, DV_C, 1.0 / DV_C, BF16)
    seg_sum = _seg_matrix(W_C, DK_C, 1.0, BF16)
    blockdiag = _seg_matrix(W_C, DK_C, 1.0, F32)

    tq = _pick_tile(seq, 512)
    tk = _pick_tile(tq, 256)
    bt_main = _attn_bias_tiles(rel_bias, tq, tk)
    bt_meta = _bias_tile(rel_bias, N_META, tq, N_META, 0, True, keys_on_rows=True)
    far_bias = rel_bias[_NB - 1, :] * LOG2E
    bias_past = _bias_tile(rel_bias, ts, past_len, past_len, 0, False)
    bias_new = _bias_tile(rel_bias, ts, ts, past_len, past_len, False)

    cache_kt = jnp.swapaxes(cache_k.reshape(depth, bs, past_len, W_A), 2, 3)
    cache_v4 = cache_v.reshape(depth, bs, past_len * H_A, DVA)

    xs = jnp.concatenate([x_sample, meta_tokens.astype(x_sample.dtype)[None]], axis=0)
    xm = x_prompt
    nbs = bs + 1
    zrow = lambda a: jnp.zeros((1,) + a.shape[1:], a.dtype)

    tm_main = _pick_tile(bp * seq, 512)
    outs = {k: [] for k in ("kp", "vp", "cp", "hp", "fp", "ks", "vs", "cs", "hs", "fs")}
    for l in range(depth):
        last = l == depth - 1
        lam_init = jnp.full((1,), 0.8 - 0.6 * math.exp(-0.3 * l), F32)
        win = w_in[l].astype(BF16)
        wout = w_out[l].astype(BF16)
        wup = w_up[l].astype(BF16)
        wdn = w_down[l].astype(BF16)
        gq = jnp.tile(g_q[l], 2 * H_A)[None]
        gk = jnp.tile(g_k[l], 2 * H_A)[None]
        row = lambda a: a[l][None]
        common_attn = (lam_init, row(lam_q1), row(lam_k1), row(lam_q2), row(lam_k2), row(g_diff))

        def mixer_front(x3, tm, feature_major):
            nb, t, _ = x3.shape
            q, kb, vf4, glu, hraw, k_extra, v_extra = _inproj(
                x3.reshape(nb * t, D_MODEL), row(g_mix), win, gq, gk, seg_qk, tm, t, feature_major)
            r3 = lambda a: a.reshape(nb, t, a.shape[-1])
            return (r3(q), r3(kb), vf4.reshape(nb, t, H_A, DVA), r3(glu), r3(hraw),
                    k_extra if feature_major else r3(k_extra), v_extra if feature_major else r3(v_extra))

        q, kb, vf_s, glu, hraw, kf_s, vb = mixer_front(xs, nbs * ts, False)
        oa = _attn_small(q, cache_kt, cache_v4, l, kb, vb, bias_past, bias_new, *common_attn)
        ob, conv_s = _conv_module(glu, jnp.concatenate([state_conv[l], zrow(state_conv[l])]),
                                  conv_w[l], row(conv_b), row(ln_g), row(ln_b), ts, False)
        oc, hg_s = _hgrn(hraw, lb_consts[l], row(g_hgrn), seg_mean, seg_sum, blockdiag,
                         _state_to_kernel(jnp.concatenate([state_hgrn[l], zrow(state_hgrn[l])]).astype(F32)),
                         ts, False)
        xs, ffn_s = _ffn(xs, oa, ob, oc, wout, row(g_ffn), wup, ffn_conv_w[l], row(ffn_conv_b), wdn,
                         jnp.concatenate([state_ffn[l], zrow(state_ffn[l])]), g_final[None],
                         nbs, ts, False, last)
        meta_k, meta_vt = kb[bs:], vb[bs].T
        gdt = jnp.broadcast_to(g_diff[l][:, None], (W_A, LANES))

        q, kb, vf_m, glu, hraw, kft_m, vt = mixer_front(xm, tm_main, True)
        oa = _attn_main(q, kb, vt, meta_k, meta_vt, bt_main, bt_meta, far_bias, *common_attn[:-1], gdt, tq, tk)
        ob, conv_m = _conv_module(glu, conv_s[bs:], conv_w[l], row(conv_b), row(ln_g), row(ln_b),
                                  _pick_tile(seq, 512), True)
        oc, hg_m = _hgrn(hraw, lb_consts[l], row(g_hgrn), seg_mean, seg_sum, blockdiag, hg_s[bs:],
                         _pick_tile(seq, 256), True)
        xm, ffn_m = _ffn(xm, oa, ob, oc, wout, row(g_ffn), wup, ffn_conv_w[l], row(ffn_conv_b), wdn,
                         ffn_s[bs:], g_final[None], 1, _pick_tile(seq, 512), True, last)

        meta_rows = lambda a: jnp.broadcast_to(a[bs:], (bp,) + a.shape[1:])
        kt_p = jnp.concatenate([meta_rows(jnp.swapaxes(kf_s, 1, 2)), kft_m], axis=2)
        outs["kp"].append(jnp.swapaxes(kt_p, 1, 2).reshape(bp, N_META + seq, H_A, 2, DQK))
        outs["vp"].append(jnp.concatenate([meta_rows(vf_s), vf_m], axis=1))
        outs["cp"].append(conv_m)
        outs["hp"].append(_state_from_kernel(hg_m))
        outs["fp"].append(ffn_m)
        outs["ks"].append(kf_s[:bs].reshape(bs, ts, H_A, 2, DQK))
        outs["vs"].append(vf_s[:bs])
        outs["cs"].append(conv_s[:bs])
        outs["hs"].append(_state_from_kernel(hg_s[:bs]))
        outs["fs"].append(ffn_s[:bs])

    st = {k: jnp.stack(v) for k, v in outs.items()}
    return (xm, xs[:bs], st["kp"], st["vp"], st["cp"], st["hp"], st["fp"],
            st["ks"], st["vs"], st["cs"], st["hs"], st["fs"])
```

```python
import functools
import math

import jax
import jax.numpy as jnp
from jax import lax
from jax.experimental import pallas as pl
from jax.experimental.pallas import tpu as pltpu

F32 = jnp.float32
BF16 = jnp.bfloat16
I32 = jnp.int32

D_MODEL = 1024
N_META = 16
CHUNK = 64
DQK = 64
DVA = 2 * DQK
H_A = 4
W_A = H_A * DVA
C_CONV = 256
CONV_W = 31
H_C = 4
DK_C = 64
DV_C = 64
W_C = H_C * DV_C
D_FF = 2816
FFN_CONV_W = 3
N_BUCKETS = 32
MAX_DIST = 128
REC_BLOCK = 16
EPS = 1e-6
NEG = -1e30
LOG2E = math.log2(math.e)
IN_COLS = 2 * W_A + W_A + 2 * C_CONV + 4 * W_C

LANES = 128
SUBLANES = 8
VMEM_LIMIT = 56 * 1024 * 1024

_NB = N_BUCKETS // 2
_MAX_EXACT = _NB // 2
_BUCKET_THR = tuple(
    next(n for n in range(_MAX_EXACT, MAX_DIST + 1)
         if n ** (_NB - _MAX_EXACT) * _MAX_EXACT ** k >= _MAX_EXACT ** (_NB - _MAX_EXACT) * MAX_DIST ** k)
    for k in range(1, _NB - _MAX_EXACT))


def _rms(x, g):
    return x * lax.rsqrt(jnp.mean(x * x, axis=-1, keepdims=True) + EPS) * g


def _dot(a, b):
    return jnp.dot(a, b, preferred_element_type=F32)


def _dot_nt(a, b):
    return lax.dot_general(a, b, (((1,), (1,)), ((), ())), preferred_element_type=F32)


def _dot_tn(a, b):
    return lax.dot_general(a, b, (((0,), (0,)), ((), ())), preferred_element_type=F32)


def _sigmoid(x):
    return jax.nn.sigmoid(x)


def _const_spec(shape):
    nd = len(shape)
    return pl.BlockSpec(shape, lambda *_: (0,) * nd, pipeline_mode=pl.Buffered(1))


def _bias_kernel(tab_ref, o_ref, *, qpos0, kpos0, prompt_chunks, keys_on_rows):
    h = pl.program_id(0)
    shape = o_ref.shape[1:]
    r = lax.broadcasted_iota(I32, shape, 1 if keys_on_rows else 0) + qpos0
    c = lax.broadcasted_iota(I32, shape, 0 if keys_on_rows else 1) + kpos0
    rel = c - r
    n = jnp.abs(rel)
    large = jnp.full(shape, _MAX_EXACT, I32)
    for thr in _BUCKET_THR:
        large = large + (n >= thr).astype(I32)
    bucket = jnp.where(rel > 0, _NB, 0) + jnp.where(n < _MAX_EXACT, n, large)
    out = jnp.zeros(shape, F32)
    for b in range(N_BUCKETS):
        out = jnp.where(bucket == b, tab_ref[b, h] * LOG2E, out)
    shift = CHUNK - N_META if prompt_chunks else 0
    log2_chunk = CHUNK.bit_length() - 1
    visible = ((c + shift) >> log2_chunk) <= ((r + shift) >> log2_chunk)
    o_ref[0] = jnp.where(visible, out, NEG)


def _bias_tile(rel_bias, rows, cols, qpos0, kpos0, prompt_chunks, keys_on_rows=False):
    return pl.pallas_call(
        functools.partial(_bias_kernel, qpos0=qpos0, kpos0=kpos0, prompt_chunks=prompt_chunks,
                          keys_on_rows=keys_on_rows),
        grid=(H_A,),
        in_specs=[pl.BlockSpec(memory_space=pltpu.SMEM)],
        out_specs=pl.BlockSpec((1, rows, cols), lambda h: (h, 0, 0)),
        out_shape=jax.ShapeDtypeStruct((H_A, rows, cols), F32),
        name="relpos_bias",
    )(rel_bias)


def _inproj_kernel(x_ref, gmix_ref, win_ref, gq_ref, gk_ref, seg_ref,
                   q_ref, kb_ref, vf4_ref, glu_ref, hraw_ref, k_extra_ref, v_extra_ref, *, feature_major):
    tm = x_ref.shape[0]
    h = _rms(x_ref[...], gmix_ref[...]).astype(BF16)

    def proj(lo, hi):
        return _dot(h, win_ref[:, lo:hi])

    def group_norm(a, g):
        ms = _dot((a * a).astype(BF16), seg_ref[...])
        return a * lax.rsqrt(ms + EPS) * g

    qa = group_norm(proj(0, W_A), gq_ref[...])
    q_ref[...] = (qa * (DQK ** -0.5 * LOG2E)).astype(BF16)
    ka = group_norm(proj(W_A, 2 * W_A), gk_ref[...])
    kb_ref[...] = ka.astype(BF16)
    va = proj(2 * W_A, 3 * W_A)
    for hd in range(H_A):
        vf4_ref[pl.ds(hd, tm, stride=H_A), :] = va[:, hd * DVA:(hd + 1) * DVA]
    if feature_major:
        k_extra_ref[0] = ka.T
        v_extra_ref[...] = va.T.astype(BF16)
    else:
        k_extra_ref[...] = ka
        v_extra_ref[...] = va.astype(BF16)
    glu_ref[...] = proj(3 * W_A, 3 * W_A + 2 * C_CONV)
    hraw_ref[...] = proj(3 * W_A + 2 * C_CONV, IN_COLS)


def _inproj(x2d, gmix, win, gq, gk, seg, tm, rows_per_seq, feature_major):
    n = x2d.shape[0]
    row = lambda w: pl.BlockSpec((tm, w), lambda i: (i, 0))
    outs = [(W_A, BF16), (W_A, BF16), None, (2 * C_CONV, F32), (4 * W_C, F32)]
    out_specs = [row(o[0]) if o else pl.BlockSpec((tm * H_A, DVA), lambda i: (i, 0)) for o in outs]
    out_shape = [jax.ShapeDtypeStruct((n, o[0]), o[1]) if o else jax.ShapeDtypeStruct((n * H_A, DVA), F32)
                 for o in outs]
    if feature_major:
        nt = rows_per_seq // tm
        out_specs += [pl.BlockSpec((1, W_A, tm), lambda i: (i // nt, 0, i % nt)),
                      pl.BlockSpec((W_A, tm), lambda i: (0, i))]
        out_shape += [jax.ShapeDtypeStruct((n // rows_per_seq, W_A, rows_per_seq), F32),
                      jax.ShapeDtypeStruct((W_A, n), BF16)]
    else:
        out_specs += [row(W_A), row(W_A)]
        out_shape += [jax.ShapeDtypeStruct((n, W_A), F32), jax.ShapeDtypeStruct((n, W_A), BF16)]
    return pl.pallas_call(
        functools.partial(_inproj_kernel, feature_major=feature_major),
        grid=(n // tm,),
        in_specs=[row(D_MODEL), _const_spec((1, D_MODEL)), _const_spec((D_MODEL, IN_COLS)),
                  _const_spec((1, W_A)), _const_spec((1, W_A)), _const_spec((W_A, W_A))],
        out_specs=out_specs,
        out_shape=out_shape,
        compiler_params=pltpu.CompilerParams(dimension_semantics=("arbitrary",), vmem_limit_bytes=VMEM_LIMIT),
        name="inproj",
    )(x2d, gmix, win, gq, gk, seg)


_CONV_PAD = 32


def _conv_kernel(glu_ref, past_ref, w_ref, b_ref, lng_ref, lnb_ref, ob_ref, new_ref, xc_ref, *, tt, rc):
    halo = CONV_W - 1
    h0 = _CONV_PAD - halo

    @pl.when(pl.program_id(1) == 0)
    def _():
        xc_ref[h0:_CONV_PAD, :] = past_ref[0]

    glu = glu_ref[0]
    xc_ref[_CONV_PAD:_CONV_PAD + tt, :] = glu[:, :C_CONV] * _sigmoid(glu[:, C_CONV:])
    bias = b_ref[...]
    for c in range(tt // rc):
        acc = jnp.zeros((rc, C_CONV), F32) + bias
        for j in range(CONV_W):
            acc = acc + w_ref[j:j + 1, :] * xc_ref[c * rc + j + h0:c * rc + j + h0 + rc, :]
        mu = jnp.mean(acc, axis=-1, keepdims=True)
        xm = acc - mu
        var = jnp.mean(xm * xm, axis=-1, keepdims=True)
        y = xm * lax.rsqrt(var + EPS) * lng_ref[...] + lnb_ref[...]
        ob_ref[0, c * rc:(c + 1) * rc, :] = (y * _sigmoid(y)).astype(BF16)
    new = xc_ref[h0 + tt:_CONV_PAD + tt, :]
    new_ref[0] = new
    xc_ref[h0:_CONV_PAD, :] = new


def _conv_module(glu3, past, w, b, lng, lnb, tt, shared_past):
    nb, t, _ = glu3.shape
    rc = min(tt, 128)
    pidx = (lambda bi, ti: (0, 0, 0)) if shared_past else (lambda bi, ti: (bi, 0, 0))
    return pl.pallas_call(
        functools.partial(_conv_kernel, tt=tt, rc=rc),
        grid=(nb, t // tt),
        in_specs=[pl.BlockSpec((1, tt, 2 * C_CONV), lambda bi, ti: (bi, ti, 0)),
                  pl.BlockSpec((1, CONV_W - 1, C_CONV), pidx),
                  _const_spec((CONV_W, C_CONV)), _const_spec((1, C_CONV)),
                  _const_spec((1, C_CONV)), _const_spec((1, C_CONV))],
        out_specs=[pl.BlockSpec((1, tt, C_CONV), lambda bi, ti: (bi, ti, 0)),
                   pl.BlockSpec((1, CONV_W - 1, C_CONV), lambda bi, ti: (bi, 0, 0))],
        out_shape=[jax.ShapeDtypeStruct((nb, t, C_CONV), BF16),
                   jax.ShapeDtypeStruct((nb, CONV_W - 1, C_CONV), F32)],
        scratch_shapes=[pltpu.VMEM((_CONV_PAD + tt, C_CONV), F32)],
        compiler_params=pltpu.CompilerParams(dimension_semantics=("arbitrary", "arbitrary")),
        name="conv_module",
    )(glu3, past, w, b, lng, lnb)


def _hgrn_kernel(hraw_ref, lbc_ref, ghg_ref, segm_ref, segs_ref, bd_ref, rsum_ref, s0_ref,
                 oc_ref, sout_ref, s_scr, *, tc, chained):
    nblk = tc // REC_BLOCK
    blk3 = (nblk, REC_BLOCK, W_C)

    if chained:
        @pl.when(pl.program_id(1) == 0)
        def _():
            s_scr[...] = s0_ref[0]

    hr = hraw_ref[...].reshape(tc, 4 * W_C)
    z = hr[:, 0:W_C]
    v = hr[:, W_C:2 * W_C]
    q = hr[:, 2 * W_C:3 * W_C]
    gc = hr[:, 3 * W_C:4 * W_C]
    log_lb = lbc_ref[0:1, :]
    log_1m_lb = lbc_ref[1:2, :]
    one_m_lb = lbc_ref[2:3, :]

    log_sig = jnp.minimum(z, 0.0) - jnp.log(1.0 + jnp.exp(-jnp.abs(z)))
    b_arg = log_1m_lb + log_sig
    log_f = jnp.maximum(log_lb, b_arg) + jnp.log(1.0 + jnp.exp(-jnp.abs(log_lb - b_arg)))
    kc = one_m_lb * _sigmoid(-z)

    pos = lax.broadcasted_iota(I32, (tc, W_C), 0) & (REC_BLOCK - 1)
    bcum = log_f
    sh = 1
    while sh < REC_BLOCK:
        bcum = bcum + jnp.where(pos >= sh, pltpu.roll(bcum, sh, axis=0), 0.0)
        sh *= 2

    b3 = bcum.reshape(blk3)
    b_last = b3[:, REC_BLOCK - 1:REC_BLOCK, :]
    qe = (q * jnp.exp(bcum)).astype(BF16)
    ke = (kc.reshape(blk3) * jnp.exp(b_last - b3)).reshape(tc, W_C).astype(BF16)
    v_bf = v.astype(BF16)
    dl = jnp.exp(b_last)

    half = REC_BLOCK // 2
    slab = (nblk, 2, half, W_C)
    hrows = nblk * half
    b_lo, b_hi = (bcum.reshape(slab)[:, i] for i in range(2))
    k_lo, k_hi = (kc.reshape(slab)[:, i] for i in range(2))
    q_lo, q_hi = (q.reshape(slab)[:, i] for i in range(2))
    v_lo, v_hi = (v.reshape(slab)[:, i] for i in range(2))
    sidx = lax.broadcasted_iota(I32, (nblk, half, W_C), 1)

    def head_sum(x):
        return _dot(x.reshape(hrows, W_C).astype(BF16), segs_ref[...]).reshape(nblk, half, W_C)

    def slab_sum(x):
        return _dot(rsum_ref[...], x.reshape(hrows, W_C).astype(BF16)).reshape(nblk, half, W_C)

    o_lo = jnp.zeros((nblk, half, W_C), F32)
    o_hi = jnp.zeros((nblk, half, W_C), F32)
    for t in range(half):
        bt, qt = b_lo[:, t:t + 1, :], q_lo[:, t:t + 1, :]
        decay = jnp.where(sidx <= t, jnp.exp(jnp.minimum(bt - b_lo, 0.0)), 0.0)
        o_lo = jnp.where(sidx == t, slab_sum(head_sum(decay * k_lo * qt) * v_lo), o_lo)
    for t in range(half):
        bt, qt = b_hi[:, t:t + 1, :], q_hi[:, t:t + 1, :]
        d_lo = jnp.exp(jnp.minimum(bt - b_lo, 0.0))
        d_hi = jnp.where(sidx <= t, jnp.exp(jnp.minimum(bt - b_hi, 0.0)), 0.0)
        av = head_sum(d_lo * k_lo * qt) * v_lo + head_sum(d_hi * k_hi * qt) * v_hi
        o_hi = jnp.where(sidx == t, slab_sum(av), o_hi)
    o_intra = jnp.stack([o_lo, o_hi], axis=1).reshape(tc, W_C)

    blk = lambda a, j: a[j * REC_BLOCK:(j + 1) * REC_BLOCK, :]
    upd = [_dot_tn(blk(v_bf, j), blk(ke, j)) * bd_ref[...] for j in range(nblk)]
    if chained:
        s = s_scr[...]
        s_before = []
        for j in range(nblk):
            s_before.append(s.astype(BF16))
            s = dl[j] * s + upd[j]
        s_scr[...] = s
    else:
        s_before = [s0_ref[j].astype(BF16) for j in range(nblk)]
        for j in range(nblk):
            sout_ref[j] = dl[j] * s0_ref[j] + upd[j]
    o_inter = jnp.concatenate([_dot_nt(blk(qe, j), s_before[j]) for j in range(nblk)], axis=0)

    o = o_intra + o_inter
    ms = _dot((o * o).astype(BF16), segm_ref[...])
    oc = (o * lax.rsqrt(ms + EPS) * ghg_ref[...] * (gc * _sigmoid(gc))).astype(BF16)
    oc_ref[...] = oc.reshape(oc_ref.shape)

    if chained:
        @pl.when(pl.program_id(1) == pl.num_programs(1) - 1)
        def _():
            sout_ref[0] = s_scr[...]


def _hgrn(hraw3, lbc, ghg, segm, segs, bd, s0, rows, tc, shared_past):
    nb, t, _ = hraw3.shape
    chained = rows == 1
    assert chained or (t == tc == REC_BLOCK and not shared_past)
    sidx = (lambda bi, ti: (0, 0, 0)) if shared_past else (lambda bi, ti: (bi, 0, 0))
    hrows = rows * tc // 2
    rsum = _seg_matrix(hrows, REC_BLOCK // 2, 1.0, BF16)
    return pl.pallas_call(
        functools.partial(_hgrn_kernel, tc=rows * tc, chained=chained),
        grid=(nb // rows, t // tc),
        in_specs=[pl.BlockSpec((rows, tc, 4 * W_C), lambda bi, ti: (bi, ti, 0)),
                  _const_spec((SUBLANES, W_C)), _const_spec((1, W_C)),
                  _const_spec((W_C, W_C)), _const_spec((W_C, W_C)), _const_spec((W_C, W_C)),
                  _const_spec((hrows, hrows)),
                  pl.BlockSpec((rows, W_C, W_C), sidx)],
        out_specs=[pl.BlockSpec((rows, tc, W_C), lambda bi, ti: (bi, ti, 0)),
                   pl.BlockSpec((rows, W_C, W_C), lambda bi, ti: (bi, 0, 0))],
        out_shape=[jax.ShapeDtypeStruct((nb, t, W_C), BF16),
                   jax.ShapeDtypeStruct((nb, W_C, W_C), F32)],
        scratch_shapes=[pltpu.VMEM((W_C, W_C), F32)],
        compiler_params=pltpu.CompilerParams(dimension_semantics=("arbitrary", "arbitrary")),
        name="hgrn2",
    )(hraw3, lbc, ghg, segm, segs, bd, rsum, s0)


def _split_halves(q):
    lane = lax.broadcasted_iota(I32, q.shape, 1)
    zero = jnp.zeros_like(q)
    return jnp.concatenate([jnp.where(lane < DQK, q, zero), jnp.where(lane >= DQK, q, zero)], axis=0)


def _twice(b):
    return jnp.concatenate([b, b], axis=0)


def _diff_combine(acc, l, t, lam_init, lq1, lk1, lq2, lk2, gd):
    o = acc / l
    lam = (jnp.exp(jnp.sum(lq1 * lk1, axis=1, keepdims=True))
           - jnp.exp(jnp.sum(lq2 * lk2, axis=1, keepdims=True)) + lam_init)
    a = o[:t] - lam * o[t:]
    return (_rms(a, gd) * (1.0 - lam_init)).astype(BF16)


def _attn_main_kernel(sc_ref, farc_ref, q_ref, k_ref, vt_ref, km_ref, vmt_ref, bt_ref, btm_ref,
                      lq1_ref, lk1_ref, lq2_ref, lk2_ref, gdt_ref, o_ref,
                      acc_scr, s_scr, *, tq, tk):
    h = pl.program_id(1)
    i = pl.program_id(2)
    q = q_ref[0]
    lane = lax.broadcasted_iota(I32, q.shape, 1)
    zero = jnp.zeros_like(q)
    q_half = (jnp.where(lane < DQK, q, zero), jnp.where(lane >= DQK, q, zero))
    n_tiles = (i + 1) * (tq // tk)

    def scores(j, slot):
        k_tile = k_ref[0, pl.ds(pl.multiple_of(j * tk, tk), tk), :]
        for c in range(2):
            s_scr[slot, c] = _dot_nt(k_tile, q_half[c])

    n_var = bt_ref.shape[0]
    far_bias = farc_ref[h]

    def softmax_pv(j, slot, stats, extra=None, all_far=False):
        bidx = jnp.minimum(n_tiles - 1 - j, n_var - 1)
        v_tile = vt_ref[:, pl.ds(pl.multiple_of(j * tk, tk), tk)]
        new_stats = []
        for c in range(2):
            m_prev, l_prev = stats[c]
            strips = []
            for w in range(tq // LANES):
                cols = slice(w * LANES, (w + 1) * LANES)
                if all_far:
                    s = s_scr[slot, c, :, cols]
                    m_new = jnp.maximum(m_prev[:, cols], jnp.max(s, axis=0, keepdims=True) + far_bias)
                    shift = m_new - far_bias
                else:
                    s = s_scr[slot, c, :, cols] + bt_ref[bidx, 0, :, cols]
                    m_new = jnp.maximum(m_prev[:, cols], jnp.max(s, axis=0, keepdims=True))
                    shift = m_new
                if extra is not None:
                    s_x = extra[c][:, cols]
                    m_new = jnp.maximum(m_new, jnp.max(s_x, axis=0, keepdims=True))
                    shift = m_new
                alpha = jnp.exp2(m_prev[:, cols] - m_new)
                p = jnp.exp2(s - shift)
                l_new = alpha * l_prev[:, cols] + jnp.sum(p, axis=0, keepdims=True)
                p_x = None
                if extra is not None:
                    p_x = jnp.exp2(s_x - m_new)
                    l_new = l_new + jnp.sum(p_x, axis=0, keepdims=True)
                    p_x = p_x.astype(BF16)
                strips.append((p.astype(BF16), m_new, l_new, alpha, p_x))
            cat = lambda k: jnp.concatenate([st[k] for st in strips], axis=1)
            pv = _dot(v_tile, cat(0))
            if extra is not None:
                pv = pv + _dot(vmt_ref[...], cat(4))
            acc_scr[c] = cat(3) * acc_scr[c] + pv
            new_stats.append((cat(1), cat(2)))
        return tuple(new_stats)

    def pair(n, stats, all_far):
        scores(2 * n + 1, 1)
        stats = softmax_pv(2 * n, 0, stats, all_far=all_far)
        scores(2 * n + 2, 0)
        return softmax_pv(2 * n + 1, 1, stats, all_far=all_far)

    assert (tq // tk) % 2 == 0
    acc_scr[...] = jnp.zeros(acc_scr.shape, F32)
    stat0 = (jnp.full((1, tq), -jnp.inf, F32), jnp.zeros((1, tq), F32))
    scores(0, 0)
    n_pairs = n_tiles // 2 - 1
    n_far_pairs = jnp.clip((n_tiles + 1 - n_var) // 2, 0, n_pairs)
    stats = lax.fori_loop(0, n_far_pairs, functools.partial(pair, all_far=True), (stat0, stat0))
    stats = lax.fori_loop(n_far_pairs, n_pairs, functools.partial(pair, all_far=False), stats)
    scores(n_tiles - 1, 1)
    stats = softmax_pv(n_tiles - 2, 0, stats)
    bias_meta = jnp.where(i == 0, btm_ref[0], farc_ref[h])
    s_meta = [_dot_nt(km_ref[0], q_half[c]) + bias_meta for c in range(2)]
    stats = softmax_pv(n_tiles - 1, 1, stats, s_meta)
    o = [acc_scr[c] * (1.0 / stats[c][1]) for c in range(2)]

    lam_init = sc_ref[0]
    lam = (jnp.exp(jnp.sum(lq1_ref[...] * lk1_ref[...], axis=1, keepdims=True))
           - jnp.exp(jnp.sum(lq2_ref[...] * lk2_ref[...], axis=1, keepdims=True)) + lam_init)
    a = o[0] - lam * o[1]
    gain = jnp.concatenate([gdt_ref[...]] * (tq // LANES), axis=1)
    y = a * lax.rsqrt(jnp.mean(a * a, axis=0, keepdims=True) + EPS) * gain * (1.0 - lam_init)
    o_ref[0] = y.T.astype(BF16)


def _attn_bias_tiles(rel_bias, tq, tk):
    n_var = -(-(tq + MAX_DIST - 1) // tk) + 1
    base = N_META + n_var * tk
    return jnp.stack([_bias_tile(rel_bias, tk, tq, base, base + tq - (d + 1) * tk, True, keys_on_rows=True)
                      for d in range(n_var)])


def _attn_main(q3, k3, vt, km, vmt, bt, btm, farc, lam_init, lq1, lk1, lq2, lk2, gdt, tq, tk):
    nb, t, _ = q3.shape
    p = km.shape[1]
    assert tq % tk == 0 and tk % CHUNK == 0 and bt.shape[2:] == (tk, tq)
    smem = pl.BlockSpec(memory_space=pltpu.SMEM)
    vec = lambda: _const_spec((1, DQK))
    return pl.pallas_call(
        functools.partial(_attn_main_kernel, tq=tq, tk=tk),
        grid=(nb, H_A, t // tq),
        in_specs=[smem, smem,
                  pl.BlockSpec((1, tq, DVA), lambda b, h, i: (b, i, h)),
                  pl.BlockSpec((1, t, DVA), lambda b, h, i: (b, 0, h)),
                  pl.BlockSpec((DVA, t), lambda b, h, i: (h, b)),
                  pl.BlockSpec((1, p, DVA), lambda b, h, i: (0, 0, h)),
                  pl.BlockSpec((DVA, p), lambda b, h, i: (h, 0)),
                  pl.BlockSpec((bt.shape[0], 1, tk, tq), lambda b, h, i: (0, h, 0, 0)),
                  pl.BlockSpec((1, p, tq), lambda b, h, i: (h, 0, 0)),
                  vec(), vec(), vec(), vec(),
                  pl.BlockSpec((DVA, LANES), lambda b, h, i: (h, 0))],
        out_specs=pl.BlockSpec((1, tq, DVA), lambda b, h, i: (b, i, h)),
        out_shape=jax.ShapeDtypeStruct((nb, t, W_A), BF16),
        scratch_shapes=[pltpu.VMEM((2, DVA, tq), F32), pltpu.VMEM((2, 2, tk, tq), F32)],
        compiler_params=pltpu.CompilerParams(dimension_semantics=("arbitrary",) * 3),
        name="attn_main",
    )(lam_init, farc, q3, k3, vt, km, vmt, bt, btm, lq1, lk1, lq2, lk2, gdt)


def _attn_small_kernel(sc_ref, q_ref, kc_ref, vc_ref, kn_ref, vn_ref, bp_ref, bn_ref,
                       lq1_ref, lk1_ref, lq2_ref, lk2_ref, gd_ref, o_ref, *, t, n_cached):
    b = pl.program_id(0)
    p = vc_ref.shape[2] // H_A
    past_bias = jnp.where(b < n_cached, 0.0, NEG)
    q_all = q_ref[0]
    kn_all = kn_ref[0]
    vn_all = vn_ref[0]
    outs = []
    for hd in range(H_A):
        cols = slice(hd * DVA, (hd + 1) * DVA)
        qq = _split_halves(q_all[:, cols])
        s_p = _dot(qq, kc_ref[0, 0, cols, :].astype(BF16)) + _twice(bp_ref[hd]) + past_bias
        s_n = _dot_nt(qq, kn_all[:, cols]) + _twice(bn_ref[hd])
        m = jnp.maximum(jnp.max(s_p, axis=1, keepdims=True), jnp.max(s_n, axis=1, keepdims=True))
        p_p = jnp.exp2(s_p - m)
        p_n = jnp.exp2(s_n - m)
        l = jnp.sum(p_p, axis=1, keepdims=True) + jnp.sum(p_n, axis=1, keepdims=True)
        v_past = vc_ref[0, 0, pl.ds(hd, p, stride=H_A), :].astype(BF16)
        acc = _dot(p_p.astype(BF16), v_past) + _dot(p_n.astype(BF16), vn_all[:, cols])
        outs.append(_diff_combine(acc, l, t, sc_ref[0], lq1_ref[...], lk1_ref[...],
                                  lq2_ref[...], lk2_ref[...], gd_ref[:, cols]))
    o_ref[0] = jnp.concatenate(outs, axis=1)


def _attn_small(q3, cache_kt, cache_v4, layer, kn, vn, bp, bn, lam_init, lq1, lk1, lq2, lk2, gdiff):
    nb, t, _ = q3.shape
    n_cached, p = cache_kt.shape[1], cache_kt.shape[3]
    smem = pl.BlockSpec(memory_space=pltpu.SMEM)
    vec = lambda: _const_spec((1, DQK))
    cached = lambda b: (layer, jnp.minimum(b, n_cached - 1), 0, 0)
    new = pl.BlockSpec((1, t, W_A), lambda b: (b, 0, 0))
    return pl.pallas_call(
        functools.partial(_attn_small_kernel, t=t, n_cached=n_cached),
        grid=(nb,),
        in_specs=[smem, new,
                  pl.BlockSpec((1, 1, W_A, p), cached), pl.BlockSpec((1, 1, p * H_A, DVA), cached),
                  new, new, _const_spec((H_A, t, p)), _const_spec((H_A, t, t)),
                  vec(), vec(), vec(), vec(), _const_spec((1, W_A))],
        out_specs=new,
        out_shape=jax.ShapeDtypeStruct((nb, t, W_A), BF16),
        compiler_params=pltpu.CompilerParams(dimension_semantics=("arbitrary",)),
        name="attn_small",
    )(lam_init, q3, cache_kt, cache_v4, kn, vn, bp, bn, lq1, lk1, lq2, lk2, gdiff)


_FF_CHUNK = 256


def _ffn_kernel(x_ref, oa_ref, ob_ref, oc_ref, wout_ref, gffn_ref, wup_ref, cw_ref, cb_ref, wdn_ref,
                past_ref, gfin_ref, y_ref, new_ref, halo_scr, *, bb, tt, final_norm):
    rows = bb * tt
    taps = FFN_CONV_W - 1

    @pl.when(pl.program_id(1) == 0)
    def _():
        halo_scr[...] = past_ref[...]

    x1 = (x_ref[...].reshape(rows, D_MODEL)
          + _dot(oa_ref[...].reshape(rows, W_A), wout_ref[0:W_A, :])
          + _dot(ob_ref[...].reshape(rows, C_CONV), wout_ref[W_A:W_A + C_CONV, :])
          + _dot(oc_ref[...].reshape(rows, W_C), wout_ref[W_A + C_CONV:D_MODEL, :]))
    h2 = _rms(x1, gffn_ref[...]).astype(BF16)

    tpos = lax.broadcasted_iota(I32, (bb, tt, _FF_CHUNK), 1)
    acc = jnp.zeros((rows, D_MODEL), F32)
    for c in range(D_FF // _FF_CHUNK):
        lo, hi = c * _FF_CHUNK, (c + 1) * _FF_CHUNK
        gate = _dot(h2, wup_ref[:, lo:hi])
        val = _dot(h2, wup_ref[:, D_FF + lo:D_FF + hi])
        gate3 = gate.reshape(bb, tt, _FF_CHUNK)
        conv = cw_ref[taps:taps + 1, lo:hi] * gate + cb_ref[:, lo:hi]
        for d in range(1, taps + 1):
            shifted = pltpu.roll(gate, d, axis=0).reshape(bb, tt, _FF_CHUNK)
            for e in range(d):
                shifted = jnp.where(tpos == e, halo_scr[:, taps - d + e:taps - d + e + 1, lo:hi], shifted)
            conv = conv + cw_ref[taps - d:taps - d + 1, lo:hi] * shifted.reshape(rows, _FF_CHUNK)
        halo_scr[:, :, lo:hi] = gate3[:, tt - taps:tt, :]
        gelu = 0.5 * conv * (1.0 + jnp.tanh(math.sqrt(2.0 / math.pi) * (conv + 0.044715 * (conv * conv * conv))))
        acc = acc + _dot((gelu * val).astype(BF16), wdn_ref[lo:hi, :])

    y = x1 + acc
    if final_norm:
        y = _rms(y, gfin_ref[...])
    y_ref[...] = y.reshape(bb, tt, D_MODEL)
    new_ref[...] = halo_scr[...]


def _ffn(x3, oa3, ob3, oc3, wout, gffn, wup, cw, cb, wdn, past, gfin, bb, tt, shared_past, final_norm):
    nb, t, _ = x3.shape
    taps = FFN_CONV_W - 1
    assert tt >= taps
    tok = lambda w: pl.BlockSpec((bb, tt, w), lambda bi, ti: (bi, ti, 0))
    pidx = (lambda bi, ti: (0, 0, 0)) if shared_past else (lambda bi, ti: (bi, 0, 0))
    assert not (shared_past and bb != 1)
    return pl.pallas_call(
        functools.partial(_ffn_kernel, bb=bb, tt=tt, final_norm=final_norm),
        grid=(nb // bb, t // tt),
        in_specs=[tok(D_MODEL), tok(W_A), tok(C_CONV), tok(W_C),
                  _const_spec((D_MODEL, D_MODEL)), _const_spec((1, D_MODEL)),
                  _const_spec((D_MODEL, 2 * D_FF)), _const_spec((FFN_CONV_W, D_FF)), _const_spec((1, D_FF)),
                  _const_spec((D_FF, D_MODEL)),
                  pl.BlockSpec((bb, taps, D_FF), pidx), _const_spec((1, D_MODEL))],
        out_specs=[tok(D_MODEL), pl.BlockSpec((bb, taps, D_FF), lambda bi, ti: (bi, 0, 0))],
        out_shape=[jax.ShapeDtypeStruct((nb, t, D_MODEL), F32),
                   jax.ShapeDtypeStruct((nb, taps, D_FF), F32)],
        scratch_shapes=[pltpu.VMEM((bb, taps, D_FF), F32)],
        compiler_params=pltpu.CompilerParams(dimension_semantics=("arbitrary", "arbitrary"),
                                             vmem_limit_bytes=VMEM_LIMIT),
        name="outproj_ffn",
    )(x3, oa3, ob3, oc3, wout, gffn, wup, cw, cb, wdn, past, gfin)


def _seg_matrix(width, group, value, dtype):
    g = jnp.arange(width, dtype=I32) // group
    return jnp.where(g[:, None] == g[None, :], value, 0.0).astype(dtype)


def _state_to_kernel(s):
    n = s.shape[0]
    eye = jnp.eye(H_C, dtype=s.dtype)
    full = jnp.einsum('nhdv,hg->nhvgd', s, eye)
    return full.reshape(n, W_C, H_C * DK_C)


def _state_from_kernel(sf):
    n = sf.shape[0]
    s5 = sf.reshape(n, H_C, DV_C, H_C, DK_C)
    diag = jnp.stack([s5[:, h, :, h, :] for h in range(H_C)], axis=1)
    return jnp.swapaxes(diag, 2, 3)


def _pick_tile(total, want):
    t = min(total, want)
    assert total % t == 0, (total, want)
    return t


def kernel(x_prompt, x_sample, cache_k, cache_v, state_conv, state_hgrn, state_ffn, meta_tokens, rel_bias, g_mix, w_in, g_q, g_k, lam_q1, lam_k1, lam_q2, lam_k2, g_diff, conv_w, conv_b, ln_g, ln_b, lb_logits, g_hgrn, w_out, g_ffn, w_up, ffn_conv_w, ffn_conv_b, w_down, g_final):
    depth = g_mix.shape[0]
    bp, seq, _ = x_prompt.shape
    bs, ts, _ = x_sample.shape
    past_len = cache_k.shape[2]
    assert ts == N_META == REC_BLOCK and meta_tokens.shape[0] == N_META
    assert seq % CHUNK == 0 and past_len % CHUNK == 0

    lb_all = jnp.cumsum(jax.nn.softmax(lb_logits.astype(F32), axis=0), axis=0)
    lb_all = lb_all - lb_all[0:1]
    lb_consts = jnp.stack([jnp.log(lb_all), jnp.log1p(-lb_all), 1.0 - lb_all]
                          + [jnp.zeros_like(lb_all)] * (SUBLANES - 3), axis=1)

    seg_qk = _seg_matrix(W_A, DQK, 1.0 / DQK, BF16)
    seg_mean = _seg_matrix(W_C, DV_C, 1.0 / DV_C, BF16)
    seg_sum = _seg_matrix(W_C, DK_C, 1.0, BF16)
    blockdiag = _seg_matrix(W_C, DK_C, 1.0, F32)

    tq = _pick_tile(seq, 512)
    tk = _pick_tile(tq, 256)
    bt_main = _attn_bias_tiles(rel_bias, tq, tk)
    bt_meta = _bias_tile(rel_bias, N_META, tq, N_META, 0, True, keys_on_rows=True)
    far_bias = rel_bias[_NB - 1, :] * LOG2E
    bias_past = _bias_tile(rel_bias, ts, past_len, past_len, 0, False)
    bias_new = _bias_tile(rel_bias, ts, ts, past_len, past_len, False)

    cache_kt = jnp.swapaxes(cache_k.reshape(depth, bs, past_len, W_A), 2, 3)
    cache_v4 = cache_v.reshape(depth, bs, past_len * H_A, DVA)

    xs = jnp.concatenate([x_sample, meta_tokens.astype(x_sample.dtype)[None]], axis=0)
    xm = x_prompt
    nbs = bs + 1
    zrow = lambda a: jnp.zeros((1,) + a.shape[1:], a.dtype)

    tm_main = _pick_tile(bp * seq, 512)
    small_rows = max(r for r in range(1, 17) if nbs % r == 0)
    outs = {k: [] for k in ("kp", "vp", "cp", "hp", "fp", "ks", "vs", "cs", "hs", "fs")}
    for l in range(depth):
        last = l == depth - 1
        lam_init = jnp.full((1,), 0.8 - 0.6 * math.exp(-0.3 * l), F32)
        win = w_in[l].astype(BF16)
        wout = w_out[l].astype(BF16)
        wup = w_up[l].astype(BF16)
        wdn = w_down[l].astype(BF16)
        gq = jnp.tile(g_q[l], 2 * H_A)[None]
        gk = jnp.tile(g_k[l], 2 * H_A)[None]
        row = lambda a: a[l][None]
        common_attn = (lam_init, row(lam_q1), row(lam_k1), row(lam_q2), row(lam_k2), row(g_diff))

        def mixer_front(x3, tm, feature_major):
            nb, t, _ = x3.shape
            q, kb, vf4, glu, hraw, k_extra, v_extra = _inproj(
                x3.reshape(nb * t, D_MODEL), row(g_mix), win, gq, gk, seg_qk, tm, t, feature_major)
            r3 = lambda a: a.reshape(nb, t, a.shape[-1])
            return (r3(q), r3(kb), vf4.reshape(nb, t, H_A, DVA), r3(glu), r3(hraw),
                    k_extra if feature_major else r3(k_extra), v_extra if feature_major else r3(v_extra))

        q, kb, vf_s, glu, hraw, kf_s, vb = mixer_front(xs, nbs * ts, False)
        oa = _attn_small(q, cache_kt, cache_v4, l, kb, vb, bias_past, bias_new, *common_attn)
        ob, conv_s = _conv_module(glu, jnp.concatenate([state_conv[l], zrow(state_conv[l])]),
                                  conv_w[l], row(conv_b), row(ln_g), row(ln_b), ts, False)
        oc, hg_s = _hgrn(hraw, lb_consts[l], row(g_hgrn), seg_mean, seg_sum, blockdiag,
                         _state_to_kernel(jnp.concatenate([state_hgrn[l], zrow(state_hgrn[l])]).astype(F32)),
                         small_rows, ts, False)
        xs, ffn_s = _ffn(xs, oa, ob, oc, wout, row(g_ffn), wup, ffn_conv_w[l], row(ffn_conv_b), wdn,
                         jnp.concatenate([state_ffn[l], zrow(state_ffn[l])]), g_final[None],
                         nbs, ts, False, last)
        meta_k, meta_vt = kb[bs:], vb[bs].T
        gdt = jnp.broadcast_to(g_diff[l][:, None], (W_A, LANES))

        q, kb, vf_m, glu, hraw, kft_m, vt = mixer_front(xm, tm_main, True)
        oa = _attn_main(q, kb, vt, meta_k, meta_vt, bt_main, bt_meta, far_bias, *common_attn[:-1], gdt, tq, tk)
        ob, conv_m = _conv_module(glu, conv_s[bs:], conv_w[l], row(conv_b), row(ln_g), row(ln_b),
                                  _pick_tile(seq, 512), True)
        oc, hg_m = _hgrn(hraw, lb_consts[l], row(g_hgrn), seg_mean, seg_sum, blockdiag, hg_s[bs:],
                         1, _pick_tile(seq, 256), True)
        xm, ffn_m = _ffn(xm, oa, ob, oc, wout, row(g_ffn), wup, ffn_conv_w[l], row(ffn_conv_b), wdn,
                         ffn_s[bs:], g_final[None], 1, _pick_tile(seq, 512), True, last)

        meta_rows = lambda a: jnp.broadcast_to(a[bs:], (bp,) + a.shape[1:])
        kt_p = jnp.concatenate([meta_rows(jnp.swapaxes(kf_s, 1, 2)), kft_m], axis=2)
        outs["kp"].append(jnp.swapaxes(kt_p, 1, 2).reshape(bp, N_META + seq, H_A, 2, DQK))
        outs["vp"].append(jnp.concatenate([meta_rows(vf_s), vf_m], axis=1))
        outs["cp"].append(conv_m)
        outs["hp"].append(_state_from_kernel(hg_m))
        outs["fp"].append(ffn_m)
        outs["ks"].append(kf_s[:bs].reshape(bs, ts, H_A, 2, DQK))
        outs["vs"].append(vf_s[:bs])
        outs["cs"].append(conv_s[:bs])
        outs["hs"].append(_state_from_kernel(hg_s[:bs]))
        outs["fs"].append(ffn_s[:bs])

    st = {k: jnp.stack(v) for k, v in outs.items()}
    return (xm, xs[:bs], st["kp"], st["vp"], st["cp"], st["hp"], st["fp"],
            st["ks"], st["vs"], st["cs"], st["hs"], st["fs"])
```

```python
import functools
import math

import jax
import jax.numpy as jnp
from jax import lax
from jax.experimental import pallas as pl
from jax.experimental.pallas import tpu as pltpu

F32 = jnp.float32
BF16 = jnp.bfloat16
I32 = jnp.int32

D_MODEL = 1024
N_META = 16
CHUNK = 64
DQK = 64
DVA = 2 * DQK
H_A = 4
W_A = H_A * DVA
C_CONV = 256
CONV_W = 31
H_C = 4
DK_C = 64
DV_C = 64
W_C = H_C * DV_C
D_FF = 2816
FFN_CONV_W = 3
N_BUCKETS = 32
MAX_DIST = 128
REC_BLOCK = 16
EPS = 1e-6
NEG = -1e30
LOG2E = math.log2(math.e)
IN_COLS = 2 * W_A + W_A + 2 * C_CONV + 4 * W_C

LANES = 128
SUBLANES = 8
VMEM_LIMIT = 56 * 1024 * 1024

_NB = N_BUCKETS // 2
_MAX_EXACT = _NB // 2
_BUCKET_THR = tuple(
    next(n for n in range(_MAX_EXACT, MAX_DIST + 1)
         if n ** (_NB - _MAX_EXACT) * _MAX_EXACT ** k >= _MAX_EXACT ** (_NB - _MAX_EXACT) * MAX_DIST ** k)
    for k in range(1, _NB - _MAX_EXACT))


def _rms(x, g):
    return x * lax.rsqrt(jnp.mean(x * x, axis=-1, keepdims=True) + EPS) * g


def _dot(a, b):
    return jnp.dot(a, b, preferred_element_type=F32)


def _dot_nt(a, b):
    return lax.dot_general(a, b, (((1,), (1,)), ((), ())), preferred_element_type=F32)


def _dot_tn(a, b):
    return lax.dot_general(a, b, (((0,), (0,)), ((), ())), preferred_element_type=F32)


def _sigmoid(x):
    return jax.nn.sigmoid(x)


def _const_spec(shape):
    nd = len(shape)
    return pl.BlockSpec(shape, lambda *_: (0,) * nd, pipeline_mode=pl.Buffered(1))


def _bias_kernel(tab_ref, o_ref, *, qpos0, kpos0, prompt_chunks, keys_on_rows):
    h = pl.program_id(0)
    shape = o_ref.shape[1:]
    r = lax.broadcasted_iota(I32, shape, 1 if keys_on_rows else 0) + qpos0
    c = lax.broadcasted_iota(I32, shape, 0 if keys_on_rows else 1) + kpos0
    rel = c - r
    n = jnp.abs(rel)
    large = jnp.full(shape, _MAX_EXACT, I32)
    for thr in _BUCKET_THR:
        large = large + (n >= thr).astype(I32)
    bucket = jnp.where(rel > 0, _NB, 0) + jnp.where(n < _MAX_EXACT, n, large)
    out = jnp.zeros(shape, F32)
    for b in range(N_BUCKETS):
        out = jnp.where(bucket == b, tab_ref[b, h] * LOG2E, out)
    shift = CHUNK - N_META if prompt_chunks else 0
    log2_chunk = CHUNK.bit_length() - 1
    visible = ((c + shift) >> log2_chunk) <= ((r + shift) >> log2_chunk)
    o_ref[0] = jnp.where(visible, out, NEG)


def _bias_tile(rel_bias, rows, cols, qpos0, kpos0, prompt_chunks, keys_on_rows=False):
    return pl.pallas_call(
        functools.partial(_bias_kernel, qpos0=qpos0, kpos0=kpos0, prompt_chunks=prompt_chunks,
                          keys_on_rows=keys_on_rows),
        grid=(H_A,),
        in_specs=[pl.BlockSpec(memory_space=pltpu.SMEM)],
        out_specs=pl.BlockSpec((1, rows, cols), lambda h: (h, 0, 0)),
        out_shape=jax.ShapeDtypeStruct((H_A, rows, cols), F32),
        name="relpos_bias",
    )(rel_bias)


def _inproj_kernel(x_ref, gmix_ref, win_ref, gq_ref, gk_ref, seg_ref,
                   q_ref, kb_ref, vf4_ref, glu_ref, hraw_ref, k_extra_ref, v_extra_ref, *, feature_major):
    tm = x_ref.shape[0]
    h = _rms(x_ref[...], gmix_ref[...]).astype(BF16)

    def proj(lo, hi):
        return _dot(h, win_ref[:, lo:hi])

    def group_norm(a, g):
        ms = _dot((a * a).astype(BF16), seg_ref[...])
        return a * lax.rsqrt(ms + EPS) * g

    qa = group_norm(proj(0, W_A), gq_ref[...])
    q_ref[...] = (qa * (DQK ** -0.5 * LOG2E)).astype(BF16)
    ka = group_norm(proj(W_A, 2 * W_A), gk_ref[...])
    kb_ref[...] = ka.astype(BF16)
    va = proj(2 * W_A, 3 * W_A)
    for hd in range(H_A):
        vf4_ref[pl.ds(hd, tm, stride=H_A), :] = va[:, hd * DVA:(hd + 1) * DVA]
    if feature_major:
        k_extra_ref[0] = ka.T
        v_extra_ref[...] = va.T.astype(BF16)
    else:
        k_extra_ref[...] = ka
        v_extra_ref[...] = va.astype(BF16)
    glu_ref[...] = proj(3 * W_A, 3 * W_A + 2 * C_CONV)
    hraw_ref[...] = proj(3 * W_A + 2 * C_CONV, IN_COLS)


def _inproj(x2d, gmix, win, gq, gk, seg, tm, rows_per_seq, feature_major):
    n = x2d.shape[0]
    row = lambda w: pl.BlockSpec((tm, w), lambda i: (i, 0))
    outs = [(W_A, BF16), (W_A, BF16), None, (2 * C_CONV, F32), (4 * W_C, F32)]
    out_specs = [row(o[0]) if o else pl.BlockSpec((tm * H_A, DVA), lambda i: (i, 0)) for o in outs]
    out_shape = [jax.ShapeDtypeStruct((n, o[0]), o[1]) if o else jax.ShapeDtypeStruct((n * H_A, DVA), F32)
                 for o in outs]
    if feature_major:
        nt = rows_per_seq // tm
        out_specs += [pl.BlockSpec((1, W_A, tm), lambda i: (i // nt, 0, i % nt)),
                      pl.BlockSpec((W_A, tm), lambda i: (0, i))]
        out_shape += [jax.ShapeDtypeStruct((n // rows_per_seq, W_A, rows_per_seq), F32),
                      jax.ShapeDtypeStruct((W_A, n), BF16)]
    else:
        out_specs += [row(W_A), row(W_A)]
        out_shape += [jax.ShapeDtypeStruct((n, W_A), F32), jax.ShapeDtypeStruct((n, W_A), BF16)]
    return pl.pallas_call(
        functools.partial(_inproj_kernel, feature_major=feature_major),
        grid=(n // tm,),
        in_specs=[row(D_MODEL), _const_spec((1, D_MODEL)), _const_spec((D_MODEL, IN_COLS)),
                  _const_spec((1, W_A)), _const_spec((1, W_A)), _const_spec((W_A, W_A))],
        out_specs=out_specs,
        out_shape=out_shape,
        compiler_params=pltpu.CompilerParams(dimension_semantics=("arbitrary",), vmem_limit_bytes=VMEM_LIMIT),
        name="inproj",
    )(x2d, gmix, win, gq, gk, seg)


_CONV_PAD = 32


def _conv_kernel(glu_ref, past_ref, w_ref, b_ref, lng_ref, lnb_ref, ob_ref, new_ref, xc_ref, xr_ref, *, tt, rc):
    halo = CONV_W - 1
    h0 = _CONV_PAD - halo

    @pl.when(pl.program_id(1) == 0)
    def _():
        xc_ref[0:h0, :] = jnp.zeros((h0, C_CONV), F32)
        xc_ref[h0:_CONV_PAD, :] = past_ref[0]

    glu = glu_ref[0]
    xc_ref[_CONV_PAD:_CONV_PAD + tt, :] = glu[:, :C_CONV] * _sigmoid(glu[:, C_CONV:])
    span = xr_ref.shape[1]
    for r in range(1, SUBLANES):
        xr_ref[r - 1] = xc_ref[r:r + span, :]
    bias = b_ref[...]
    for c in range(tt // rc):
        acc = jnp.zeros((rc, C_CONV), F32) + bias
        for j in range(CONV_W):
            r = (j + h0) % SUBLANES
            base = c * rc + j + h0 - r
            rows = xc_ref[base:base + rc, :] if r == 0 else xr_ref[r - 1, base:base + rc, :]
            acc = acc + w_ref[j:j + 1, :] * rows
        mu = jnp.mean(acc, axis=-1, keepdims=True)
        xm = acc - mu
        var = jnp.mean(xm * xm, axis=-1, keepdims=True)
        y = xm * lax.rsqrt(var + EPS) * lng_ref[...] + lnb_ref[...]
        ob_ref[0, c * rc:(c + 1) * rc, :] = (y * _sigmoid(y)).astype(BF16)
    new = xc_ref[h0 + tt:_CONV_PAD + tt, :]
    new_ref[0] = new
    xc_ref[h0:_CONV_PAD, :] = new


def _conv_module(glu3, past, w, b, lng, lnb, tt, shared_past):
    nb, t, _ = glu3.shape
    rc = min(tt, 128)
    pidx = (lambda bi, ti: (0, 0, 0)) if shared_past else (lambda bi, ti: (bi, 0, 0))
    return pl.pallas_call(
        functools.partial(_conv_kernel, tt=tt, rc=rc),
        grid=(nb, t // tt),
        in_specs=[pl.BlockSpec((1, tt, 2 * C_CONV), lambda bi, ti: (bi, ti, 0)),
                  pl.BlockSpec((1, CONV_W - 1, C_CONV), pidx),
                  _const_spec((CONV_W, C_CONV)), _const_spec((1, C_CONV)),
                  _const_spec((1, C_CONV)), _const_spec((1, C_CONV))],
        out_specs=[pl.BlockSpec((1, tt, C_CONV), lambda bi, ti: (bi, ti, 0)),
                   pl.BlockSpec((1, CONV_W - 1, C_CONV), lambda bi, ti: (bi, 0, 0))],
        out_shape=[jax.ShapeDtypeStruct((nb, t, C_CONV), BF16),
                   jax.ShapeDtypeStruct((nb, CONV_W - 1, C_CONV), F32)],
        scratch_shapes=[pltpu.VMEM((_CONV_PAD + tt, C_CONV), F32),
                        pltpu.VMEM((SUBLANES - 1, _CONV_PAD + tt - SUBLANES, C_CONV), F32)],
        compiler_params=pltpu.CompilerParams(dimension_semantics=("arbitrary", "arbitrary")),
        name="conv_module",
    )(glu3, past, w, b, lng, lnb)


def _hgrn_kernel(hraw_ref, lbc_ref, ghg_ref, segm_ref, segs_ref, bd_ref, rsum_ref, s0_ref,
                 oc_ref, sout_ref, s_scr, *, tc, chained):
    nblk = tc // REC_BLOCK
    blk3 = (nblk, REC_BLOCK, W_C)

    if chained:
        @pl.when(pl.program_id(1) == 0)
        def _():
            s_scr[...] = s0_ref[0]

    hr = hraw_ref[...].reshape(tc, 4 * W_C)
    z = hr[:, 0:W_C]
    v = hr[:, W_C:2 * W_C]
    q = hr[:, 2 * W_C:3 * W_C]
    gc = hr[:, 3 * W_C:4 * W_C]
    log_lb = lbc_ref[0:1, :]
    log_1m_lb = lbc_ref[1:2, :]
    one_m_lb = lbc_ref[2:3, :]

    log_sig = jnp.minimum(z, 0.0) - jnp.log(1.0 + jnp.exp(-jnp.abs(z)))
    b_arg = log_1m_lb + log_sig
    log_f = jnp.maximum(log_lb, b_arg) + jnp.log(1.0 + jnp.exp(-jnp.abs(log_lb - b_arg)))
    kc = one_m_lb * _sigmoid(-z)

    pos = lax.broadcasted_iota(I32, (tc, W_C), 0) & (REC_BLOCK - 1)
    bcum = log_f
    sh = 1
    while sh < REC_BLOCK:
        bcum = bcum + jnp.where(pos >= sh, pltpu.roll(bcum, sh, axis=0), 0.0)
        sh *= 2

    b3 = bcum.reshape(blk3)
    b_last = b3[:, REC_BLOCK - 1:REC_BLOCK, :]
    qe = (q * jnp.exp(bcum)).astype(BF16)
    ke = (kc.reshape(blk3) * jnp.exp(b_last - b3)).reshape(tc, W_C).astype(BF16)
    v_bf = v.astype(BF16)
    dl = jnp.exp(b_last)

    half = REC_BLOCK // 2
    slab = (nblk, 2, half, W_C)
    hrows = nblk * half
    b_lo, b_hi = (bcum.reshape(slab)[:, i] for i in range(2))
    k_lo, k_hi = (kc.reshape(slab)[:, i] for i in range(2))
    q_lo, q_hi = (q.reshape(slab)[:, i] for i in range(2))
    v_lo, v_hi = (v.reshape(slab)[:, i] for i in range(2))
    sidx = lax.broadcasted_iota(I32, (nblk, half, W_C), 1)

    def head_sum(x):
        return _dot(x.reshape(hrows, W_C).astype(BF16), segs_ref[...]).reshape(nblk, half, W_C)

    def slab_sum(x):
        return _dot(rsum_ref[...], x.reshape(hrows, W_C).astype(BF16)).reshape(nblk, half, W_C)

    o_lo = jnp.zeros((nblk, half, W_C), F32)
    o_hi = jnp.zeros((nblk, half, W_C), F32)
    for t in range(half):
        bt, qt = b_lo[:, t:t + 1, :], q_lo[:, t:t + 1, :]
        decay = jnp.where(sidx <= t, jnp.exp(jnp.minimum(bt - b_lo, 0.0)), 0.0)
        o_lo = jnp.where(sidx == t, slab_sum(head_sum(decay * k_lo * qt) * v_lo), o_lo)
    for t in range(half):
        bt, qt = b_hi[:, t:t + 1, :], q_hi[:, t:t + 1, :]
        d_lo = jnp.exp(jnp.minimum(bt - b_lo, 0.0))
        d_hi = jnp.where(sidx <= t, jnp.exp(jnp.minimum(bt - b_hi, 0.0)), 0.0)
        av = head_sum(d_lo * k_lo * qt) * v_lo + head_sum(d_hi * k_hi * qt) * v_hi
        o_hi = jnp.where(sidx == t, slab_sum(av), o_hi)
    o_intra = jnp.stack([o_lo, o_hi], axis=1).reshape(tc, W_C)

    blk = lambda a, j: a[j * REC_BLOCK:(j + 1) * REC_BLOCK, :]
    upd = [_dot_tn(blk(v_bf, j), blk(ke, j)) * bd_ref[...] for j in range(nblk)]
    if chained:
        s = s_scr[...]
        s_before = []
        for j in range(nblk):
            s_before.append(s.astype(BF16))
            s = dl[j] * s + upd[j]
        s_scr[...] = s
    else:
        s_before = [s0_ref[j].astype(BF16) for j in range(nblk)]
        for j in range(nblk):
            sout_ref[j] = dl[j] * s0_ref[j] + upd[j]
    o_inter = jnp.concatenate([_dot_nt(blk(qe, j), s_before[j]) for j in range(nblk)], axis=0)

    o = o_intra + o_inter
    ms = _dot((o * o).astype(BF16), segm_ref[...])
    oc = (o * lax.rsqrt(ms + EPS) * ghg_ref[...] * (gc * _sigmoid(gc))).astype(BF16)
    oc_ref[...] = oc.reshape(oc_ref.shape)

    if chained:
        @pl.when(pl.program_id(1) == pl.num_programs(1) - 1)
        def _():
            sout_ref[0] = s_scr[...]


def _hgrn(hraw3, lbc, ghg, segm, segs, bd, s0, rows, tc, shared_past):
    nb, t, _ = hraw3.shape
    chained = rows == 1
    assert chained or (t == tc == REC_BLOCK and not shared_past)
    sidx = (lambda bi, ti: (0, 0, 0)) if shared_past else (lambda bi, ti: (bi, 0, 0))
    hrows = rows * tc // 2
    rsum = _seg_matrix(hrows, REC_BLOCK // 2, 1.0, BF16)
    return pl.pallas_call(
        functools.partial(_hgrn_kernel, tc=rows * tc, chained=chained),
        grid=(nb // rows, t // tc),
        in_specs=[pl.BlockSpec((rows, tc, 4 * W_C), lambda bi, ti: (bi, ti, 0)),
                  _const_spec((SUBLANES, W_C)), _const_spec((1, W_C)),
                  _const_spec((W_C, W_C)), _const_spec((W_C, W_C)), _const_spec((W_C, W_C)),
                  _const_spec((hrows, hrows)),
                  pl.BlockSpec((rows, W_C, W_C), sidx)],
        out_specs=[pl.BlockSpec((rows, tc, W_C), lambda bi, ti: (bi, ti, 0)),
                   pl.BlockSpec((rows, W_C, W_C), lambda bi, ti: (bi, 0, 0))],
        out_shape=[jax.ShapeDtypeStruct((nb, t, W_C), BF16),
                   jax.ShapeDtypeStruct((nb, W_C, W_C), F32)],
        scratch_shapes=[pltpu.VMEM((W_C, W_C), F32)],
        compiler_params=pltpu.CompilerParams(dimension_semantics=("arbitrary", "arbitrary")),
        name="hgrn2",
    )(hraw3, lbc, ghg, segm, segs, bd, rsum, s0)


def _split_halves(q):
    lane = lax.broadcasted_iota(I32, q.shape, 1)
    zero = jnp.zeros_like(q)
    return jnp.concatenate([jnp.where(lane < DQK, q, zero), jnp.where(lane >= DQK, q, zero)], axis=0)


def _twice(b):
    return jnp.concatenate([b, b], axis=0)


def _diff_combine(acc, l, t, lam_init, lq1, lk1, lq2, lk2, gd):
    o = acc / l
    lam = (jnp.exp(jnp.sum(lq1 * lk1, axis=1, keepdims=True))
           - jnp.exp(jnp.sum(lq2 * lk2, axis=1, keepdims=True)) + lam_init)
    a = o[:t] - lam * o[t:]
    return (_rms(a, gd) * (1.0 - lam_init)).astype(BF16)


def _attn_main_kernel(sc_ref, farc_ref, q_ref, k_ref, vt_ref, km_ref, vmt_ref, bt_ref, btm_ref,
                      lq1_ref, lk1_ref, lq2_ref, lk2_ref, gdt_ref, o_ref,
                      acc_scr, s_scr, *, tq, tk):
    h = pl.program_id(1)
    i = pl.program_id(2)
    q = q_ref[0]
    lane = lax.broadcasted_iota(I32, q.shape, 1)
    zero = jnp.zeros_like(q)
    q_half = (jnp.where(lane < DQK, q, zero), jnp.where(lane >= DQK, q, zero))
    n_tiles = (i + 1) * (tq // tk)

    def scores(j, slot):
        k_tile = k_ref[0, pl.ds(pl.multiple_of(j * tk, tk), tk), :]
        for c in range(2):
            s_scr[slot, c] = _dot_nt(k_tile, q_half[c])

    n_var = bt_ref.shape[0]
    far_bias = farc_ref[h]

    def softmax_pv(j, slot, stats, extra=None, all_far=False):
        bidx = jnp.minimum(n_tiles - 1 - j, n_var - 1)
        v_tile = vt_ref[:, pl.ds(pl.multiple_of(j * tk, tk), tk)]
        new_stats = []
        for c in range(2):
            m_prev, l_prev = stats[c]
            strips = []
            for w in range(tq // LANES):
                cols = slice(w * LANES, (w + 1) * LANES)
                if all_far:
                    s = s_scr[slot, c, :, cols]
                    m_new = jnp.maximum(m_prev[:, cols], jnp.max(s, axis=0, keepdims=True) + far_bias)
                    shift = m_new - far_bias
                else:
                    s = s_scr[slot, c, :, cols] + bt_ref[bidx, 0, :, cols]
                    m_new = jnp.maximum(m_prev[:, cols], jnp.max(s, axis=0, keepdims=True))
                    shift = m_new
                if extra is not None:
                    s_x = extra[c][:, cols]
                    m_new = jnp.maximum(m_new, jnp.max(s_x, axis=0, keepdims=True))
                    shift = m_new
                alpha = jnp.exp2(m_prev[:, cols] - m_new)
                p = jnp.exp2(s - shift)
                l_new = alpha * l_prev[:, cols] + jnp.sum(p, axis=0, keepdims=True)
                p_x = None
                if extra is not None:
                    p_x = jnp.exp2(s_x - m_new)
                    l_new = l_new + jnp.sum(p_x, axis=0, keepdims=True)
                    p_x = p_x.astype(BF16)
                strips.append((p.astype(BF16), m_new, l_new, alpha, p_x))
            cat = lambda k: jnp.concatenate([st[k] for st in strips], axis=1)
            pv = _dot(v_tile, cat(0))
            if extra is not None:
                pv = pv + _dot(vmt_ref[...], cat(4))
            acc_scr[c] = cat(3) * acc_scr[c] + pv
            new_stats.append((cat(1), cat(2)))
        return tuple(new_stats)

    def pair(n, stats, all_far):
        scores(2 * n + 1, 1)
        stats = softmax_pv(2 * n, 0, stats, all_far=all_far)
        scores(2 * n + 2, 0)
        return softmax_pv(2 * n + 1, 1, stats, all_far=all_far)

    assert (tq // tk) % 2 == 0
    acc_scr[...] = jnp.zeros(acc_scr.shape, F32)
    stat0 = (jnp.full((1, tq), -jnp.inf, F32), jnp.zeros((1, tq), F32))
    scores(0, 0)
    n_pairs = n_tiles // 2 - 1
    n_far_pairs = jnp.clip((n_tiles + 1 - n_var) // 2, 0, n_pairs)
    stats = lax.fori_loop(0, n_far_pairs, functools.partial(pair, all_far=True), (stat0, stat0))
    stats = lax.fori_loop(n_far_pairs, n_pairs, functools.partial(pair, all_far=False), stats)
    scores(n_tiles - 1, 1)
    stats = softmax_pv(n_tiles - 2, 0, stats)
    bias_meta = jnp.where(i == 0, btm_ref[0], farc_ref[h])
    s_meta = [_dot_nt(km_ref[0], q_half[c]) + bias_meta for c in range(2)]
    stats = softmax_pv(n_tiles - 1, 1, stats, s_meta)
    o = [acc_scr[c] * (1.0 / stats[c][1]) for c in range(2)]

    lam_init = sc_ref[0]
    lam = (jnp.exp(jnp.sum(lq1_ref[...] * lk1_ref[...], axis=1, keepdims=True))
           - jnp.exp(jnp.sum(lq2_ref[...] * lk2_ref[...], axis=1, keepdims=True)) + lam_init)
    a = o[0] - lam * o[1]
    gain = jnp.concatenate([gdt_ref[...]] * (tq // LANES), axis=1)
    y = a * lax.rsqrt(jnp.mean(a * a, axis=0, keepdims=True) + EPS) * gain * (1.0 - lam_init)
    o_ref[0] = y.T.astype(BF16)


def _attn_bias_tiles(rel_bias, tq, tk):
    n_var = -(-(tq + MAX_DIST - 1) // tk) + 1
    base = N_META + n_var * tk
    return jnp.stack([_bias_tile(rel_bias, tk, tq, base, base + tq - (d + 1) * tk, True, keys_on_rows=True)
                      for d in range(n_var)])


def _attn_main(q3, k3, vt, km, vmt, bt, btm, farc, lam_init, lq1, lk1, lq2, lk2, gdt, tq, tk):
    nb, t, _ = q3.shape
    p = km.shape[1]
    assert tq % tk == 0 and tk % CHUNK == 0 and bt.shape[2:] == (tk, tq)
    smem = pl.BlockSpec(memory_space=pltpu.SMEM)
    vec = lambda: _const_spec((1, DQK))
    return pl.pallas_call(
        functools.partial(_attn_main_kernel, tq=tq, tk=tk),
        grid=(nb, H_A, t // tq),
        in_specs=[smem, smem,
                  pl.BlockSpec((1, tq, DVA), lambda b, h, i: (b, i, h)),
                  pl.BlockSpec((1, t, DVA), lambda b, h, i: (b, 0, h)),
                  pl.BlockSpec((DVA, t), lambda b, h, i: (h, b)),
                  pl.BlockSpec((1, p, DVA), lambda b, h, i: (0, 0, h)),
                  pl.BlockSpec((DVA, p), lambda b, h, i: (h, 0)),
                  pl.BlockSpec((bt.shape[0], 1, tk, tq), lambda b, h, i: (0, h, 0, 0)),
                  pl.BlockSpec((1, p, tq), lambda b, h, i: (h, 0, 0)),
                  vec(), vec(), vec(), vec(),
                  pl.BlockSpec((DVA, LANES), lambda b, h, i: (h, 0))],
        out_specs=pl.BlockSpec((1, tq, DVA), lambda b, h, i: (b, i, h)),
        out_shape=jax.ShapeDtypeStruct((nb, t, W_A), BF16),
        scratch_shapes=[pltpu.VMEM((2, DVA, tq), F32), pltpu.VMEM((2, 2, tk, tq), F32)],
        compiler_params=pltpu.CompilerParams(dimension_semantics=("arbitrary",) * 3),
        name="attn_main",
    )(lam_init, farc, q3, k3, vt, km, vmt, bt, btm, lq1, lk1, lq2, lk2, gdt)


def _attn_small_kernel(sc_ref, q_ref, kc_ref, vc_ref, kn_ref, vn_ref, bp_ref, bn_ref,
                       lq1_ref, lk1_ref, lq2_ref, lk2_ref, gd_ref, o_ref, *, t, n_cached):
    b = pl.program_id(0)
    p = vc_ref.shape[2] // H_A
    past_bias = jnp.where(b < n_cached, 0.0, NEG)
    q_all = q_ref[0]
    kn_all = kn_ref[0]
    vn_all = vn_ref[0]
    outs = []
    for hd in range(H_A):
        cols = slice(hd * DVA, (hd + 1) * DVA)
        qq = _split_halves(q_all[:, cols])
        s_p = _dot(qq, kc_ref[0, 0, cols, :].astype(BF16)) + _twice(bp_ref[hd]) + past_bias
        s_n = _dot_nt(qq, kn_all[:, cols]) + _twice(bn_ref[hd])
        m = jnp.maximum(jnp.max(s_p, axis=1, keepdims=True), jnp.max(s_n, axis=1, keepdims=True))
        p_p = jnp.exp2(s_p - m)
        p_n = jnp.exp2(s_n - m)
        l = jnp.sum(p_p, axis=1, keepdims=True) + jnp.sum(p_n, axis=1, keepdims=True)
        v_past = vc_ref[0, 0, pl.ds(hd, p, stride=H_A), :].astype(BF16)
        acc = _dot(p_p.astype(BF16), v_past) + _dot(p_n.astype(BF16), vn_all[:, cols])
        outs.append(_diff_combine(acc, l, t, sc_ref[0], lq1_ref[...], lk1_ref[...],
                                  lq2_ref[...], lk2_ref[...], gd_ref[:, cols]))
    o_ref[0] = jnp.concatenate(outs, axis=1)


def _attn_small(q3, cache_kt, cache_v4, layer, kn, vn, bp, bn, lam_init, lq1, lk1, lq2, lk2, gdiff):
    nb, t, _ = q3.shape
    n_cached, p = cache_kt.shape[1], cache_kt.shape[3]
    smem = pl.BlockSpec(memory_space=pltpu.SMEM)
    vec = lambda: _const_spec((1, DQK))
    cached = lambda b: (layer, jnp.minimum(b, n_cached - 1), 0, 0)
    new = pl.BlockSpec((1, t, W_A), lambda b: (b, 0, 0))
    return pl.pallas_call(
        functools.partial(_attn_small_kernel, t=t, n_cached=n_cached),
        grid=(nb,),
        in_specs=[smem, new,
                  pl.BlockSpec((1, 1, W_A, p), cached), pl.BlockSpec((1, 1, p * H_A, DVA), cached),
                  new, new, _const_spec((H_A, t, p)), _const_spec((H_A, t, t)),
                  vec(), vec(), vec(), vec(), _const_spec((1, W_A))],
        out_specs=new,
        out_shape=jax.ShapeDtypeStruct((nb, t, W_A), BF16),
        compiler_params=pltpu.CompilerParams(dimension_semantics=("arbitrary",)),
        name="attn_small",
    )(lam_init, q3, cache_kt, cache_v4, kn, vn, bp, bn, lq1, lk1, lq2, lk2, gdiff)


_MXU_N = 256
_FF_CHUNK = 6 * _MXU_N
_FF_BOUNDS = tuple((lo, min(lo + _FF_CHUNK, D_FF)) for lo in range(0, D_FF, _FF_CHUNK))
assert all((hi - lo) % _MXU_N == 0 for lo, hi in _FF_BOUNDS)


def _ffn_kernel(x_ref, oa_ref, ob_ref, oc_ref, wout_ref, gffn_ref, wup_ref, cw_ref, cb_ref, wdn_ref,
                past_ref, gfin_ref, y_ref, new_ref, halo_scr, *, bb, tt, final_norm):
    rows = bb * tt
    taps = FFN_CONV_W - 1

    @pl.when(pl.program_id(1) == 0)
    def _():
        halo_scr[...] = past_ref[...]

    x1 = (x_ref[...].reshape(rows, D_MODEL)
          + _dot(oa_ref[...].reshape(rows, W_A), wout_ref[0:W_A, :])
          + _dot(ob_ref[...].reshape(rows, C_CONV), wout_ref[W_A:W_A + C_CONV, :])
          + _dot(oc_ref[...].reshape(rows, W_C), wout_ref[W_A + C_CONV:D_MODEL, :]))
    h2 = _rms(x1, gffn_ref[...]).astype(BF16)

    acc = jnp.zeros((rows, D_MODEL), F32)
    for lo, hi in _FF_BOUNDS:
        gate = _dot(h2, wup_ref[:, lo:hi])
        val = _dot(h2, wup_ref[:, D_FF + lo:D_FF + hi])
        gate3 = gate.reshape(bb, tt, hi - lo)
        tpos = lax.broadcasted_iota(I32, gate3.shape, 1)
        conv = cw_ref[taps:taps + 1, lo:hi] * gate + cb_ref[:, lo:hi]
        for d in range(1, taps + 1):
            shifted = pltpu.roll(gate, d, axis=0).reshape(gate3.shape)
            for e in range(d):
                shifted = jnp.where(tpos == e, halo_scr[:, taps - d + e:taps - d + e + 1, lo:hi], shifted)
            conv = conv + cw_ref[taps - d:taps - d + 1, lo:hi] * shifted.reshape(rows, hi - lo)
        halo_scr[:, :, lo:hi] = gate3[:, tt - taps:tt, :]
        gelu = 0.5 * conv * (1.0 + jnp.tanh(math.sqrt(2.0 / math.pi) * (conv + 0.044715 * (conv * conv * conv))))
        acc = acc + _dot((gelu * val).astype(BF16), wdn_ref[lo:hi, :])

    y = x1 + acc
    if final_norm:
        y = _rms(y, gfin_ref[...])
    y_ref[...] = y.reshape(bb, tt, D_MODEL)
    new_ref[...] = halo_scr[...]


def _ffn(x3, oa3, ob3, oc3, wout, gffn, wup, cw, cb, wdn, past, gfin, bb, tt, shared_past, final_norm):
    nb, t, _ = x3.shape
    taps = FFN_CONV_W - 1
    assert tt >= taps
    tok = lambda w: pl.BlockSpec((bb, tt, w), lambda bi, ti: (bi, ti, 0))
    pidx = (lambda bi, ti: (0, 0, 0)) if shared_past else (lambda bi, ti: (bi, 0, 0))
    assert not (shared_past and bb != 1)
    return pl.pallas_call(
        functools.partial(_ffn_kernel, bb=bb, tt=tt, final_norm=final_norm),
        grid=(nb // bb, t // tt),
        in_specs=[tok(D_MODEL), tok(W_A), tok(C_CONV), tok(W_C),
                  _const_spec((D_MODEL, D_MODEL)), _const_spec((1, D_MODEL)),
                  _const_spec((D_MODEL, 2 * D_FF)), _const_spec((FFN_CONV_W, D_FF)), _const_spec((1, D_FF)),
                  _const_spec((D_FF, D_MODEL)),
                  pl.BlockSpec((bb, taps, D_FF), pidx), _const_spec((1, D_MODEL))],
        out_specs=[tok(D_MODEL), pl.BlockSpec((bb, taps, D_FF), lambda bi, ti: (bi, 0, 0))],
        out_shape=[jax.ShapeDtypeStruct((nb, t, D_MODEL), F32),
                   jax.ShapeDtypeStruct((nb, taps, D_FF), F32)],
        scratch_shapes=[pltpu.VMEM((bb, taps, D_FF), F32)],
        compiler_params=pltpu.CompilerParams(dimension_semantics=("arbitrary", "arbitrary"),
                                             vmem_limit_bytes=VMEM_LIMIT),
        name="outproj_ffn",
    )(x3, oa3, ob3, oc3, wout, gffn, wup, cw, cb, wdn, past, gfin)


def _seg_matrix(width, group, value, dtype):
    g = jnp.arange(width, dtype=I32) // group
    return jnp.where(g[:, None] == g[None, :], value, 0.0).astype(dtype)


def _state_to_kernel(s):
    n = s.shape[0]
    eye = jnp.eye(H_C, dtype=s.dtype)
    full = jnp.einsum('nhdv,hg->nhvgd', s, eye)
    return full.reshape(n, W_C, H_C * DK_C)


def _state_from_kernel(sf):
    n = sf.shape[0]
    s5 = sf.reshape(n, H_C, DV_C, H_C, DK_C)
    diag = jnp.stack([s5[:, h, :, h, :] for h in range(H_C)], axis=1)
    return jnp.swapaxes(diag, 2, 3)


def _pick_tile(total, want):
    t = min(total, want)
    assert total % t == 0, (total, want)
    return t


def kernel(x_prompt, x_sample, cache_k, cache_v, state_conv, state_hgrn, state_ffn, meta_tokens, rel_bias, g_mix, w_in, g_q, g_k, lam_q1, lam_k1, lam_q2, lam_k2, g_diff, conv_w, conv_b, ln_g, ln_b, lb_logits, g_hgrn, w_out, g_ffn, w_up, ffn_conv_w, ffn_conv_b, w_down, g_final):
    depth = g_mix.shape[0]
    bp, seq, _ = x_prompt.shape
    bs, ts, _ = x_sample.shape
    past_len = cache_k.shape[2]
    assert ts == N_META == REC_BLOCK and meta_tokens.shape[0] == N_META
    assert seq % CHUNK == 0 and past_len % CHUNK == 0

    lb_all = jnp.cumsum(jax.nn.softmax(lb_logits.astype(F32), axis=0), axis=0)
    lb_all = lb_all - lb_all[0:1]
    lb_consts = jnp.stack([jnp.log(lb_all), jnp.log1p(-lb_all), 1.0 - lb_all]
                          + [jnp.zeros_like(lb_all)] * (SUBLANES - 3), axis=1)

    seg_qk = _seg_matrix(W_A, DQK, 1.0 / DQK, BF16)
    seg_mean = _seg_matrix(W_C, DV_C, 1.0 / DV_C, BF16)
    seg_sum = _seg_matrix(W_C, DK_C, 1.0, BF16)
    blockdiag = _seg_matrix(W_C, DK_C, 1.0, F32)

    tq = _pick_tile(seq, 512)
    tk = _pick_tile(tq, 256)
    bt_main = _attn_bias_tiles(rel_bias, tq, tk)
    bt_meta = _bias_tile(rel_bias, N_META, tq, N_META, 0, True, keys_on_rows=True)
    far_bias = rel_bias[_NB - 1, :] * LOG2E
    bias_past = _bias_tile(rel_bias, ts, past_len, past_len, 0, False)
    bias_new = _bias_tile(rel_bias, ts, ts, past_len, past_len, False)

    cache_kt = jnp.swapaxes(cache_k.reshape(depth, bs, past_len, W_A), 2, 3)
    cache_v4 = cache_v.reshape(depth, bs, past_len * H_A, DVA)

    xs = jnp.concatenate([x_sample, meta_tokens.astype(x_sample.dtype)[None]], axis=0)
    xm = x_prompt
    nbs = bs + 1
    zrow = lambda a: jnp.zeros((1,) + a.shape[1:], a.dtype)

    tm_main = _pick_tile(bp * seq, 512)
    small_rows = max(r for r in range(1, 17) if nbs % r == 0)
    outs = {k: [] for k in ("kp", "vp", "cp", "hp", "fp", "ks", "vs", "cs", "hs", "fs")}
    for l in range(depth):
        last = l == depth - 1
        lam_init = jnp.full((1,), 0.8 - 0.6 * math.exp(-0.3 * l), F32)
        win = w_in[l].astype(BF16)
        wout = w_out[l].astype(BF16)
        wup = w_up[l].astype(BF16)
        wdn = w_down[l].astype(BF16)
        gq = jnp.tile(g_q[l], 2 * H_A)[None]
        gk = jnp.tile(g_k[l], 2 * H_A)[None]
        row = lambda a: a[l][None]
        common_attn = (lam_init, row(lam_q1), row(lam_k1), row(lam_q2), row(lam_k2), row(g_diff))

        def mixer_front(x3, tm, feature_major):
            nb, t, _ = x3.shape
            q, kb, vf4, glu, hraw, k_extra, v_extra = _inproj(
                x3.reshape(nb * t, D_MODEL), row(g_mix), win, gq, gk, seg_qk, tm, t, feature_major)
            r3 = lambda a: a.reshape(nb, t, a.shape[-1])
            return (r3(q), r3(kb), vf4.reshape(nb, t, H_A, DVA), r3(glu), r3(hraw),
                    k_extra if feature_major else r3(k_extra), v_extra if feature_major else r3(v_extra))

        q, kb, vf_s, glu, hraw, kf_s, vb = mixer_front(xs, nbs * ts, False)
        oa = _attn_small(q, cache_kt, cache_v4, l, kb, vb, bias_past, bias_new, *common_attn)
        ob, conv_s = _conv_module(glu, jnp.concatenate([state_conv[l], zrow(state_conv[l])]),
                                  conv_w[l], row(conv_b), row(ln_g), row(ln_b), ts, False)
        oc, hg_s = _hgrn(hraw, lb_consts[l], row(g_hgrn), seg_mean, seg_sum, blockdiag,
                         _state_to_kernel(jnp.concatenate([state_hgrn[l], zrow(state_hgrn[l])]).astype(F32)),
                         small_rows, ts, False)
        xs, ffn_s = _ffn(xs, oa, ob, oc, wout, row(g_ffn), wup, ffn_conv_w[l], row(ffn_conv_b), wdn,
                         jnp.concatenate([state_ffn[l], zrow(state_ffn[l])]), g_final[None],
                         nbs, ts, False, last)
        meta_k, meta_vt = kb[bs:], vb[bs].T
        gdt = jnp.broadcast_to(g_diff[l][:, None], (W_A, LANES))

        q, kb, vf_m, glu, hraw, kft_m, vt = mixer_front(xm, tm_main, True)
        oa = _attn_main(q, kb, vt, meta_k, meta_vt, bt_main, bt_meta, far_bias, *common_attn[:-1], gdt, tq, tk)
        ob, conv_m = _conv_module(glu, conv_s[bs:], conv_w[l], row(conv_b), row(ln_g), row(ln_b),
                                  _pick_tile(seq, 512), True)
        oc, hg_m = _hgrn(hraw, lb_consts[l], row(g_hgrn), seg_mean, seg_sum, blockdiag, hg_s[bs:],
                         1, _pick_tile(seq, 256), True)
        xm, ffn_m = _ffn(xm, oa, ob, oc, wout, row(g_ffn), wup, ffn_conv_w[l], row(ffn_conv_b), wdn,
                         ffn_s[bs:], g_final[None], 1, _pick_tile(seq, 512), True, last)

        meta_rows = lambda a: jnp.broadcast_to(a[bs:], (bp,) + a.shape[1:])
        kt_p = jnp.concatenate([meta_rows(jnp.swapaxes(kf_s, 1, 2)), kft_m], axis=2)
        outs["kp"].append(jnp.swapaxes(kt_p, 1, 2).reshape(bp, N_META + seq, H_A, 2, DQK))
        outs["vp"].append(jnp.concatenate([meta_rows(vf_s), vf_m], axis=1))
        outs["cp"].append(conv_m)
        outs["hp"].append(_state_from_kernel(hg_m))
        outs["fp"].append(ffn_m)
        outs["ks"].append(kf_s[:bs].reshape(bs, ts, H_A, 2, DQK))
        outs["vs"].append(vf_s[:bs])
        outs["cs"].append(conv_s[:bs])
        outs["hs"].append(_state_from_kernel(hg_s[:bs]))
        outs["fs"].append(ffn_s[:bs])

    st = {k: jnp.stack(v) for k, v in outs.items()}
    return (xm, xs[:bs], st["kp"], st["vp"], st["cp"], st["hp"], st["fp"],
            st["ks"], st["vs"], st["cs"], st["hs"], st["fs"])
```

```python
import functools
import math

import jax
import jax.numpy as jnp
from jax import lax
from jax.experimental import pallas as pl
from jax.experimental.pallas import tpu as pltpu

F32 = jnp.float32
BF16 = jnp.bfloat16
I32 = jnp.int32

D_MODEL = 1024
N_META = 16
CHUNK = 64
DQK = 64
DVA = 2 * DQK
H_A = 4
W_A = H_A * DVA
C_CONV = 256
CONV_W = 31
H_C = 4
DK_C = 64
DV_C = 64
W_C = H_C * DV_C
D_FF = 2816
FFN_CONV_W = 3
N_BUCKETS = 32
MAX_DIST = 128
REC_BLOCK = 16
EPS = 1e-6
NEG = -1e30
LOG2E = math.log2(math.e)
IN_COLS = 2 * W_A + W_A + 2 * C_CONV + 4 * W_C

LANES = 128
SUBLANES = 8
VMEM_LIMIT = 56 * 1024 * 1024

_NB = N_BUCKETS // 2
_MAX_EXACT = _NB // 2
_BUCKET_THR = tuple(
    next(n for n in range(_MAX_EXACT, MAX_DIST + 1)
         if n ** (_NB - _MAX_EXACT) * _MAX_EXACT ** k >= _MAX_EXACT ** (_NB - _MAX_EXACT) * MAX_DIST ** k)
    for k in range(1, _NB - _MAX_EXACT))


def _rms(x, g):
    return x * lax.rsqrt(jnp.mean(x * x, axis=-1, keepdims=True) + EPS) * g


def _dot(a, b):
    return jnp.dot(a, b, preferred_element_type=F32)


def _dot_nt(a, b):
    return lax.dot_general(a, b, (((1,), (1,)), ((), ())), preferred_element_type=F32)


def _dot_tn(a, b):
    return lax.dot_general(a, b, (((0,), (0,)), ((), ())), preferred_element_type=F32)


def _sigmoid(x):
    return jax.nn.sigmoid(x)


def _const_spec(shape):
    nd = len(shape)
    return pl.BlockSpec(shape, lambda *_: (0,) * nd, pipeline_mode=pl.Buffered(1))


def _bias_kernel(tab_ref, o_ref, *, qpos0, kpos0, prompt_chunks, keys_on_rows):
    h = pl.program_id(0)
    shape = o_ref.shape[1:]
    r = lax.broadcasted_iota(I32, shape, 1 if keys_on_rows else 0) + qpos0
    c = lax.broadcasted_iota(I32, shape, 0 if keys_on_rows else 1) + kpos0
    rel = c - r
    n = jnp.abs(rel)
    large = jnp.full(shape, _MAX_EXACT, I32)
    for thr in _BUCKET_THR:
        large = large + (n >= thr).astype(I32)
    bucket = jnp.where(rel > 0, _NB, 0) + jnp.where(n < _MAX_EXACT, n, large)
    out = jnp.zeros(shape, F32)
    for b in range(N_BUCKETS):
        out = jnp.where(bucket == b, tab_ref[b, h] * LOG2E, out)
    shift = CHUNK - N_META if prompt_chunks else 0
    log2_chunk = CHUNK.bit_length() - 1
    visible = ((c + shift) >> log2_chunk) <= ((r + shift) >> log2_chunk)
    o_ref[0] = jnp.where(visible, out, NEG)


def _bias_tile(rel_bias, rows, cols, qpos0, kpos0, prompt_chunks, keys_on_rows=False):
    return pl.pallas_call(
        functools.partial(_bias_kernel, qpos0=qpos0, kpos0=kpos0, prompt_chunks=prompt_chunks,
                          keys_on_rows=keys_on_rows),
        grid=(H_A,),
        in_specs=[pl.BlockSpec(memory_space=pltpu.SMEM)],
        out_specs=pl.BlockSpec((1, rows, cols), lambda h: (h, 0, 0)),
        out_shape=jax.ShapeDtypeStruct((H_A, rows, cols), F32),
        name="relpos_bias",
    )(rel_bias)


def _inproj_kernel(x_ref, gmix_ref, win_ref, gq_ref, gk_ref, seg_ref,
                   q_ref, kb_ref, vf4_ref, glu_ref, hraw_ref, k_extra_ref, v_extra_ref, *, feature_major):
    tm = x_ref.shape[0]
    h = _rms(x_ref[...], gmix_ref[...]).astype(BF16)

    def proj(lo, hi):
        return _dot(h, win_ref[:, lo:hi])

    def group_norm(a, g):
        ms = _dot((a * a).astype(BF16), seg_ref[...])
        return a * lax.rsqrt(ms + EPS) * g

    qa = group_norm(proj(0, W_A), gq_ref[...])
    q_ref[...] = (qa * (DQK ** -0.5 * LOG2E)).astype(BF16)
    ka = group_norm(proj(W_A, 2 * W_A), gk_ref[...])
    kb_ref[...] = ka.astype(BF16)
    va = proj(2 * W_A, 3 * W_A)
    for hd in range(H_A):
        vf4_ref[pl.ds(hd, tm, stride=H_A), :] = va[:, hd * DVA:(hd + 1) * DVA]
    if feature_major:
        k_extra_ref[0] = ka.T
        v_extra_ref[...] = va.T.astype(BF16)
    else:
        k_extra_ref[...] = ka
        v_extra_ref[...] = va.astype(BF16)
    glu_ref[...] = proj(3 * W_A, 3 * W_A + 2 * C_CONV)
    hraw_ref[...] = proj(3 * W_A + 2 * C_CONV, IN_COLS)


def _inproj(x2d, gmix, win, gq, gk, seg, tm, rows_per_seq, feature_major):
    n = x2d.shape[0]
    row = lambda w: pl.BlockSpec((tm, w), lambda i: (i, 0))
    outs = [(W_A, BF16), (W_A, BF16), None, (2 * C_CONV, F32), (4 * W_C, F32)]
    out_specs = [row(o[0]) if o else pl.BlockSpec((tm * H_A, DVA), lambda i: (i, 0)) for o in outs]
    out_shape = [jax.ShapeDtypeStruct((n, o[0]), o[1]) if o else jax.ShapeDtypeStruct((n * H_A, DVA), F32)
                 for o in outs]
    if feature_major:
        nt = rows_per_seq // tm
        out_specs += [pl.BlockSpec((1, W_A, tm), lambda i: (i // nt, 0, i % nt)),
                      pl.BlockSpec((W_A, tm), lambda i: (0, i))]
        out_shape += [jax.ShapeDtypeStruct((n // rows_per_seq, W_A, rows_per_seq), F32),
                      jax.ShapeDtypeStruct((W_A, n), BF16)]
    else:
        out_specs += [row(W_A), row(W_A)]
        out_shape += [jax.ShapeDtypeStruct((n, W_A), F32), jax.ShapeDtypeStruct((n, W_A), BF16)]
    return pl.pallas_call(
        functools.partial(_inproj_kernel, feature_major=feature_major),
        grid=(n // tm,),
        in_specs=[row(D_MODEL), _const_spec((1, D_MODEL)), _const_spec((D_MODEL, IN_COLS)),
                  _const_spec((1, W_A)), _const_spec((1, W_A)), _const_spec((W_A, W_A))],
        out_specs=out_specs,
        out_shape=out_shape,
        compiler_params=pltpu.CompilerParams(dimension_semantics=("arbitrary",), vmem_limit_bytes=VMEM_LIMIT),
        name="inproj",
    )(x2d, gmix, win, gq, gk, seg)


_CONV_PAD = 32


def _conv_kernel(glu_ref, past_ref, w_ref, b_ref, lng_ref, lnb_ref, ob_ref, new_ref, xc_ref, xr_ref, *, tt, rc):
    halo = CONV_W - 1
    h0 = _CONV_PAD - halo

    @pl.when(pl.program_id(1) == 0)
    def _():
        xc_ref[0:h0, :] = jnp.zeros((h0, C_CONV), F32)
        xc_ref[h0:_CONV_PAD, :] = past_ref[0]

    glu = glu_ref[0]
    xc_ref[_CONV_PAD:_CONV_PAD + tt, :] = glu[:, :C_CONV] * _sigmoid(glu[:, C_CONV:])
    span = xr_ref.shape[1]
    for r in range(1, SUBLANES):
        xr_ref[r - 1] = xc_ref[r:r + span, :]
    bias = b_ref[...]
    for c in range(tt // rc):
        acc = jnp.zeros((rc, C_CONV), F32) + bias
        for j in range(CONV_W):
            r = (j + h0) % SUBLANES
            base = c * rc + j + h0 - r
            rows = xc_ref[base:base + rc, :] if r == 0 else xr_ref[r - 1, base:base + rc, :]
            acc = acc + w_ref[j:j + 1, :] * rows
        mu = jnp.mean(acc, axis=-1, keepdims=True)
        xm = acc - mu
        var = jnp.mean(xm * xm, axis=-1, keepdims=True)
        y = xm * lax.rsqrt(var + EPS) * lng_ref[...] + lnb_ref[...]
        ob_ref[0, c * rc:(c + 1) * rc, :] = (y * _sigmoid(y)).astype(BF16)
    new = xc_ref[h0 + tt:_CONV_PAD + tt, :]
    new_ref[0] = new
    xc_ref[h0:_CONV_PAD, :] = new


def _conv_module(glu3, past, w, b, lng, lnb, tt, shared_past):
    nb, t, _ = glu3.shape
    rc = min(tt, 128)
    pidx = (lambda bi, ti: (0, 0, 0)) if shared_past else (lambda bi, ti: (bi, 0, 0))
    return pl.pallas_call(
        functools.partial(_conv_kernel, tt=tt, rc=rc),
        grid=(nb, t // tt),
        in_specs=[pl.BlockSpec((1, tt, 2 * C_CONV), lambda bi, ti: (bi, ti, 0)),
                  pl.BlockSpec((1, CONV_W - 1, C_CONV), pidx),
                  _const_spec((CONV_W, C_CONV)), _const_spec((1, C_CONV)),
                  _const_spec((1, C_CONV)), _const_spec((1, C_CONV))],
        out_specs=[pl.BlockSpec((1, tt, C_CONV), lambda bi, ti: (bi, ti, 0)),
                   pl.BlockSpec((1, CONV_W - 1, C_CONV), lambda bi, ti: (bi, 0, 0))],
        out_shape=[jax.ShapeDtypeStruct((nb, t, C_CONV), BF16),
                   jax.ShapeDtypeStruct((nb, CONV_W - 1, C_CONV), F32)],
        scratch_shapes=[pltpu.VMEM((_CONV_PAD + tt, C_CONV), F32),
                        pltpu.VMEM((SUBLANES - 1, _CONV_PAD + tt - SUBLANES, C_CONV), F32)],
        compiler_params=pltpu.CompilerParams(dimension_semantics=("arbitrary", "arbitrary")),
        name="conv_module",
    )(glu3, past, w, b, lng, lnb)


def _hgrn_kernel(hraw_ref, lbc_ref, ghg_ref, segm_ref, segs_ref, bd_ref, rsum_ref, s0_ref,
                 oc_ref, sout_ref, s_scr, *, tc, chained):
    nblk = tc // REC_BLOCK
    blk3 = (nblk, REC_BLOCK, W_C)

    if chained:
        @pl.when(pl.program_id(1) == 0)
        def _():
            s_scr[...] = s0_ref[0]

    hr = hraw_ref[...].reshape(tc, 4 * W_C)
    z = hr[:, 0:W_C]
    v = hr[:, W_C:2 * W_C]
    q = hr[:, 2 * W_C:3 * W_C]
    gc = hr[:, 3 * W_C:4 * W_C]
    log_lb = lbc_ref[0:1, :]
    log_1m_lb = lbc_ref[1:2, :]
    one_m_lb = lbc_ref[2:3, :]

    log_sig = jnp.minimum(z, 0.0) - jnp.log(1.0 + jnp.exp(-jnp.abs(z)))
    b_arg = log_1m_lb + log_sig
    log_f = jnp.maximum(log_lb, b_arg) + jnp.log(1.0 + jnp.exp(-jnp.abs(log_lb - b_arg)))
    kc = one_m_lb * _sigmoid(-z)

    pos = lax.broadcasted_iota(I32, (tc, W_C), 0) & (REC_BLOCK - 1)
    bcum = log_f
    sh = 1
    while sh < REC_BLOCK:
        bcum = bcum + jnp.where(pos >= sh, pltpu.roll(bcum, sh, axis=0), 0.0)
        sh *= 2

    b3 = bcum.reshape(blk3)
    b_last = b3[:, REC_BLOCK - 1:REC_BLOCK, :]
    qe = (q * jnp.exp(bcum)).astype(BF16)
    ke = (kc.reshape(blk3) * jnp.exp(b_last - b3)).reshape(tc, W_C).astype(BF16)
    v_bf = v.astype(BF16)
    dl = jnp.exp(b_last)

    half = REC_BLOCK // 2
    slab = (nblk, 2, half, W_C)
    hrows = nblk * half
    b_lo, b_hi = (bcum.reshape(slab)[:, i] for i in range(2))
    k_lo, k_hi = (kc.reshape(slab)[:, i] for i in range(2))
    q_lo, q_hi = (q.reshape(slab)[:, i] for i in range(2))
    v_lo, v_hi = (v.reshape(slab)[:, i] for i in range(2))
    sidx = lax.broadcasted_iota(I32, (nblk, half, W_C), 1)

    def head_sum(x):
        return _dot(x.reshape(hrows, W_C).astype(BF16), segs_ref[...]).reshape(nblk, half, W_C)

    def slab_sum(x):
        return _dot(rsum_ref[...], x.reshape(hrows, W_C).astype(BF16)).reshape(nblk, half, W_C)

    o_lo = jnp.zeros((nblk, half, W_C), F32)
    o_hi = jnp.zeros((nblk, half, W_C), F32)
    for t in range(half):
        bt, qt = b_lo[:, t:t + 1, :], q_lo[:, t:t + 1, :]
        decay = jnp.where(sidx <= t, jnp.exp(jnp.minimum(bt - b_lo, 0.0)), 0.0)
        o_lo = jnp.where(sidx == t, slab_sum(head_sum(decay * k_lo * qt) * v_lo), o_lo)
    for t in range(half):
        bt, qt = b_hi[:, t:t + 1, :], q_hi[:, t:t + 1, :]
        d_lo = jnp.exp(jnp.minimum(bt - b_lo, 0.0))
        d_hi = jnp.where(sidx <= t, jnp.exp(jnp.minimum(bt - b_hi, 0.0)), 0.0)
        av = head_sum(d_lo * k_lo * qt) * v_lo + head_sum(d_hi * k_hi * qt) * v_hi
        o_hi = jnp.where(sidx == t, slab_sum(av), o_hi)
    o_intra = jnp.stack([o_lo, o_hi], axis=1).reshape(tc, W_C)

    blk = lambda a, j: a[j * REC_BLOCK:(j + 1) * REC_BLOCK, :]
    upd = [_dot_tn(blk(v_bf, j), blk(ke, j)) * bd_ref[...] for j in range(nblk)]
    if chained:
        s = s_scr[...]
        s_before = []
        for j in range(nblk):
            s_before.append(s.astype(BF16))
            s = dl[j] * s + upd[j]
        s_scr[...] = s
    else:
        s_before = [s0_ref[j].astype(BF16) for j in range(nblk)]
        for j in range(nblk):
            sout_ref[j] = dl[j] * s0_ref[j] + upd[j]
    o_inter = jnp.concatenate([_dot_nt(blk(qe, j), s_before[j]) for j in range(nblk)], axis=0)

    o = o_intra + o_inter
    ms = _dot((o * o).astype(BF16), segm_ref[...])
    oc = (o * lax.rsqrt(ms + EPS) * ghg_ref[...] * (gc * _sigmoid(gc))).astype(BF16)
    oc_ref[...] = oc.reshape(oc_ref.shape)

    if chained:
        @pl.when(pl.program_id(1) == pl.num_programs(1) - 1)
        def _():
            sout_ref[0] = s_scr[...]


def _hgrn(hraw3, lbc, ghg, segm, segs, bd, s0, rows, tc, shared_past):
    nb, t, _ = hraw3.shape
    chained = rows == 1
    assert chained or (t == tc == REC_BLOCK and not shared_past)
    sidx = (lambda bi, ti: (0, 0, 0)) if shared_past else (lambda bi, ti: (bi, 0, 0))
    hrows = rows * tc // 2
    rsum = _seg_matrix(hrows, REC_BLOCK // 2, 1.0, BF16)
    return pl.pallas_call(
        functools.partial(_hgrn_kernel, tc=rows * tc, chained=chained),
        grid=(nb // rows, t // tc),
        in_specs=[pl.BlockSpec((rows, tc, 4 * W_C), lambda bi, ti: (bi, ti, 0)),
                  _const_spec((SUBLANES, W_C)), _const_spec((1, W_C)),
                  _const_spec((W_C, W_C)), _const_spec((W_C, W_C)), _const_spec((W_C, W_C)),
                  _const_spec((hrows, hrows)),
                  pl.BlockSpec((rows, W_C, W_C), sidx)],
        out_specs=[pl.BlockSpec((rows, tc, W_C), lambda bi, ti: (bi, ti, 0)),
                   pl.BlockSpec((rows, W_C, W_C), lambda bi, ti: (bi, 0, 0))],
        out_shape=[jax.ShapeDtypeStruct((nb, t, W_C), BF16),
                   jax.ShapeDtypeStruct((nb, W_C, W_C), F32)],
        scratch_shapes=[pltpu.VMEM((W_C, W_C), F32)],
        compiler_params=pltpu.CompilerParams(dimension_semantics=("arbitrary", "arbitrary")),
        name="hgrn2",
    )(hraw3, lbc, ghg, segm, segs, bd, rsum, s0)


def _split_halves(q):
    lane = lax.broadcasted_iota(I32, q.shape, 1)
    zero = jnp.zeros_like(q)
    return jnp.concatenate([jnp.where(lane < DQK, q, zero), jnp.where(lane >= DQK, q, zero)], axis=0)


def _twice(b):
    return jnp.concatenate([b, b], axis=0)


def _diff_combine(acc, l, t, lam_init, lq1, lk1, lq2, lk2, gd):
    o = acc / l
    lam = (jnp.exp(jnp.sum(lq1 * lk1, axis=1, keepdims=True))
           - jnp.exp(jnp.sum(lq2 * lk2, axis=1, keepdims=True)) + lam_init)
    a = o[:t] - lam * o[t:]
    return (_rms(a, gd) * (1.0 - lam_init)).astype(BF16)


def _attn_main_kernel(sc_ref, farc_ref, q_ref, k_ref, vt_ref, km_ref, vmt_ref, bt_ref, btm_ref,
                      lq1_ref, lk1_ref, lq2_ref, lk2_ref, gdt_ref, o_ref,
                      acc_scr, s_scr, *, tq, tk):
    h = pl.program_id(1)
    i = pl.program_id(2)
    q = q_ref[0]
    lane = lax.broadcasted_iota(I32, q.shape, 1)
    zero = jnp.zeros_like(q)
    q_half = (jnp.where(lane < DQK, q, zero), jnp.where(lane >= DQK, q, zero))
    n_tiles = (i + 1) * (tq // tk)

    def scores(j, slot):
        k_tile = k_ref[0, pl.ds(pl.multiple_of(j * tk, tk), tk), :]
        for c in range(2):
            s_scr[slot, c] = _dot_nt(k_tile, q_half[c])

    n_var = bt_ref.shape[0]
    far_bias = farc_ref[h]

    ones_rows = jnp.ones((2 * SUBLANES, tk), BF16)

    def softmax_pv(j, slot, stats, extra=None, all_far=False, masked_cols=0):
        bidx = jnp.minimum(n_tiles - 1 - j, n_var - 1)
        v_ones = jnp.concatenate([vt_ref[:, pl.ds(pl.multiple_of(j * tk, tk), tk)], ones_rows], axis=0)
        new_stats = []
        for c in range(2):
            m_prev, l_prev = stats[c]
            strips = []
            for w in range(tq // LANES):
                cols = slice(w * LANES, (w + 1) * LANES)
                visible = (w + 1) * LANES > masked_cols
                m_new = m_prev[:, cols]
                if visible and all_far:
                    s = s_scr[slot, c, :, cols]
                    m_new = jnp.maximum(m_new, jnp.max(s, axis=0, keepdims=True) + far_bias)
                elif visible:
                    s = s_scr[slot, c, :, cols] + bt_ref[bidx, 0, :, cols]
                    m_new = jnp.maximum(m_new, jnp.max(s, axis=0, keepdims=True))
                if extra is not None:
                    s_x = extra[c][:, cols]
                    m_new = jnp.maximum(m_new, jnp.max(s_x, axis=0, keepdims=True))
                shift = m_new - far_bias if all_far else m_new
                alpha = jnp.exp2(m_prev[:, cols] - m_new)
                p = jnp.exp2((s - shift).astype(BF16)) if visible else jnp.zeros((tk, LANES), BF16)
                p_x = l_x = None
                if extra is not None:
                    p_x = jnp.exp2(s_x - m_new)
                    l_x = jnp.sum(p_x, axis=0, keepdims=True)
                    p_x = p_x.astype(BF16)
                strips.append((p, m_new, alpha, p_x, l_x))
            cat = lambda k: jnp.concatenate([st[k] for st in strips], axis=1)
            p_tile, alpha = cat(0), cat(2)
            pv_sum = _dot(v_ones, p_tile)
            pv = pv_sum[0:DVA, :]
            l_new = alpha * l_prev + pv_sum[DVA:DVA + 1, :]
            if extra is not None:
                l_new = l_new + cat(4)
                pv = pv + _dot(vmt_ref[...], cat(3))
            acc_scr[c] = alpha * acc_scr[c] + pv
            new_stats.append((cat(1), l_new))
        return tuple(new_stats)

    def pair(n, stats, all_far):
        scores(2 * n + 1, 1)
        stats = softmax_pv(2 * n, 0, stats, all_far=all_far)
        scores(2 * n + 2, 0)
        return softmax_pv(2 * n + 1, 1, stats, all_far=all_far)

    assert (tq // tk) % 2 == 0
    acc_scr[...] = jnp.zeros(acc_scr.shape, F32)
    stat0 = (jnp.full((1, tq), -jnp.inf, F32), jnp.zeros((1, tq), F32))
    scores(0, 0)
    n_pairs = n_tiles // 2 - 1
    n_far_pairs = jnp.clip((n_tiles + 1 - n_var) // 2, 0, n_pairs)
    stats = lax.fori_loop(0, n_far_pairs, functools.partial(pair, all_far=True), (stat0, stat0))
    stats = lax.fori_loop(n_far_pairs, n_pairs, functools.partial(pair, all_far=False), stats)
    scores(n_tiles - 1, 1)
    stats = softmax_pv(n_tiles - 2, 0, stats)
    bias_meta = jnp.where(i == 0, btm_ref[0], farc_ref[h])
    s_meta = [_dot_nt(km_ref[0], q_half[c]) + bias_meta for c in range(2)]
    stats = softmax_pv(n_tiles - 1, 1, stats, s_meta, masked_cols=tq - tk)
    o = [acc_scr[c] * (1.0 / stats[c][1]) for c in range(2)]

    lam_init = sc_ref[0]
    lam = (jnp.exp(jnp.sum(lq1_ref[...] * lk1_ref[...], axis=1, keepdims=True))
           - jnp.exp(jnp.sum(lq2_ref[...] * lk2_ref[...], axis=1, keepdims=True)) + lam_init)
    a = o[0] - lam * o[1]
    gain = jnp.concatenate([gdt_ref[...]] * (tq // LANES), axis=1)
    y = a * lax.rsqrt(jnp.mean(a * a, axis=0, keepdims=True) + EPS) * gain * (1.0 - lam_init)
    o_ref[0] = y.T.astype(BF16)


def _attn_bias_tiles(rel_bias, tq, tk):
    n_var = -(-(tq + MAX_DIST - 1) // tk) + 1
    base = N_META + n_var * tk
    return jnp.stack([_bias_tile(rel_bias, tk, tq, base, base + tq - (d + 1) * tk, True, keys_on_rows=True)
                      for d in range(n_var)])


def _attn_main(q3, k3, vt, km, vmt, bt, btm, farc, lam_init, lq1, lk1, lq2, lk2, gdt, tq, tk):
    nb, t, _ = q3.shape
    p = km.shape[1]
    assert tq % tk == 0 and tk % CHUNK == 0 and bt.shape[2:] == (tk, tq)
    smem = pl.BlockSpec(memory_space=pltpu.SMEM)
    vec = lambda: _const_spec((1, DQK))
    return pl.pallas_call(
        functools.partial(_attn_main_kernel, tq=tq, tk=tk),
        grid=(nb, H_A, t // tq),
        in_specs=[smem, smem,
                  pl.BlockSpec((1, tq, DVA), lambda b, h, i: (b, i, h)),
                  pl.BlockSpec((1, t, DVA), lambda b, h, i: (b, 0, h)),
                  pl.BlockSpec((DVA, t), lambda b, h, i: (h, b)),
                  pl.BlockSpec((1, p, DVA), lambda b, h, i: (0, 0, h)),
                  pl.BlockSpec((DVA, p), lambda b, h, i: (h, 0)),
                  pl.BlockSpec((bt.shape[0], 1, tk, tq), lambda b, h, i: (0, h, 0, 0)),
                  pl.BlockSpec((1, p, tq), lambda b, h, i: (h, 0, 0)),
                  vec(), vec(), vec(), vec(),
                  pl.BlockSpec((DVA, LANES), lambda b, h, i: (h, 0))],
        out_specs=pl.BlockSpec((1, tq, DVA), lambda b, h, i: (b, i, h)),
        out_shape=jax.ShapeDtypeStruct((nb, t, W_A), BF16),
        scratch_shapes=[pltpu.VMEM((2, DVA, tq), F32), pltpu.VMEM((2, 2, tk, tq), F32)],
        compiler_params=pltpu.CompilerParams(dimension_semantics=("arbitrary",) * 3),
        name="attn_main",
    )(lam_init, farc, q3, k3, vt, km, vmt, bt, btm, lq1, lk1, lq2, lk2, gdt)


def _attn_small_kernel(sc_ref, q_ref, kc_ref, vc_ref, kn_ref, vn_ref, bp_ref, bn_ref,
                       lq1_ref, lk1_ref, lq2_ref, lk2_ref, gd_ref, o_ref, *, t, n_cached):
    b = pl.program_id(0)
    p = vc_ref.shape[2] // H_A
    past_bias = jnp.where(b < n_cached, 0.0, NEG)
    q_all = q_ref[0]
    kn_all = kn_ref[0]
    vn_all = vn_ref[0]
    outs = []
    for hd in range(H_A):
        cols = slice(hd * DVA, (hd + 1) * DVA)
        qq = _split_halves(q_all[:, cols])
        s_p = _dot(qq, kc_ref[0, 0, cols, :].astype(BF16)) + _twice(bp_ref[hd]) + past_bias
        s_n = _dot_nt(qq, kn_all[:, cols]) + _twice(bn_ref[hd])
        m = jnp.maximum(jnp.max(s_p, axis=1, keepdims=True), jnp.max(s_n, axis=1, keepdims=True))
        p_p = jnp.exp2(s_p - m)
        p_n = jnp.exp2(s_n - m)
        l = jnp.sum(p_p, axis=1, keepdims=True) + jnp.sum(p_n, axis=1, keepdims=True)
        v_past = vc_ref[0, 0, pl.ds(hd, p, stride=H_A), :].astype(BF16)
        acc = _dot(p_p.astype(BF16), v_past) + _dot(p_n.astype(BF16), vn_all[:, cols])
        outs.append(_diff_combine(acc, l, t, sc_ref[0], lq1_ref[...], lk1_ref[...],
                                  lq2_ref[...], lk2_ref[...], gd_ref[:, cols]))
    o_ref[0] = jnp.concatenate(outs, axis=1)


def _attn_small(q3, cache_kt, cache_v4, layer, kn, vn, bp, bn, lam_init, lq1, lk1, lq2, lk2, gdiff):
    nb, t, _ = q3.shape
    n_cached, p = cache_kt.shape[1], cache_kt.shape[3]
    smem = pl.BlockSpec(memory_space=pltpu.SMEM)
    vec = lambda: _const_spec((1, DQK))
    cached = lambda b: (layer, jnp.minimum(b, n_cached - 1), 0, 0)
    new = pl.BlockSpec((1, t, W_A), lambda b: (b, 0, 0))
    return pl.pallas_call(
        functools.partial(_attn_small_kernel, t=t, n_cached=n_cached),
        grid=(nb,),
        in_specs=[smem, new,
                  pl.BlockSpec((1, 1, W_A, p), cached), pl.BlockSpec((1, 1, p * H_A, DVA), cached),
                  new, new, _const_spec((H_A, t, p)), _const_spec((H_A, t, t)),
                  vec(), vec(), vec(), vec(), _const_spec((1, W_A))],
        out_specs=new,
        out_shape=jax.ShapeDtypeStruct((nb, t, W_A), BF16),
        compiler_params=pltpu.CompilerParams(dimension_semantics=("arbitrary",)),
        name="attn_small",
    )(lam_init, q3, cache_kt, cache_v4, kn, vn, bp, bn, lq1, lk1, lq2, lk2, gdiff)


_MXU_N = 256
_FF_CHUNK = 6 * _MXU_N
_FF_BOUNDS = tuple((lo, min(lo + _FF_CHUNK, D_FF)) for lo in range(0, D_FF, _FF_CHUNK))
assert all((hi - lo) % _MXU_N == 0 for lo, hi in _FF_BOUNDS)


def _ffn_kernel(x_ref, oa_ref, ob_ref, oc_ref, wout_ref, gffn_ref, wup_ref, cw_ref, cb_ref, wdn_ref,
                past_ref, gfin_ref, y_ref, new_ref, halo_scr, *, bb, tt, final_norm):
    rows = bb * tt
    taps = FFN_CONV_W - 1

    @pl.when(pl.program_id(1) == 0)
    def _():
        halo_scr[...] = past_ref[...]

    x1 = (x_ref[...].reshape(rows, D_MODEL)
          + _dot(oa_ref[...].reshape(rows, W_A), wout_ref[0:W_A, :])
          + _dot(ob_ref[...].reshape(rows, C_CONV), wout_ref[W_A:W_A + C_CONV, :])
          + _dot(oc_ref[...].reshape(rows, W_C), wout_ref[W_A + C_CONV:D_MODEL, :]))
    h2 = _rms(x1, gffn_ref[...]).astype(BF16)

    acc = jnp.zeros((rows, D_MODEL), F32)
    for lo, hi in _FF_BOUNDS:
        gate = _dot(h2, wup_ref[:, lo:hi])
        val = _dot(h2, wup_ref[:, D_FF + lo:D_FF + hi])
        gate3 = gate.reshape(bb, tt, hi - lo)
        tpos = lax.broadcasted_iota(I32, gate3.shape, 1)
        conv = cw_ref[taps:taps + 1, lo:hi] * gate + cb_ref[:, lo:hi]
        for d in range(1, taps + 1):
            shifted = pltpu.roll(gate, d, axis=0).reshape(gate3.shape)
            for e in range(d):
                shifted = jnp.where(tpos == e, halo_scr[:, taps - d + e:taps - d + e + 1, lo:hi], shifted)
            conv = conv + cw_ref[taps - d:taps - d + 1, lo:hi] * shifted.reshape(rows, hi - lo)
        halo_scr[:, :, lo:hi] = gate3[:, tt - taps:tt, :]
        gelu = 0.5 * conv * (1.0 + jnp.tanh(math.sqrt(2.0 / math.pi) * (conv + 0.044715 * (conv * conv * conv))))
        acc = acc + _dot((gelu * val).astype(BF16), wdn_ref[lo:hi, :])

    y = x1 + acc
    if final_norm:
        y = _rms(y, gfin_ref[...])
    y_ref[...] = y.reshape(bb, tt, D_MODEL)
    new_ref[...] = halo_scr[...]


def _ffn(x3, oa3, ob3, oc3, wout, gffn, wup, cw, cb, wdn, past, gfin, bb, tt, shared_past, final_norm):
    nb, t, _ = x3.shape
    taps = FFN_CONV_W - 1
    assert tt >= taps
    tok = lambda w: pl.BlockSpec((bb, tt, w), lambda bi, ti: (bi, ti, 0))
    pidx = (lambda bi, ti: (0, 0, 0)) if shared_past else (lambda bi, ti: (bi, 0, 0))
    assert not (shared_past and bb != 1)
    return pl.pallas_call(
        functools.partial(_ffn_kernel, bb=bb, tt=tt, final_norm=final_norm),
        grid=(nb // bb, t // tt),
        in_specs=[tok(D_MODEL), tok(W_A), tok(C_CONV), tok(W_C),
                  _const_spec((D_MODEL, D_MODEL)), _const_spec((1, D_MODEL)),
                  _const_spec((D_MODEL, 2 * D_FF)), _const_spec((FFN_CONV_W, D_FF)), _const_spec((1, D_FF)),
                  _const_spec((D_FF, D_MODEL)),
                  pl.BlockSpec((bb, taps, D_FF), pidx), _const_spec((1, D_MODEL))],
        out_specs=[tok(D_MODEL), pl.BlockSpec((bb, taps, D_FF), lambda bi, ti: (bi, 0, 0))],
        out_shape=[jax.ShapeDtypeStruct((nb, t, D_MODEL), F32),
                   jax.ShapeDtypeStruct((nb, taps, D_FF), F32)],
        scratch_shapes=[pltpu.VMEM((bb, taps, D_FF), F32)],
        compiler_params=pltpu.CompilerParams(dimension_semantics=("arbitrary", "arbitrary"),
                                             vmem_limit_bytes=VMEM_LIMIT),
        name="outproj_ffn",
    )(x3, oa3, ob3, oc3, wout, gffn, wup, cw, cb, wdn, past, gfin)


def _seg_matrix(width, group, value, dtype):
    g = jnp.arange(width, dtype=I32) // group
    return jnp.where(g[:, None] == g[None, :], value, 0.0).astype(dtype)


def _state_to_kernel(s):
    n = s.shape[0]
    eye = jnp.eye(H_C, dtype=s.dtype)
    full = jnp.einsum('nhdv,hg->nhvgd', s, eye)
    return full.reshape(n, W_C, H_C * DK_C)


def _state_from_kernel(sf):
    n = sf.shape[0]
    s5 = sf.reshape(n, H_C, DV_C, H_C, DK_C)
    diag = jnp.stack([s5[:, h, :, h, :] for h in range(H_C)], axis=1)
    return jnp.swapaxes(diag, 2, 3)


def _pick_tile(total, want):
    t = min(total, want)
    assert total % t == 0, (total, want)
    return t


def kernel(x_prompt, x_sample, cache_k, cache_v, state_conv, state_hgrn, state_ffn, meta_tokens, rel_bias, g_mix, w_in, g_q, g_k, lam_q1, lam_k1, lam_q2, lam_k2, g_diff, conv_w, conv_b, ln_g, ln_b, lb_logits, g_hgrn, w_out, g_ffn, w_up, ffn_conv_w, ffn_conv_b, w_down, g_final):
    depth = g_mix.shape[0]
    bp, seq, _ = x_prompt.shape
    bs, ts, _ = x_sample.shape
    past_len = cache_k.shape[2]
    assert ts == N_META == REC_BLOCK and meta_tokens.shape[0] == N_META
    assert seq % CHUNK == 0 and past_len % CHUNK == 0

    lb_all = jnp.cumsum(jax.nn.softmax(lb_logits.astype(F32), axis=0), axis=0)
    lb_all = lb_all - lb_all[0:1]
    lb_consts = jnp.stack([jnp.log(lb_all), jnp.log1p(-lb_all), 1.0 - lb_all]
                          + [jnp.zeros_like(lb_all)] * (SUBLANES - 3), axis=1)

    seg_qk = _seg_matrix(W_A, DQK, 1.0 / DQK, BF16)
    seg_mean = _seg_matrix(W_C, DV_C, 1.0 / DV_C, BF16)
    seg_sum = _seg_matrix(W_C, DK_C, 1.0, BF16)
    blockdiag = _seg_matrix(W_C, DK_C, 1.0, F32)

    tq = _pick_tile(seq, 512)
    tk = _pick_tile(tq, 256)
    bt_main = _attn_bias_tiles(rel_bias, tq, tk)
    bt_meta = _bias_tile(rel_bias, N_META, tq, N_META, 0, True, keys_on_rows=True)
    far_bias = rel_bias[_NB - 1, :] * LOG2E
    bias_past = _bias_tile(rel_bias, ts, past_len, past_len, 0, False)
    bias_new = _bias_tile(rel_bias, ts, ts, past_len, past_len, False)

    cache_kt = jnp.swapaxes(cache_k.reshape(depth, bs, past_len, W_A), 2, 3)
    cache_v4 = cache_v.reshape(depth, bs, past_len * H_A, DVA)

    xs = jnp.concatenate([x_sample, meta_tokens.astype(x_sample.dtype)[None]], axis=0)
    xm = x_prompt
    nbs = bs + 1
    zrow = lambda a: jnp.zeros((1,) + a.shape[1:], a.dtype)

    tm_main = _pick_tile(bp * seq, 512)
    small_rows = max(r for r in range(1, 17) if nbs % r == 0)
    outs = {k: [] for k in ("kp", "vp", "cp", "hp", "fp", "ks", "vs", "cs", "hs", "fs")}
    for l in range(depth):
        last = l == depth - 1
        lam_init = jnp.full((1,), 0.8 - 0.6 * math.exp(-0.3 * l), F32)
        win = w_in[l].astype(BF16)
        wout = w_out[l].astype(BF16)
        wup = w_up[l].astype(BF16)
        wdn = w_down[l].astype(BF16)
        gq = jnp.tile(g_q[l], 2 * H_A)[None]
        gk = jnp.tile(g_k[l], 2 * H_A)[None]
        row = lambda a: a[l][None]
        common_attn = (lam_init, row(lam_q1), row(lam_k1), row(lam_q2), row(lam_k2), row(g_diff))

        def mixer_front(x3, tm, feature_major):
            nb, t, _ = x3.shape
            q, kb, vf4, glu, hraw, k_extra, v_extra = _inproj(
                x3.reshape(nb * t, D_MODEL), row(g_mix), win, gq, gk, seg_qk, tm, t, feature_major)
            r3 = lambda a: a.reshape(nb, t, a.shape[-1])
            return (r3(q), r3(kb), vf4.reshape(nb, t, H_A, DVA), r3(glu), r3(hraw),
                    k_extra if feature_major else r3(k_extra), v_extra if feature_major else r3(v_extra))

        q, kb, vf_s, glu, hraw, kf_s, vb = mixer_front(xs, nbs * ts, False)
        oa = _attn_small(q, cache_kt, cache_v4, l, kb, vb, bias_past, bias_new, *common_attn)
        ob, conv_s = _conv_module(glu, jnp.concatenate([state_conv[l], zrow(state_conv[l])]),
                                  conv_w[l], row(conv_b), row(ln_g), row(ln_b), ts, False)
        oc, hg_s = _hgrn(hraw, lb_consts[l], row(g_hgrn), seg_mean, seg_sum, blockdiag,
                         _state_to_kernel(jnp.concatenate([state_hgrn[l], zrow(state_hgrn[l])]).astype(F32)),
                         small_rows, ts, False)
        xs, ffn_s = _ffn(xs, oa, ob, oc, wout, row(g_ffn), wup, ffn_conv_w[l], row(ffn_conv_b), wdn,
                         jnp.concatenate([state_ffn[l], zrow(state_ffn[l])]), g_final[None],
                         nbs, ts, False, last)
        meta_k, meta_vt = kb[bs:], vb[bs].T
        gdt = jnp.broadcast_to(g_diff[l][:, None], (W_A, LANES))

        q, kb, vf_m, glu, hraw, kft_m, vt = mixer_front(xm, tm_main, True)
        oa = _attn_main(q, kb, vt, meta_k, meta_vt, bt_main, bt_meta, far_bias, *common_attn[:-1], gdt, tq, tk)
        ob, conv_m = _conv_module(glu, conv_s[bs:], conv_w[l], row(conv_b), row(ln_g), row(ln_b),
                                  _pick_tile(seq, 512), True)
        oc, hg_m = _hgrn(hraw, lb_consts[l], row(g_hgrn), seg_mean, seg_sum, blockdiag, hg_s[bs:],
                         1, _pick_tile(seq, 256), True)
        xm, ffn_m = _ffn(xm, oa, ob, oc, wout, row(g_ffn), wup, ffn_conv_w[l], row(ffn_conv_b), wdn,
                         ffn_s[bs:], g_final[None], 1, _pick_tile(seq, 512), True, last)

        meta_rows = lambda a: jnp.broadcast_to(a[bs:], (bp,) + a.shape[1:])
        kt_p = jnp.concatenate([meta_rows(jnp.swapaxes(kf_s, 1, 2)), kft_m], axis=2)
        outs["kp"].append(jnp.swapaxes(kt_p, 1, 2).reshape(bp, N_META + seq, H_A, 2, DQK))
        outs["vp"].append(jnp.concatenate([meta_rows(vf_s), vf_m], axis=1))
        outs["cp"].append(conv_m)
        outs["hp"].append(_state_from_kernel(hg_m))
        outs["fp"].append(ffn_m)
        outs["ks"].append(kf_s[:bs].reshape(bs, ts, H_A, 2, DQK))
        outs["vs"].append(vf_s[:bs])
        outs["cs"].append(conv_s[:bs])
        outs["hs"].append(_state_from_kernel(hg_s[:bs]))
        outs["fs"].append(ffn_s[:bs])

    st = {k: jnp.stack(v) for k, v in outs.items()}
    return (xm, xs[:bs], st["kp"], st["vp"], st["cp"], st["hp"], st["fp"],
            st["ks"], st["vs"], st["cs"], st["hs"], st["fs"])
```

```python
import functools
import math

import jax
import jax.numpy as jnp
from jax import lax
from jax.experimental import pallas as pl
from jax.experimental.pallas import tpu as pltpu

F32 = jnp.float32
BF16 = jnp.bfloat16
I32 = jnp.int32

D_MODEL = 1024
N_META = 16
CHUNK = 64
DQK = 64
DVA = 2 * DQK
H_A = 4
W_A = H_A * DVA
C_CONV = 256
CONV_W = 31
H_C = 4
DK_C = 64
DV_C = 64
W_C = H_C * DV_C
D_FF = 2816
FFN_CONV_W = 3
N_BUCKETS = 32
MAX_DIST = 128
REC_BLOCK = 16
EPS = 1e-6
NEG = -1e30
LOG2E = math.log2(math.e)
IN_COLS = 2 * W_A + W_A + 2 * C_CONV + 4 * W_C

LANES = 128
SUBLANES = 8
VMEM_LIMIT = 56 * 1024 * 1024

_NB = N_BUCKETS // 2
_MAX_EXACT = _NB // 2
_BUCKET_THR = tuple(
    next(n for n in range(_MAX_EXACT, MAX_DIST + 1)
         if n ** (_NB - _MAX_EXACT) * _MAX_EXACT ** k >= _MAX_EXACT ** (_NB - _MAX_EXACT) * MAX_DIST ** k)
    for k in range(1, _NB - _MAX_EXACT))


def _rms(x, g):
    return x * lax.rsqrt(jnp.mean(x * x, axis=-1, keepdims=True) + EPS) * g


def _dot(a, b):
    return jnp.dot(a, b, preferred_element_type=F32)


def _dot_nt(a, b):
    return lax.dot_general(a, b, (((1,), (1,)), ((), ())), preferred_element_type=F32)


def _dot_tn(a, b):
    return lax.dot_general(a, b, (((0,), (0,)), ((), ())), preferred_element_type=F32)


def _sigmoid(x):
    return jax.nn.sigmoid(x)


def _const_spec(shape):
    nd = len(shape)
    return pl.BlockSpec(shape, lambda *_: (0,) * nd, pipeline_mode=pl.Buffered(1))


def _bias_kernel(tab_ref, o_ref, *, qpos0, kpos0, prompt_chunks, keys_on_rows):
    h = pl.program_id(0)
    shape = o_ref.shape[1:]
    r = lax.broadcasted_iota(I32, shape, 1 if keys_on_rows else 0) + qpos0
    c = lax.broadcasted_iota(I32, shape, 0 if keys_on_rows else 1) + kpos0
    rel = c - r
    n = jnp.abs(rel)
    large = jnp.full(shape, _MAX_EXACT, I32)
    for thr in _BUCKET_THR:
        large = large + (n >= thr).astype(I32)
    bucket = jnp.where(rel > 0, _NB, 0) + jnp.where(n < _MAX_EXACT, n, large)
    out = jnp.zeros(shape, F32)
    for b in range(N_BUCKETS):
        out = jnp.where(bucket == b, tab_ref[b, h] * LOG2E, out)
    shift = CHUNK - N_META if prompt_chunks else 0
    log2_chunk = CHUNK.bit_length() - 1
    visible = ((c + shift) >> log2_chunk) <= ((r + shift) >> log2_chunk)
    o_ref[0] = jnp.where(visible, out, NEG)


def _bias_tile(rel_bias, rows, cols, qpos0, kpos0, prompt_chunks, keys_on_rows=False):
    return pl.pallas_call(
        functools.partial(_bias_kernel, qpos0=qpos0, kpos0=kpos0, prompt_chunks=prompt_chunks,
                          keys_on_rows=keys_on_rows),
        grid=(H_A,),
        in_specs=[pl.BlockSpec(memory_space=pltpu.SMEM)],
        out_specs=pl.BlockSpec((1, rows, cols), lambda h: (h, 0, 0)),
        out_shape=jax.ShapeDtypeStruct((H_A, rows, cols), F32),
        name="relpos_bias",
    )(rel_bias)


def _inproj_kernel(x_ref, gmix_ref, win_ref, gq_ref, gk_ref, seg_ref,
                   q_ref, kb_ref, vf4_ref, glu_ref, hraw_ref, k_extra_ref, v_extra_ref, *, feature_major):
    tm = x_ref.shape[0]
    h = _rms(x_ref[...], gmix_ref[...]).astype(BF16)

    def proj(lo, hi):
        return _dot(h, win_ref[:, lo:hi])

    def group_norm(a, g):
        ms = _dot((a * a).astype(BF16), seg_ref[...])
        return a * lax.rsqrt(ms + EPS) * g

    qa = group_norm(proj(0, W_A), gq_ref[...])
    q_ref[...] = (qa * (DQK ** -0.5 * LOG2E)).astype(BF16)
    ka = group_norm(proj(W_A, 2 * W_A), gk_ref[...])
    kb_ref[...] = ka.astype(BF16)
    va = proj(2 * W_A, 3 * W_A)
    for hd in range(H_A):
        vf4_ref[pl.ds(hd, tm, stride=H_A), :] = va[:, hd * DVA:(hd + 1) * DVA]
    if feature_major:
        k_extra_ref[0] = ka.T
        v_extra_ref[...] = va.T.astype(BF16)
    else:
        k_extra_ref[...] = ka
        v_extra_ref[...] = va.astype(BF16)
    glu_ref[...] = proj(3 * W_A, 3 * W_A + 2 * C_CONV)
    hraw_ref[...] = proj(3 * W_A + 2 * C_CONV, IN_COLS)


def _inproj(x2d, gmix, win, gq, gk, seg, tm, rows_per_seq, feature_major):
    n = x2d.shape[0]
    row = lambda w: pl.BlockSpec((tm, w), lambda i: (i, 0))
    outs = [(W_A, BF16), (W_A, BF16), None, (2 * C_CONV, F32), (4 * W_C, F32)]
    out_specs = [row(o[0]) if o else pl.BlockSpec((tm * H_A, DVA), lambda i: (i, 0)) for o in outs]
    out_shape = [jax.ShapeDtypeStruct((n, o[0]), o[1]) if o else jax.ShapeDtypeStruct((n * H_A, DVA), F32)
                 for o in outs]
    if feature_major:
        nt = rows_per_seq // tm
        out_specs += [pl.BlockSpec((1, W_A, tm), lambda i: (i // nt, 0, i % nt)),
                      pl.BlockSpec((W_A, tm), lambda i: (0, i))]
        out_shape += [jax.ShapeDtypeStruct((n // rows_per_seq, W_A, rows_per_seq), F32),
                      jax.ShapeDtypeStruct((W_A, n), BF16)]
    else:
        out_specs += [row(W_A), row(W_A)]
        out_shape += [jax.ShapeDtypeStruct((n, W_A), F32), jax.ShapeDtypeStruct((n, W_A), BF16)]
    return pl.pallas_call(
        functools.partial(_inproj_kernel, feature_major=feature_major),
        grid=(n // tm,),
        in_specs=[row(D_MODEL), _const_spec((1, D_MODEL)), _const_spec((D_MODEL, IN_COLS)),
                  _const_spec((1, W_A)), _const_spec((1, W_A)), _const_spec((W_A, W_A))],
        out_specs=out_specs,
        out_shape=out_shape,
        compiler_params=pltpu.CompilerParams(dimension_semantics=("arbitrary",), vmem_limit_bytes=VMEM_LIMIT),
        name="inproj",
    )(x2d, gmix, win, gq, gk, seg)


_CONV_PAD = 32


def _conv_kernel(glu_ref, past_ref, w_ref, b_ref, lng_ref, lnb_ref, ob_ref, new_ref, xc_ref, xr_ref, *, tt, rc):
    halo = CONV_W - 1
    h0 = _CONV_PAD - halo

    @pl.when(pl.program_id(1) == 0)
    def _():
        xc_ref[0:h0, :] = jnp.zeros((h0, C_CONV), F32)
        xc_ref[h0:_CONV_PAD, :] = past_ref[0]

    glu = glu_ref[0]
    xc_ref[_CONV_PAD:_CONV_PAD + tt, :] = glu[:, :C_CONV] * _sigmoid(glu[:, C_CONV:])
    span = xr_ref.shape[1]
    for r in range(1, SUBLANES):
        xr_ref[r - 1] = xc_ref[r:r + span, :]
    bias = b_ref[...]
    for c in range(tt // rc):
        acc = jnp.zeros((rc, C_CONV), F32) + bias
        for j in range(CONV_W):
            r = (j + h0) % SUBLANES
            base = c * rc + j + h0 - r
            rows = xc_ref[base:base + rc, :] if r == 0 else xr_ref[r - 1, base:base + rc, :]
            acc = acc + w_ref[j:j + 1, :] * rows
        mu = jnp.mean(acc, axis=-1, keepdims=True)
        xm = acc - mu
        var = jnp.mean(xm * xm, axis=-1, keepdims=True)
        y = xm * lax.rsqrt(var + EPS) * lng_ref[...] + lnb_ref[...]
        ob_ref[0, c * rc:(c + 1) * rc, :] = (y * _sigmoid(y)).astype(BF16)
    new = xc_ref[h0 + tt:_CONV_PAD + tt, :]
    new_ref[0] = new
    xc_ref[h0:_CONV_PAD, :] = new


def _conv_module(glu3, past, w, b, lng, lnb, tt, shared_past):
    nb, t, _ = glu3.shape
    rc = min(tt, 128)
    pidx = (lambda bi, ti: (0, 0, 0)) if shared_past else (lambda bi, ti: (bi, 0, 0))
    return pl.pallas_call(
        functools.partial(_conv_kernel, tt=tt, rc=rc),
        grid=(nb, t // tt),
        in_specs=[pl.BlockSpec((1, tt, 2 * C_CONV), lambda bi, ti: (bi, ti, 0)),
                  pl.BlockSpec((1, CONV_W - 1, C_CONV), pidx),
                  _const_spec((CONV_W, C_CONV)), _const_spec((1, C_CONV)),
                  _const_spec((1, C_CONV)), _const_spec((1, C_CONV))],
        out_specs=[pl.BlockSpec((1, tt, C_CONV), lambda bi, ti: (bi, ti, 0)),
                   pl.BlockSpec((1, CONV_W - 1, C_CONV), lambda bi, ti: (bi, 0, 0))],
        out_shape=[jax.ShapeDtypeStruct((nb, t, C_CONV), BF16),
                   jax.ShapeDtypeStruct((nb, CONV_W - 1, C_CONV), F32)],
        scratch_shapes=[pltpu.VMEM((_CONV_PAD + tt, C_CONV), F32),
                        pltpu.VMEM((SUBLANES - 1, _CONV_PAD + tt - SUBLANES, C_CONV), F32)],
        compiler_params=pltpu.CompilerParams(dimension_semantics=("arbitrary", "arbitrary")),
        name="conv_module",
    )(glu3, past, w, b, lng, lnb)


def _hgrn_kernel(hraw_ref, lbc_ref, ghg_ref, segm_ref, segs_ref, bd_ref, rsum_ref, s0_ref,
                 oc_ref, sout_ref, s_scr, *, tc, chained):
    nblk = tc // REC_BLOCK
    blk3 = (nblk, REC_BLOCK, W_C)

    if chained:
        @pl.when(pl.program_id(1) == 0)
        def _():
            s_scr[...] = s0_ref[0]

    hr = hraw_ref[...].reshape(tc, 4 * W_C)
    z = hr[:, 0:W_C]
    v = hr[:, W_C:2 * W_C]
    q = hr[:, 2 * W_C:3 * W_C]
    gc = hr[:, 3 * W_C:4 * W_C]
    log_lb = lbc_ref[0:1, :]
    log_1m_lb = lbc_ref[1:2, :]
    one_m_lb = lbc_ref[2:3, :]

    log_sig = jnp.minimum(z, 0.0) - jnp.log(1.0 + jnp.exp(-jnp.abs(z)))
    b_arg = log_1m_lb + log_sig
    log_f = jnp.maximum(log_lb, b_arg) + jnp.log(1.0 + jnp.exp(-jnp.abs(log_lb - b_arg)))
    kc = one_m_lb * _sigmoid(-z)

    pos = lax.broadcasted_iota(I32, (tc, W_C), 0) & (REC_BLOCK - 1)
    bcum = log_f
    sh = 1
    while sh < REC_BLOCK:
        bcum = bcum + jnp.where(pos >= sh, pltpu.roll(bcum, sh, axis=0), 0.0)
        sh *= 2

    b3 = bcum.reshape(blk3)
    b_last = b3[:, REC_BLOCK - 1:REC_BLOCK, :]
    qe = (q * jnp.exp(bcum)).astype(BF16)
    ke = (kc.reshape(blk3) * jnp.exp(b_last - b3)).reshape(tc, W_C).astype(BF16)
    v_bf = v.astype(BF16)
    dl = jnp.exp(b_last)

    half = REC_BLOCK // 2
    slab = (nblk, 2, half, W_C)
    hrows = nblk * half
    b_lo, b_hi = (bcum.reshape(slab)[:, i] for i in range(2))
    k_lo, k_hi = (kc.reshape(slab)[:, i] for i in range(2))
    q_lo, q_hi = (q.reshape(slab)[:, i] for i in range(2))
    v_lo, v_hi = (v.reshape(slab)[:, i] for i in range(2))
    sidx = lax.broadcasted_iota(I32, (nblk, half, W_C), 1)

    def head_sums(xs):
        stacked = jnp.concatenate([x.reshape(hrows, W_C) for x in xs], axis=0).astype(BF16)
        a = _dot(stacked, segs_ref[...])
        return [a[t * hrows:(t + 1) * hrows, :].reshape(nblk, half, W_C) for t in range(half)]

    def slab_sums(xs):
        side = jnp.concatenate([x.reshape(hrows, W_C) for x in xs], axis=1).astype(BF16)
        r = _dot(rsum_ref[...], side)
        return [r[:, t * W_C:(t + 1) * W_C].reshape(nblk, half, W_C) for t in range(half)]

    def place(rows):
        out = jnp.zeros((nblk, half, W_C), F32)
        for t in range(half):
            out = jnp.where(sidx == t, rows[t], out)
        return out

    row_t = lambda a, t: a[:, t:t + 1, :]
    x_lo = [jnp.where(sidx <= t, jnp.exp(jnp.minimum(row_t(b_lo, t) - b_lo, 0.0)), 0.0) * k_lo * row_t(q_lo, t)
            for t in range(half)]
    o_lo = place(slab_sums([a * v_lo for a in head_sums(x_lo)]))
    x_hl = [jnp.exp(jnp.minimum(row_t(b_hi, t) - b_lo, 0.0)) * k_lo * row_t(q_hi, t) for t in range(half)]
    x_hh = [jnp.where(sidx <= t, jnp.exp(jnp.minimum(row_t(b_hi, t) - b_hi, 0.0)), 0.0) * k_hi * row_t(q_hi, t)
            for t in range(half)]
    o_hi = place(slab_sums([al * v_lo + ah * v_hi for al, ah in zip(head_sums(x_hl), head_sums(x_hh))]))
    o_intra = jnp.stack([o_lo, o_hi], axis=1).reshape(tc, W_C)

    blk = lambda a, j: a[j * REC_BLOCK:(j + 1) * REC_BLOCK, :]
    upd = [_dot_tn(blk(v_bf, j), blk(ke, j)) * bd_ref[...] for j in range(nblk)]
    if chained:
        s = s_scr[...]
        s_before = []
        for j in range(nblk):
            s_before.append(s.astype(BF16))
            s = dl[j] * s + upd[j]
        s_scr[...] = s
    else:
        s_before = [s0_ref[j].astype(BF16) for j in range(nblk)]
        for j in range(nblk):
            sout_ref[j] = dl[j] * s0_ref[j] + upd[j]
    o_inter = jnp.concatenate([_dot_nt(blk(qe, j), s_before[j]) for j in range(nblk)], axis=0)

    o = o_intra + o_inter
    ms = _dot((o * o).astype(BF16), segm_ref[...])
    oc = (o * lax.rsqrt(ms + EPS) * ghg_ref[...] * (gc * _sigmoid(gc))).astype(BF16)
    oc_ref[...] = oc.reshape(oc_ref.shape)

    if chained:
        @pl.when(pl.program_id(1) == pl.num_programs(1) - 1)
        def _():
            sout_ref[0] = s_scr[...]


def _hgrn(hraw3, lbc, ghg, segm, segs, bd, s0, rows, tc, shared_past):
    nb, t, _ = hraw3.shape
    chained = rows == 1
    assert chained or (t == tc == REC_BLOCK and not shared_past)
    sidx = (lambda bi, ti: (0, 0, 0)) if shared_past else (lambda bi, ti: (bi, 0, 0))
    hrows = rows * tc // 2
    rsum = _seg_matrix(hrows, REC_BLOCK // 2, 1.0, BF16)
    return pl.pallas_call(
        functools.partial(_hgrn_kernel, tc=rows * tc, chained=chained),
        grid=(nb // rows, t // tc),
        in_specs=[pl.BlockSpec((rows, tc, 4 * W_C), lambda bi, ti: (bi, ti, 0)),
                  _const_spec((SUBLANES, W_C)), _const_spec((1, W_C)),
                  _const_spec((W_C, W_C)), _const_spec((W_C, W_C)), _const_spec((W_C, W_C)),
                  _const_spec((hrows, hrows)),
                  pl.BlockSpec((rows, W_C, W_C), sidx)],
        out_specs=[pl.BlockSpec((rows, tc, W_C), lambda bi, ti: (bi, ti, 0)),
                   pl.BlockSpec((rows, W_C, W_C), lambda bi, ti: (bi, 0, 0))],
        out_shape=[jax.ShapeDtypeStruct((nb, t, W_C), BF16),
                   jax.ShapeDtypeStruct((nb, W_C, W_C), F32)],
        scratch_shapes=[pltpu.VMEM((W_C, W_C), F32)],
        compiler_params=pltpu.CompilerParams(dimension_semantics=("arbitrary", "arbitrary")),
        name="hgrn2",
    )(hraw3, lbc, ghg, segm, segs, bd, rsum, s0)


def _split_halves(q):
    lane = lax.broadcasted_iota(I32, q.shape, 1)
    zero = jnp.zeros_like(q)
    return jnp.concatenate([jnp.where(lane < DQK, q, zero), jnp.where(lane >= DQK, q, zero)], axis=0)


def _twice(b):
    return jnp.concatenate([b, b], axis=0)


def _diff_combine(acc, l, t, lam_init, lq1, lk1, lq2, lk2, gd):
    o = acc / l
    lam = (jnp.exp(jnp.sum(lq1 * lk1, axis=1, keepdims=True))
           - jnp.exp(jnp.sum(lq2 * lk2, axis=1, keepdims=True)) + lam_init)
    a = o[:t] - lam * o[t:]
    return (_rms(a, gd) * (1.0 - lam_init)).astype(BF16)


def _attn_main_kernel(sc_ref, farc_ref, q_ref, k_ref, vt_ref, km_ref, vmt_ref, bt_ref, btm_ref,
                      lq1_ref, lk1_ref, lq2_ref, lk2_ref, gdt_ref, o_ref,
                      acc_scr, s_scr, *, tq, tk):
    h = pl.program_id(1)
    i = pl.program_id(2)
    q = q_ref[0]
    lane = lax.broadcasted_iota(I32, q.shape, 1)
    zero = jnp.zeros_like(q)
    q_half = (jnp.where(lane < DQK, q, zero), jnp.where(lane >= DQK, q, zero))
    n_tiles = (i + 1) * (tq // tk)

    def scores(j, slot):
        k_tile = k_ref[0, pl.ds(pl.multiple_of(j * tk, tk), tk), :]
        for c in range(2):
            s_scr[slot, c] = _dot_nt(k_tile, q_half[c])

    n_var = bt_ref.shape[0]
    far_bias = farc_ref[h]

    ones_rows = jnp.ones((2 * SUBLANES, tk), BF16)

    def softmax_pv(j, slot, stats, extra=None, all_far=False, masked_cols=0):
        bidx = jnp.minimum(n_tiles - 1 - j, n_var - 1)
        v_ones = jnp.concatenate([vt_ref[:, pl.ds(pl.multiple_of(j * tk, tk), tk)], ones_rows], axis=0)
        new_stats = []
        for c in range(2):
            m_prev, l_prev = stats[c]
            strips = []
            for w in range(tq // LANES):
                cols = slice(w * LANES, (w + 1) * LANES)
                visible = (w + 1) * LANES > masked_cols
                m_new = m_prev[:, cols]
                if visible and all_far:
                    s = s_scr[slot, c, :, cols]
                    m_new = jnp.maximum(m_new, jnp.max(s, axis=0, keepdims=True) + far_bias)
                elif visible:
                    s = s_scr[slot, c, :, cols] + bt_ref[bidx, 0, :, cols]
                    m_new = jnp.maximum(m_new, jnp.max(s, axis=0, keepdims=True))
                if extra is not None:
                    s_x = extra[c][:, cols]
                    m_new = jnp.maximum(m_new, jnp.max(s_x, axis=0, keepdims=True))
                shift = m_new - far_bias if all_far else m_new
                alpha = jnp.exp2(m_prev[:, cols] - m_new)
                p = jnp.exp2((s - shift).astype(BF16)) if visible else jnp.zeros((tk, LANES), BF16)
                p_x = l_x = None
                if extra is not None:
                    p_x = jnp.exp2(s_x - m_new)
                    l_x = jnp.sum(p_x, axis=0, keepdims=True)
                    p_x = p_x.astype(BF16)
                strips.append((p, m_new, alpha, p_x, l_x))
            cat = lambda k: jnp.concatenate([st[k] for st in strips], axis=1)
            p_tile, alpha = cat(0), cat(2)
            pv_sum = _dot(v_ones, p_tile)
            pv = pv_sum[0:DVA, :]
            l_new = alpha * l_prev + pv_sum[DVA:DVA + 1, :]
            if extra is not None:
                l_new = l_new + cat(4)
                pv = pv + _dot(vmt_ref[...], cat(3))
            acc_scr[c] = alpha * acc_scr[c] + pv
            new_stats.append((cat(1), l_new))
        return tuple(new_stats)

    def pair(n, stats, all_far):
        scores(2 * n + 1, 1)
        stats = softmax_pv(2 * n, 0, stats, all_far=all_far)
        scores(2 * n + 2, 0)
        return softmax_pv(2 * n + 1, 1, stats, all_far=all_far)

    assert (tq // tk) % 2 == 0
    acc_scr[...] = jnp.zeros(acc_scr.shape, F32)
    stat0 = (jnp.full((1, tq), -jnp.inf, F32), jnp.zeros((1, tq), F32))
    scores(0, 0)
    n_pairs = n_tiles // 2 - 1
    n_far_pairs = jnp.clip((n_tiles + 1 - n_var) // 2, 0, n_pairs)
    stats = lax.fori_loop(0, n_far_pairs, functools.partial(pair, all_far=True), (stat0, stat0))
    stats = lax.fori_loop(n_far_pairs, n_pairs, functools.partial(pair, all_far=False), stats)
    scores(n_tiles - 1, 1)
    stats = softmax_pv(n_tiles - 2, 0, stats)
    bias_meta = jnp.where(i == 0, btm_ref[0], farc_ref[h])
    s_meta = [_dot_nt(km_ref[0], q_half[c]) + bias_meta for c in range(2)]
    stats = softmax_pv(n_tiles - 1, 1, stats, s_meta, masked_cols=tq - tk)
    o = [acc_scr[c] * (1.0 / stats[c][1]) for c in range(2)]

    lam_init = sc_ref[0]
    lam = (jnp.exp(jnp.sum(lq1_ref[...] * lk1_ref[...], axis=1, keepdims=True))
           - jnp.exp(jnp.sum(lq2_ref[...] * lk2_ref[...], axis=1, keepdims=True)) + lam_init)
    a = o[0] - lam * o[1]
    gain = jnp.concatenate([gdt_ref[...]] * (tq // LANES), axis=1)
    y = a * lax.rsqrt(jnp.mean(a * a, axis=0, keepdims=True) + EPS) * gain * (1.0 - lam_init)
    o_ref[0] = y.T.astype(BF16)


def _attn_bias_tiles(rel_bias, tq, tk):
    n_var = -(-(tq + MAX_DIST - 1) // tk) + 1
    base = N_META + n_var * tk
    return jnp.stack([_bias_tile(rel_bias, tk, tq, base, base + tq - (d + 1) * tk, True, keys_on_rows=True)
                      for d in range(n_var)])


def _attn_main(q3, k3, vt, km, vmt, bt, btm, farc, lam_init, lq1, lk1, lq2, lk2, gdt, tq, tk):
    nb, t, _ = q3.shape
    p = km.shape[1]
    assert tq % tk == 0 and tk % CHUNK == 0 and bt.shape[2:] == (tk, tq)
    smem = pl.BlockSpec(memory_space=pltpu.SMEM)
    vec = lambda: _const_spec((1, DQK))
    return pl.pallas_call(
        functools.partial(_attn_main_kernel, tq=tq, tk=tk),
        grid=(nb, H_A, t // tq),
        in_specs=[smem, smem,
                  pl.BlockSpec((1, tq, DVA), lambda b, h, i: (b, i, h)),
                  pl.BlockSpec((1, t, DVA), lambda b, h, i: (b, 0, h)),
                  pl.BlockSpec((DVA, t), lambda b, h, i: (h, b)),
                  pl.BlockSpec((1, p, DVA), lambda b, h, i: (0, 0, h)),
                  pl.BlockSpec((DVA, p), lambda b, h, i: (h, 0)),
                  pl.BlockSpec((bt.shape[0], 1, tk, tq), lambda b, h, i: (0, h, 0, 0)),
                  pl.BlockSpec((1, p, tq), lambda b, h, i: (h, 0, 0)),
                  vec(), vec(), vec(), vec(),
                  pl.BlockSpec((DVA, LANES), lambda b, h, i: (h, 0))],
        out_specs=pl.BlockSpec((1, tq, DVA), lambda b, h, i: (b, i, h)),
        out_shape=jax.ShapeDtypeStruct((nb, t, W_A), BF16),
        scratch_shapes=[pltpu.VMEM((2, DVA, tq), F32), pltpu.VMEM((2, 2, tk, tq), F32)],
        compiler_params=pltpu.CompilerParams(dimension_semantics=("arbitrary",) * 3),
        name="attn_main",
    )(lam_init, farc, q3, k3, vt, km, vmt, bt, btm, lq1, lk1, lq2, lk2, gdt)


def _attn_small_kernel(sc_ref, q_ref, kc_ref, vc_ref, kn_ref, vn_ref, bp_ref, bn_ref,
                       lq1_ref, lk1_ref, lq2_ref, lk2_ref, gd_ref, o_ref, *, t, n_cached):
    b = pl.program_id(0)
    p = vc_ref.shape[2] // H_A
    past_bias = jnp.where(b < n_cached, 0.0, NEG)
    q_all = q_ref[0]
    kn_all = kn_ref[0]
    vn_all = vn_ref[0]
    outs = []
    for hd in range(H_A):
        cols = slice(hd * DVA, (hd + 1) * DVA)
        qq = _split_halves(q_all[:, cols])
        s_p = _dot(qq, kc_ref[0, 0, cols, :].astype(BF16)) + _twice(bp_ref[hd]) + past_bias
        s_n = _dot_nt(qq, kn_all[:, cols]) + _twice(bn_ref[hd])
        m = jnp.maximum(jnp.max(s_p, axis=1, keepdims=True), jnp.max(s_n, axis=1, keepdims=True))
        p_p = jnp.exp2(s_p - m)
        p_n = jnp.exp2(s_n - m)
        l = jnp.sum(p_p, axis=1, keepdims=True) + jnp.sum(p_n, axis=1, keepdims=True)
        v_past = vc_ref[0, 0, pl.ds(hd, p, stride=H_A), :].astype(BF16)
        acc = _dot(p_p.astype(BF16), v_past) + _dot(p_n.astype(BF16), vn_all[:, cols])
        outs.append(_diff_combine(acc, l, t, sc_ref[0], lq1_ref[...], lk1_ref[...],
                                  lq2_ref[...], lk2_ref[...], gd_ref[:, cols]))
    o_ref[0] = jnp.concatenate(outs, axis=1)


def _attn_small(q3, cache_kt, cache_v4, layer, kn, vn, bp, bn, lam_init, lq1, lk1, lq2, lk2, gdiff):
    nb, t, _ = q3.shape
    n_cached, p = cache_kt.shape[1], cache_kt.shape[3]
    smem = pl.BlockSpec(memory_space=pltpu.SMEM)
    vec = lambda: _const_spec((1, DQK))
    cached = lambda b: (layer, jnp.minimum(b, n_cached - 1), 0, 0)
    new = pl.BlockSpec((1, t, W_A), lambda b: (b, 0, 0))
    return pl.pallas_call(
        functools.partial(_attn_small_kernel, t=t, n_cached=n_cached),
        grid=(nb,),
        in_specs=[smem, new,
                  pl.BlockSpec((1, 1, W_A, p), cached), pl.BlockSpec((1, 1, p * H_A, DVA), cached),
                  new, new, _const_spec((H_A, t, p)), _const_spec((H_A, t, t)),
                  vec(), vec(), vec(), vec(), _const_spec((1, W_A))],
        out_specs=new,
        out_shape=jax.ShapeDtypeStruct((nb, t, W_A), BF16),
        compiler_params=pltpu.CompilerParams(dimension_semantics=("arbitrary",)),
        name="attn_small",
    )(lam_init, q3, cache_kt, cache_v4, kn, vn, bp, bn, lq1, lk1, lq2, lk2, gdiff)


_MXU_N = 256
_FF_CHUNK = 6 * _MXU_N
_FF_BOUNDS = tuple((lo, min(lo + _FF_CHUNK, D_FF)) for lo in range(0, D_FF, _FF_CHUNK))
assert all((hi - lo) % _MXU_N == 0 for lo, hi in _FF_BOUNDS)


def _ffn_kernel(x_ref, oa_ref, ob_ref, oc_ref, wout_ref, gffn_ref, wup_ref, cw_ref, cb_ref, wdn_ref,
                past_ref, gfin_ref, y_ref, new_ref, halo_scr, *, bb, tt, final_norm):
    rows = bb * tt
    taps = FFN_CONV_W - 1

    @pl.when(pl.program_id(1) == 0)
    def _():
        halo_scr[...] = past_ref[...]

    x1 = (x_ref[...].reshape(rows, D_MODEL)
          + _dot(oa_ref[...].reshape(rows, W_A), wout_ref[0:W_A, :])
          + _dot(ob_ref[...].reshape(rows, C_CONV), wout_ref[W_A:W_A + C_CONV, :])
          + _dot(oc_ref[...].reshape(rows, W_C), wout_ref[W_A + C_CONV:D_MODEL, :]))
    h2 = _rms(x1, gffn_ref[...]).astype(BF16)

    acc = jnp.zeros((rows, D_MODEL), F32)
    for lo, hi in _FF_BOUNDS:
        gate = _dot(h2, wup_ref[:, lo:hi])
        val = _dot(h2, wup_ref[:, D_FF + lo:D_FF + hi])
        gate3 = gate.reshape(bb, tt, hi - lo)
        tpos = lax.broadcasted_iota(I32, gate3.shape, 1)
        conv = cw_ref[taps:taps + 1, lo:hi] * gate + cb_ref[:, lo:hi]
        for d in range(1, taps + 1):
            shifted = pltpu.roll(gate, d, axis=0).reshape(gate3.shape)
            for e in range(d):
                shifted = jnp.where(tpos == e, halo_scr[:, taps - d + e:taps - d + e + 1, lo:hi], shifted)
            conv = conv + cw_ref[taps - d:taps - d + 1, lo:hi] * shifted.reshape(rows, hi - lo)
        halo_scr[:, :, lo:hi] = gate3[:, tt - taps:tt, :]
        gelu = 0.5 * conv * (1.0 + jnp.tanh(math.sqrt(2.0 / math.pi) * (conv + 0.044715 * (conv * conv * conv))))
        acc = acc + _dot((gelu * val).astype(BF16), wdn_ref[lo:hi, :])

    y = x1 + acc
    if final_norm:
        y = _rms(y, gfin_ref[...])
    y_ref[...] = y.reshape(bb, tt, D_MODEL)
    new_ref[...] = halo_scr[...]


def _ffn(x3, oa3, ob3, oc3, wout, gffn, wup, cw, cb, wdn, past, gfin, bb, tt, shared_past, final_norm):
    nb, t, _ = x3.shape
    taps = FFN_CONV_W - 1
    assert tt >= taps
    tok = lambda w: pl.BlockSpec((bb, tt, w), lambda bi, ti: (bi, ti, 0))
    pidx = (lambda bi, ti: (0, 0, 0)) if shared_past else (lambda bi, ti: (bi, 0, 0))
    assert not (shared_past and bb != 1)
    return pl.pallas_call(
        functools.partial(_ffn_kernel, bb=bb, tt=tt, final_norm=final_norm),
        grid=(nb // bb, t // tt),
        in_specs=[tok(D_MODEL), tok(W_A), tok(C_CONV), tok(W_C),
                  _const_spec((D_MODEL, D_MODEL)), _const_spec((1, D_MODEL)),
                  _const_spec((D_MODEL, 2 * D_FF)), _const_spec((FFN_CONV_W, D_FF)), _const_spec((1, D_FF)),
                  _const_spec((D_FF, D_MODEL)),
                  pl.BlockSpec((bb, taps, D_FF), pidx), _const_spec((1, D_MODEL))],
        out_specs=[tok(D_MODEL), pl.BlockSpec((bb, taps, D_FF), lambda bi, ti: (bi, 0, 0))],
        out_shape=[jax.ShapeDtypeStruct((nb, t, D_MODEL), F32),
                   jax.ShapeDtypeStruct((nb, taps, D_FF), F32)],
        scratch_shapes=[pltpu.VMEM((bb, taps, D_FF), F32)],
        compiler_params=pltpu.CompilerParams(dimension_semantics=("arbitrary", "arbitrary"),
                                             vmem_limit_bytes=VMEM_LIMIT),
        name="outproj_ffn",
    )(x3, oa3, ob3, oc3, wout, gffn, wup, cw, cb, wdn, past, gfin)


def _seg_matrix(width, group, value, dtype):
    g = jnp.arange(width, dtype=I32) // group
    return jnp.where(g[:, None] == g[None, :], value, 0.0).astype(dtype)


def _state_to_kernel(s):
    n = s.shape[0]
    eye = jnp.eye(H_C, dtype=s.dtype)
    full = jnp.einsum('nhdv,hg->nhvgd', s, eye)
    return full.reshape(n, W_C, H_C * DK_C)


def _state_from_kernel(sf):
    n = sf.shape[0]
    s5 = sf.reshape(n, H_C, DV_C, H_C, DK_C)
    diag = jnp.stack([s5[:, h, :, h, :] for h in range(H_C)], axis=1)
    return jnp.swapaxes(diag, 2, 3)


def _pick_tile(total, want):
    t = min(total, want)
    assert total % t == 0, (total, want)
    return t


def kernel(x_prompt, x_sample, cache_k, cache_v, state_conv, state_hgrn, state_ffn, meta_tokens, rel_bias, g_mix, w_in, g_q, g_k, lam_q1, lam_k1, lam_q2, lam_k2, g_diff, conv_w, conv_b, ln_g, ln_b, lb_logits, g_hgrn, w_out, g_ffn, w_up, ffn_conv_w, ffn_conv_b, w_down, g_final):
    depth = g_mix.shape[0]
    bp, seq, _ = x_prompt.shape
    bs, ts, _ = x_sample.shape
    past_len = cache_k.shape[2]
    assert ts == N_META == REC_BLOCK and meta_tokens.shape[0] == N_META
    assert seq % CHUNK == 0 and past_len % CHUNK == 0

    lb_all = jnp.cumsum(jax.nn.softmax(lb_logits.astype(F32), axis=0), axis=0)
    lb_all = lb_all - lb_all[0:1]
    lb_consts = jnp.stack([jnp.log(lb_all), jnp.log1p(-lb_all), 1.0 - lb_all]
                          + [jnp.zeros_like(lb_all)] * (SUBLANES - 3), axis=1)

    seg_qk = _seg_matrix(W_A, DQK, 1.0 / DQK, BF16)
    seg_mean = _seg_matrix(W_C, DV_C, 1.0 / DV_C, BF16)
    seg_sum = _seg_matrix(W_C, DK_C, 1.0, BF16)
    blockdiag = _seg_matrix(W_C, DK_C, 1.0, F32)

    tq = _pick_tile(seq, 512)
    tk = _pick_tile(tq, 256)
    bt_main = _attn_bias_tiles(rel_bias, tq, tk)
    bt_meta = _bias_tile(rel_bias, N_META, tq, N_META, 0, True, keys_on_rows=True)
    far_bias = rel_bias[_NB - 1, :] * LOG2E
    bias_past = _bias_tile(rel_bias, ts, past_len, past_len, 0, False)
    bias_new = _bias_tile(rel_bias, ts, ts, past_len, past_len, False)

    cache_kt = jnp.swapaxes(cache_k.reshape(depth, bs, past_len, W_A), 2, 3)
    cache_v4 = cache_v.reshape(depth, bs, past_len * H_A, DVA)

    xs = jnp.concatenate([x_sample, meta_tokens.astype(x_sample.dtype)[None]], axis=0)
    xm = x_prompt
    nbs = bs + 1
    zrow = lambda a: jnp.zeros((1,) + a.shape[1:], a.dtype)

    tm_main = _pick_tile(bp * seq, 512)
    small_rows = max(r for r in range(1, 17) if nbs % r == 0)
    outs = {k: [] for k in ("kp", "vp", "cp", "hp", "fp", "ks", "vs", "cs", "hs", "fs")}
    for l in range(depth):
        last = l == depth - 1
        lam_init = jnp.full((1,), 0.8 - 0.6 * math.exp(-0.3 * l), F32)
        win = w_in[l].astype(BF16)
        wout = w_out[l].astype(BF16)
        wup = w_up[l].astype(BF16)
        wdn = w_down[l].astype(BF16)
        gq = jnp.tile(g_q[l], 2 * H_A)[None]
        gk = jnp.tile(g_k[l], 2 * H_A)[None]
        row = lambda a: a[l][None]
        common_attn = (lam_init, row(lam_q1), row(lam_k1), row(lam_q2), row(lam_k2), row(g_diff))

        def mixer_front(x3, tm, feature_major):
            nb, t, _ = x3.shape
            q, kb, vf4, glu, hraw, k_extra, v_extra = _inproj(
                x3.reshape(nb * t, D_MODEL), row(g_mix), win, gq, gk, seg_qk, tm, t, feature_major)
            r3 = lambda a: a.reshape(nb, t, a.shape[-1])
            return (r3(q), r3(kb), vf4.reshape(nb, t, H_A, DVA), r3(glu), r3(hraw),
                    k_extra if feature_major else r3(k_extra), v_extra if feature_major else r3(v_extra))

        q, kb, vf_s, glu, hraw, kf_s, vb = mixer_front(xs, nbs * ts, False)
        oa = _attn_small(q, cache_kt, cache_v4, l, kb, vb, bias_past, bias_new, *common_attn)
        ob, conv_s = _conv_module(glu, jnp.concatenate([state_conv[l], zrow(state_conv[l])]),
                                  conv_w[l], row(conv_b), row(ln_g), row(ln_b), ts, False)
        oc, hg_s = _hgrn(hraw, lb_consts[l], row(g_hgrn), seg_mean, seg_sum, blockdiag,
                         _state_to_kernel(jnp.concatenate([state_hgrn[l], zrow(state_hgrn[l])]).astype(F32)),
                         small_rows, ts, False)
        xs, ffn_s = _ffn(xs, oa, ob, oc, wout, row(g_ffn), wup, ffn_conv_w[l], row(ffn_conv_b), wdn,
                         jnp.concatenate([state_ffn[l], zrow(state_ffn[l])]), g_final[None],
                         nbs, ts, False, last)
        meta_k, meta_vt = kb[bs:], vb[bs].T
        gdt = jnp.broadcast_to(g_diff[l][:, None], (W_A, LANES))

        q, kb, vf_m, glu, hraw, kft_m, vt = mixer_front(xm, tm_main, True)
        oa = _attn_main(q, kb, vt, meta_k, meta_vt, bt_main, bt_meta, far_bias, *common_attn[:-1], gdt, tq, tk)
        ob, conv_m = _conv_module(glu, conv_s[bs:], conv_w[l], row(conv_b), row(ln_g), row(ln_b),
                                  _pick_tile(seq, 512), True)
        oc, hg_m = _hgrn(hraw, lb_consts[l], row(g_hgrn), seg_mean, seg_sum, blockdiag, hg_s[bs:],
                         1, _pick_tile(seq, 256), True)
        xm, ffn_m = _ffn(xm, oa, ob, oc, wout, row(g_ffn), wup, ffn_conv_w[l], row(ffn_conv_b), wdn,
                         ffn_s[bs:], g_final[None], 1, _pick_tile(seq, 512), True, last)

        meta_rows = lambda a: jnp.broadcast_to(a[bs:], (bp,) + a.shape[1:])
        kt_p = jnp.concatenate([meta_rows(jnp.swapaxes(kf_s, 1, 2)), kft_m], axis=2)
        outs["kp"].append(jnp.swapaxes(kt_p, 1, 2).reshape(bp, N_META + seq, H_A, 2, DQK))
        outs["vp"].append(jnp.concatenate([meta_rows(vf_s), vf_m], axis=1))
        outs["cp"].append(conv_m)
        outs["hp"].append(_state_from_kernel(hg_m))
        outs["fp"].append(ffn_m)
        outs["ks"].append(kf_s[:bs].reshape(bs, ts, H_A, 2, DQK))
        outs["vs"].append(vf_s[:bs])
        outs["cs"].append(conv_s[:bs])
        outs["hs"].append(_state_from_kernel(hg_s[:bs]))
        outs["fs"].append(ffn_s[:bs])

    st = {k: jnp.stack(v) for k, v in outs.items()}
    return (xm, xs[:bs], st["kp"], st["vp"], st["cp"], st["hp"], st["fp"],
            st["ks"], st["vs"], st["cs"], st["hs"], st["fs"])
```

```python
import functools
import math

import jax
import jax.numpy as jnp
from jax import lax
from jax.experimental import pallas as pl
from jax.experimental.pallas import tpu as pltpu

F32 = jnp.float32
BF16 = jnp.bfloat16
I32 = jnp.int32

D_MODEL = 1024
N_META = 16
CHUNK = 64
DQK = 64
DVA = 2 * DQK
H_A = 4
W_A = H_A * DVA
C_CONV = 256
CONV_W = 31
H_C = 4
DK_C = 64
DV_C = 64
W_C = H_C * DV_C
D_FF = 2816
FFN_CONV_W = 3
N_BUCKETS = 32
MAX_DIST = 128
REC_BLOCK = 16
EPS = 1e-6
NEG = -1e30
LOG2E = math.log2(math.e)
IN_COLS = 2 * W_A + W_A + 2 * C_CONV + 4 * W_C

LANES = 128
SUBLANES = 8
VMEM_LIMIT = 56 * 1024 * 1024

_NB = N_BUCKETS // 2
_MAX_EXACT = _NB // 2
_BUCKET_THR = tuple(
    next(n for n in range(_MAX_EXACT, MAX_DIST + 1)
         if n ** (_NB - _MAX_EXACT) * _MAX_EXACT ** k >= _MAX_EXACT ** (_NB - _MAX_EXACT) * MAX_DIST ** k)
    for k in range(1, _NB - _MAX_EXACT))


def _rms(x, g):
    return x * lax.rsqrt(jnp.mean(x * x, axis=-1, keepdims=True) + EPS) * g


def _dot(a, b):
    return jnp.dot(a, b, preferred_element_type=F32)


def _dot_nt(a, b):
    return lax.dot_general(a, b, (((1,), (1,)), ((), ())), preferred_element_type=F32)


def _dot_tn(a, b):
    return lax.dot_general(a, b, (((0,), (0,)), ((), ())), preferred_element_type=F32)


def _sigmoid(x):
    return jax.nn.sigmoid(x)


def _const_spec(shape):
    nd = len(shape)
    return pl.BlockSpec(shape, lambda *_: (0,) * nd, pipeline_mode=pl.Buffered(1))


def _bias_kernel(tab_ref, o_ref, *, qpos0, kpos0, prompt_chunks, keys_on_rows):
    h = pl.program_id(0)
    shape = o_ref.shape[1:]
    r = lax.broadcasted_iota(I32, shape, 1 if keys_on_rows else 0) + qpos0
    c = lax.broadcasted_iota(I32, shape, 0 if keys_on_rows else 1) + kpos0
    rel = c - r
    n = jnp.abs(rel)
    large = jnp.full(shape, _MAX_EXACT, I32)
    for thr in _BUCKET_THR:
        large = large + (n >= thr).astype(I32)
    bucket = jnp.where(rel > 0, _NB, 0) + jnp.where(n < _MAX_EXACT, n, large)
    out = jnp.zeros(shape, F32)
    for b in range(N_BUCKETS):
        out = jnp.where(bucket == b, tab_ref[b, h] * LOG2E, out)
    shift = CHUNK - N_META if prompt_chunks else 0
    log2_chunk = CHUNK.bit_length() - 1
    visible = ((c + shift) >> log2_chunk) <= ((r + shift) >> log2_chunk)
    o_ref[0] = jnp.where(visible, out, NEG)


def _bias_tile(rel_bias, rows, cols, qpos0, kpos0, prompt_chunks, keys_on_rows=False):
    return pl.pallas_call(
        functools.partial(_bias_kernel, qpos0=qpos0, kpos0=kpos0, prompt_chunks=prompt_chunks,
                          keys_on_rows=keys_on_rows),
        grid=(H_A,),
        in_specs=[pl.BlockSpec(memory_space=pltpu.SMEM)],
        out_specs=pl.BlockSpec((1, rows, cols), lambda h: (h, 0, 0)),
        out_shape=jax.ShapeDtypeStruct((H_A, rows, cols), F32),
        name="relpos_bias",
    )(rel_bias)


def _inproj_kernel(x_ref, gmix_ref, win_ref, gq_ref, gk_ref, seg_ref,
                   q_ref, kb_ref, vf4_ref, glu_ref, hraw_ref, k_extra_ref, v_extra_ref, *, feature_major):
    tm = x_ref.shape[0]
    h = _rms(x_ref[...], gmix_ref[...]).astype(BF16)

    def proj(lo, hi):
        return _dot(h, win_ref[:, lo:hi])

    def group_norm(a, g):
        ms = _dot((a * a).astype(BF16), seg_ref[...])
        return a * lax.rsqrt(ms + EPS) * g

    qa = group_norm(proj(0, W_A), gq_ref[...])
    q_ref[...] = (qa * (DQK ** -0.5 * LOG2E)).astype(BF16)
    ka = group_norm(proj(W_A, 2 * W_A), gk_ref[...])
    kb_ref[...] = ka.astype(BF16)
    va = proj(2 * W_A, 3 * W_A)
    for hd in range(H_A):
        vf4_ref[pl.ds(hd, tm, stride=H_A), :] = va[:, hd * DVA:(hd + 1) * DVA]
    if feature_major:
        k_extra_ref[0] = ka.T
        v_extra_ref[...] = va.T.astype(BF16)
    else:
        k_extra_ref[...] = ka
        v_extra_ref[...] = va.astype(BF16)
    glu_ref[...] = proj(3 * W_A, 3 * W_A + 2 * C_CONV)
    hraw_ref[...] = proj(3 * W_A + 2 * C_CONV, IN_COLS)


def _inproj(x2d, gmix, win, gq, gk, seg, tm, rows_per_seq, feature_major):
    n = x2d.shape[0]
    row = lambda w: pl.BlockSpec((tm, w), lambda i: (i, 0))
    outs = [(W_A, BF16), (W_A, BF16), None, (2 * C_CONV, F32), (4 * W_C, F32)]
    out_specs = [row(o[0]) if o else pl.BlockSpec((tm * H_A, DVA), lambda i: (i, 0)) for o in outs]
    out_shape = [jax.ShapeDtypeStruct((n, o[0]), o[1]) if o else jax.ShapeDtypeStruct((n * H_A, DVA), F32)
                 for o in outs]
    if feature_major:
        nt = rows_per_seq // tm
        out_specs += [pl.BlockSpec((1, W_A, tm), lambda i: (i // nt, 0, i % nt)),
                      pl.BlockSpec((W_A, tm), lambda i: (0, i))]
        out_shape += [jax.ShapeDtypeStruct((n // rows_per_seq, W_A, rows_per_seq), F32),
                      jax.ShapeDtypeStruct((W_A, n), BF16)]
    else:
        out_specs += [row(W_A), row(W_A)]
        out_shape += [jax.ShapeDtypeStruct((n, W_A), F32), jax.ShapeDtypeStruct((n, W_A), BF16)]
    return pl.pallas_call(
        functools.partial(_inproj_kernel, feature_major=feature_major),
        grid=(n // tm,),
        in_specs=[row(D_MODEL), _const_spec((1, D_MODEL)), _const_spec((D_MODEL, IN_COLS)),
                  _const_spec((1, W_A)), _const_spec((1, W_A)), _const_spec((W_A, W_A))],
        out_specs=out_specs,
        out_shape=out_shape,
        compiler_params=pltpu.CompilerParams(dimension_semantics=("arbitrary",), vmem_limit_bytes=VMEM_LIMIT),
        name="inproj",
    )(x2d, gmix, win, gq, gk, seg)


_CONV_PAD = 32


def _conv_kernel(glu_ref, past_ref, w_ref, b_ref, lng_ref, lnb_ref, ob_ref, new_ref, xc_ref, xr_ref, *, tt, rc):
    halo = CONV_W - 1
    h0 = _CONV_PAD - halo

    @pl.when(pl.program_id(1) == 0)
    def _():
        xc_ref[0:h0, :] = jnp.zeros((h0, C_CONV), F32)
        xc_ref[h0:_CONV_PAD, :] = past_ref[0]

    glu = glu_ref[0]
    xc_ref[_CONV_PAD:_CONV_PAD + tt, :] = glu[:, :C_CONV] * _sigmoid(glu[:, C_CONV:])
    span = xr_ref.shape[1]
    for r in range(1, SUBLANES):
        xr_ref[r - 1] = xc_ref[r:r + span, :]
    bias = b_ref[...]
    for c in range(tt // rc):
        acc = jnp.zeros((rc, C_CONV), F32) + bias
        for j in range(CONV_W):
            r = (j + h0) % SUBLANES
            base = c * rc + j + h0 - r
            rows = xc_ref[base:base + rc, :] if r == 0 else xr_ref[r - 1, base:base + rc, :]
            acc = acc + w_ref[j:j + 1, :] * rows
        mu = jnp.mean(acc, axis=-1, keepdims=True)
        xm = acc - mu
        var = jnp.mean(xm * xm, axis=-1, keepdims=True)
        y = xm * lax.rsqrt(var + EPS) * lng_ref[...] + lnb_ref[...]
        ob_ref[0, c * rc:(c + 1) * rc, :] = (y * _sigmoid(y)).astype(BF16)
    new = xc_ref[h0 + tt:_CONV_PAD + tt, :]
    new_ref[0] = new
    xc_ref[h0:_CONV_PAD, :] = new


def _conv_module(glu3, past, w, b, lng, lnb, tt, shared_past):
    nb, t, _ = glu3.shape
    rc = min(tt, 128)
    pidx = (lambda bi, ti: (0, 0, 0)) if shared_past else (lambda bi, ti: (bi, 0, 0))
    return pl.pallas_call(
        functools.partial(_conv_kernel, tt=tt, rc=rc),
        grid=(nb, t // tt),
        in_specs=[pl.BlockSpec((1, tt, 2 * C_CONV), lambda bi, ti: (bi, ti, 0)),
                  pl.BlockSpec((1, CONV_W - 1, C_CONV), pidx),
                  _const_spec((CONV_W, C_CONV)), _const_spec((1, C_CONV)),
                  _const_spec((1, C_CONV)), _const_spec((1, C_CONV))],
        out_specs=[pl.BlockSpec((1, tt, C_CONV), lambda bi, ti: (bi, ti, 0)),
                   pl.BlockSpec((1, CONV_W - 1, C_CONV), lambda bi, ti: (bi, 0, 0))],
        out_shape=[jax.ShapeDtypeStruct((nb, t, C_CONV), BF16),
                   jax.ShapeDtypeStruct((nb, CONV_W - 1, C_CONV), F32)],
        scratch_shapes=[pltpu.VMEM((_CONV_PAD + tt, C_CONV), F32),
                        pltpu.VMEM((SUBLANES - 1, _CONV_PAD + tt - SUBLANES, C_CONV), F32)],
        compiler_params=pltpu.CompilerParams(dimension_semantics=("arbitrary", "arbitrary")),
        name="conv_module",
    )(glu3, past, w, b, lng, lnb)


def _hgrn_kernel(hraw_ref, lbc_ref, ghg_ref, segm_ref, segs_ref, bd_ref, rsum_ref, s0_ref,
                 oc_ref, sout_ref, s_scr, *, tc, chained):
    nblk = tc // REC_BLOCK
    blk3 = (nblk, REC_BLOCK, W_C)

    if chained:
        @pl.when(pl.program_id(1) == 0)
        def _():
            s_scr[...] = s0_ref[0]

    hr = hraw_ref[...].reshape(tc, 4 * W_C)
    z = hr[:, 0:W_C]
    v = hr[:, W_C:2 * W_C]
    q = hr[:, 2 * W_C:3 * W_C]
    gc = hr[:, 3 * W_C:4 * W_C]
    log_lb = lbc_ref[0:1, :]
    log_1m_lb = lbc_ref[1:2, :]
    one_m_lb = lbc_ref[2:3, :]

    log_sig = jnp.minimum(z, 0.0) - jnp.log(1.0 + jnp.exp(-jnp.abs(z)))
    b_arg = log_1m_lb + log_sig
    log_f = jnp.maximum(log_lb, b_arg) + jnp.log(1.0 + jnp.exp(-jnp.abs(log_lb - b_arg)))
    kc = one_m_lb * _sigmoid(-z)

    pos = lax.broadcasted_iota(I32, (tc, W_C), 0) & (REC_BLOCK - 1)
    bcum = log_f
    sh = 1
    while sh < REC_BLOCK:
        bcum = bcum + jnp.where(pos >= sh, pltpu.roll(bcum, sh, axis=0), 0.0)
        sh *= 2

    b3 = bcum.reshape(blk3)
    b_last = b3[:, REC_BLOCK - 1:REC_BLOCK, :]
    qe = (q * jnp.exp(bcum)).astype(BF16)
    ke = (kc.reshape(blk3) * jnp.exp(b_last - b3)).reshape(tc, W_C).astype(BF16)
    v_bf = v.astype(BF16)
    dl = jnp.exp(b_last)

    half = REC_BLOCK // 2
    slab = (nblk, 2, half, W_C)
    hrows = nblk * half
    b_lo, b_hi = (bcum.reshape(slab)[:, i] for i in range(2))
    k_lo, k_hi = (kc.reshape(slab)[:, i] for i in range(2))
    q_lo, q_hi = (q.reshape(slab)[:, i] for i in range(2))
    v_lo, v_hi = (v.reshape(slab)[:, i] for i in range(2))
    sidx = lax.broadcasted_iota(I32, (nblk, half, W_C), 1)

    def head_sums(xs):
        stacked = jnp.concatenate([x.reshape(hrows, W_C) for x in xs], axis=0).astype(BF16)
        a = _dot(stacked, segs_ref[...])
        return [a[t * hrows:(t + 1) * hrows, :].reshape(nblk, half, W_C) for t in range(half)]

    def slab_sums(xs):
        side = jnp.concatenate([x.reshape(hrows, W_C) for x in xs], axis=1).astype(BF16)
        r = _dot(rsum_ref[...], side)
        return [r[:, t * W_C:(t + 1) * W_C].reshape(nblk, half, W_C) for t in range(half)]

    def place(rows):
        out = jnp.zeros((nblk, half, W_C), F32)
        for t in range(half):
            out = jnp.where(sidx == t, rows[t], out)
        return out

    row_t = lambda a, t: a[:, t:t + 1, :]
    x_lo = [jnp.where(sidx <= t, jnp.exp(jnp.minimum(row_t(b_lo, t) - b_lo, 0.0)), 0.0) * k_lo * row_t(q_lo, t)
            for t in range(half)]
    o_lo = place(slab_sums([a * v_lo for a in head_sums(x_lo)]))
    x_hl = [jnp.exp(jnp.minimum(row_t(b_hi, t) - b_lo, 0.0)) * k_lo * row_t(q_hi, t) for t in range(half)]
    x_hh = [jnp.where(sidx <= t, jnp.exp(jnp.minimum(row_t(b_hi, t) - b_hi, 0.0)), 0.0) * k_hi * row_t(q_hi, t)
            for t in range(half)]
    o_hi = place(slab_sums([al * v_lo + ah * v_hi for al, ah in zip(head_sums(x_hl), head_sums(x_hh))]))
    o_intra = jnp.stack([o_lo, o_hi], axis=1).reshape(tc, W_C)

    blk = lambda a, j: a[j * REC_BLOCK:(j + 1) * REC_BLOCK, :]
    upd = [_dot_tn(blk(v_bf, j), blk(ke, j)) * bd_ref[...] for j in range(nblk)]

    def compact(s):
        folded = s
        for hd in range(1, H_C):
            folded = folded + pltpu.roll(s, hd * DK_C, axis=1)
        return folded[:, 0:DK_C]

    if chained:
        s = s_scr[...]
        s_before = []
        for j in range(nblk):
            s_before.append(s.astype(BF16))
            s = dl[j] * s + upd[j]
        s_scr[...] = s
    else:
        s_before = [s0_ref[j].astype(BF16) for j in range(nblk)]
        for j in range(nblk):
            sout_ref[j] = compact(dl[j] * s0_ref[j] + upd[j])
    o_inter = jnp.concatenate([_dot_nt(blk(qe, j), s_before[j]) for j in range(nblk)], axis=0)

    o = o_intra + o_inter
    ms = _dot((o * o).astype(BF16), segm_ref[...])
    oc = (o * lax.rsqrt(ms + EPS) * ghg_ref[...] * (gc * _sigmoid(gc))).astype(BF16)
    oc_ref[...] = oc.reshape(oc_ref.shape)

    if chained:
        @pl.when(pl.program_id(1) == pl.num_programs(1) - 1)
        def _():
            sout_ref[0] = compact(s_scr[...])


def _hgrn(hraw3, lbc, ghg, segm, segs, bd, s0, rows, tc, shared_past):
    nb, t, _ = hraw3.shape
    chained = rows == 1
    assert chained or (t == tc == REC_BLOCK and not shared_past)
    sidx = (lambda bi, ti: (0, 0, 0)) if shared_past else (lambda bi, ti: (bi, 0, 0))
    hrows = rows * tc // 2
    rsum = _seg_matrix(hrows, REC_BLOCK // 2, 1.0, BF16)
    return pl.pallas_call(
        functools.partial(_hgrn_kernel, tc=rows * tc, chained=chained),
        grid=(nb // rows, t // tc),
        in_specs=[pl.BlockSpec((rows, tc, 4 * W_C), lambda bi, ti: (bi, ti, 0)),
                  _const_spec((SUBLANES, W_C)), _const_spec((1, W_C)),
                  _const_spec((W_C, W_C)), _const_spec((W_C, W_C)), _const_spec((W_C, W_C)),
                  _const_spec((hrows, hrows)),
                  pl.BlockSpec((rows, W_C, W_C), sidx)],
        out_specs=[pl.BlockSpec((rows, tc, W_C), lambda bi, ti: (bi, ti, 0)),
                   pl.BlockSpec((rows, W_C, DK_C), lambda bi, ti: (bi, 0, 0))],
        out_shape=[jax.ShapeDtypeStruct((nb, t, W_C), BF16),
                   jax.ShapeDtypeStruct((nb, W_C, DK_C), F32)],
        scratch_shapes=[pltpu.VMEM((W_C, W_C), F32)],
        compiler_params=pltpu.CompilerParams(dimension_semantics=("arbitrary", "arbitrary")),
        name="hgrn2",
    )(hraw3, lbc, ghg, segm, segs, bd, rsum, s0)


def _split_halves(q):
    lane = lax.broadcasted_iota(I32, q.shape, 1)
    zero = jnp.zeros_like(q)
    return jnp.concatenate([jnp.where(lane < DQK, q, zero), jnp.where(lane >= DQK, q, zero)], axis=0)


def _twice(b):
    return jnp.concatenate([b, b], axis=0)


def _diff_combine(acc, l, t, lam_init, lq1, lk1, lq2, lk2, gd):
    o = acc / l
    lam = (jnp.exp(jnp.sum(lq1 * lk1, axis=1, keepdims=True))
           - jnp.exp(jnp.sum(lq2 * lk2, axis=1, keepdims=True)) + lam_init)
    a = o[:t] - lam * o[t:]
    return (_rms(a, gd) * (1.0 - lam_init)).astype(BF16)


def _attn_main_kernel(sc_ref, farc_ref, q_ref, k_ref, vt_ref, km_ref, vmt_ref, bt_ref, btm_ref,
                      lq1_ref, lk1_ref, lq2_ref, lk2_ref, gdt_ref, o_ref,
                      acc_scr, s_scr, *, tq, tk):
    h = pl.program_id(1)
    i = pl.program_id(2)
    q = q_ref[0]
    lane = lax.broadcasted_iota(I32, q.shape, 1)
    zero = jnp.zeros_like(q)
    q_half = (jnp.where(lane < DQK, q, zero), jnp.where(lane >= DQK, q, zero))
    n_tiles = (i + 1) * (tq // tk)

    def scores(j, slot):
        k_tile = k_ref[0, pl.ds(pl.multiple_of(j * tk, tk), tk), :]
        for c in range(2):
            s_scr[slot, c] = _dot_nt(k_tile, q_half[c])

    n_var = bt_ref.shape[0]
    far_bias = farc_ref[h]

    ones_rows = jnp.ones((2 * SUBLANES, tk), BF16)

    def softmax_pv(j, slot, stats, extra=None, all_far=False, masked_cols=0):
        bidx = jnp.minimum(n_tiles - 1 - j, n_var - 1)
        v_ones = jnp.concatenate([vt_ref[:, pl.ds(pl.multiple_of(j * tk, tk), tk)], ones_rows], axis=0)
        new_stats = []
        for c in range(2):
            m_prev, l_prev = stats[c]
            strips = []
            for w in range(tq // LANES):
                cols = slice(w * LANES, (w + 1) * LANES)
                visible = (w + 1) * LANES > masked_cols
                m_new = m_prev[:, cols]
                if visible and all_far:
                    s = s_scr[slot, c, :, cols]
                    m_new = jnp.maximum(m_new, jnp.max(s, axis=0, keepdims=True) + far_bias)
                elif visible:
                    s = s_scr[slot, c, :, cols] + bt_ref[bidx, 0, :, cols]
                    m_new = jnp.maximum(m_new, jnp.max(s, axis=0, keepdims=True))
                if extra is not None:
                    s_x = extra[c][:, cols]
                    m_new = jnp.maximum(m_new, jnp.max(s_x, axis=0, keepdims=True))
                shift = m_new - far_bias if all_far else m_new
                alpha = jnp.exp2(m_prev[:, cols] - m_new)
                p = jnp.exp2((s - shift).astype(BF16)) if visible else jnp.zeros((tk, LANES), BF16)
                p_x = l_x = None
                if extra is not None:
                    p_x = jnp.exp2(s_x - m_new)
                    l_x = jnp.sum(p_x, axis=0, keepdims=True)
                    p_x = p_x.astype(BF16)
                strips.append((p, m_new, alpha, p_x, l_x))
            cat = lambda k: jnp.concatenate([st[k] for st in strips], axis=1)
            p_tile, alpha = cat(0), cat(2)
            pv_sum = _dot(v_ones, p_tile)
            pv = pv_sum[0:DVA, :]
            l_new = alpha * l_prev + pv_sum[DVA:DVA + 1, :]
            if extra is not None:
                l_new = l_new + cat(4)
                pv = pv + _dot(vmt_ref[...], cat(3))
            acc_scr[c] = alpha * acc_scr[c] + pv
            new_stats.append((cat(1), l_new))
        return tuple(new_stats)

    def pair(n, stats, all_far):
        scores(2 * n + 1, 1)
        stats = softmax_pv(2 * n, 0, stats, all_far=all_far)
        scores(2 * n + 2, 0)
        return softmax_pv(2 * n + 1, 1, stats, all_far=all_far)

    assert (tq // tk) % 2 == 0
    acc_scr[...] = jnp.zeros(acc_scr.shape, F32)
    stat0 = (jnp.full((1, tq), -jnp.inf, F32), jnp.zeros((1, tq), F32))
    scores(0, 0)
    n_pairs = n_tiles // 2 - 1
    n_far_pairs = jnp.clip((n_tiles + 1 - n_var) // 2, 0, n_pairs)
    stats = lax.fori_loop(0, n_far_pairs, functools.partial(pair, all_far=True), (stat0, stat0))
    stats = lax.fori_loop(n_far_pairs, n_pairs, functools.partial(pair, all_far=False), stats)
    scores(n_tiles - 1, 1)
    stats = softmax_pv(n_tiles - 2, 0, stats)
    bias_meta = jnp.where(i == 0, btm_ref[0], farc_ref[h])
    s_meta = [_dot_nt(km_ref[0], q_half[c]) + bias_meta for c in range(2)]
    stats = softmax_pv(n_tiles - 1, 1, stats, s_meta, masked_cols=tq - tk)
    o = [acc_scr[c] * (1.0 / stats[c][1]) for c in range(2)]

    lam_init = sc_ref[0]
    lam = (jnp.exp(jnp.sum(lq1_ref[...] * lk1_ref[...], axis=1, keepdims=True))
           - jnp.exp(jnp.sum(lq2_ref[...] * lk2_ref[...], axis=1, keepdims=True)) + lam_init)
    a = o[0] - lam * o[1]
    gain = jnp.concatenate([gdt_ref[...]] * (tq // LANES), axis=1)
    y = a * lax.rsqrt(jnp.mean(a * a, axis=0, keepdims=True) + EPS) * gain * (1.0 - lam_init)
    o_ref[0] = y.T.astype(BF16)


def _attn_bias_tiles(rel_bias, tq, tk):
    n_var = -(-(tq + MAX_DIST - 1) // tk) + 1
    base = N_META + n_var * tk
    return jnp.stack([_bias_tile(rel_bias, tk, tq, base, base + tq - (d + 1) * tk, True, keys_on_rows=True)
                      for d in range(n_var)])


def _attn_main(q3, k3, vt, km, vmt, bt, btm, farc, lam_init, lq1, lk1, lq2, lk2, gdt, tq, tk):
    nb, t, _ = q3.shape
    p = km.shape[1]
    assert tq % tk == 0 and tk % CHUNK == 0 and bt.shape[2:] == (tk, tq)
    smem = pl.BlockSpec(memory_space=pltpu.SMEM)
    vec = lambda: _const_spec((1, DQK))
    return pl.pallas_call(
        functools.partial(_attn_main_kernel, tq=tq, tk=tk),
        grid=(nb, H_A, t // tq),
        in_specs=[smem, smem,
                  pl.BlockSpec((1, tq, DVA), lambda b, h, i: (b, i, h)),
                  pl.BlockSpec((1, t, DVA), lambda b, h, i: (b, 0, h)),
                  pl.BlockSpec((DVA, t), lambda b, h, i: (h, b)),
                  pl.BlockSpec((1, p, DVA), lambda b, h, i: (0, 0, h)),
                  pl.BlockSpec((DVA, p), lambda b, h, i: (h, 0)),
                  pl.BlockSpec((bt.shape[0], 1, tk, tq), lambda b, h, i: (0, h, 0, 0)),
                  pl.BlockSpec((1, p, tq), lambda b, h, i: (h, 0, 0)),
                  vec(), vec(), vec(), vec(),
                  pl.BlockSpec((DVA, LANES), lambda b, h, i: (h, 0))],
        out_specs=pl.BlockSpec((1, tq, DVA), lambda b, h, i: (b, i, h)),
        out_shape=jax.ShapeDtypeStruct((nb, t, W_A), BF16),
        scratch_shapes=[pltpu.VMEM((2, DVA, tq), F32), pltpu.VMEM((2, 2, tk, tq), F32)],
        compiler_params=pltpu.CompilerParams(dimension_semantics=("arbitrary",) * 3),
        name="attn_main",
    )(lam_init, farc, q3, k3, vt, km, vmt, bt, btm, lq1, lk1, lq2, lk2, gdt)


def _attn_small_kernel(sc_ref, q_ref, kc_ref, vc_ref, kn_ref, vn_ref, bp_ref, bn_ref,
                       lq1_ref, lk1_ref, lq2_ref, lk2_ref, gd_ref, o_ref, *, t, n_cached):
    b = pl.program_id(0)
    p = vc_ref.shape[2] // H_A
    past_bias = jnp.where(b < n_cached, 0.0, NEG)
    q_all = q_ref[0]
    kn_all = kn_ref[0]
    vn_all = vn_ref[0]
    outs = []
    for hd in range(H_A):
        cols = slice(hd * DVA, (hd + 1) * DVA)
        qq = _split_halves(q_all[:, cols])
        s_p = _dot(qq, kc_ref[0, 0, cols, :].astype(BF16)) + _twice(bp_ref[hd]) + past_bias
        s_n = _dot_nt(qq, kn_all[:, cols]) + _twice(bn_ref[hd])
        m = jnp.maximum(jnp.max(s_p, axis=1, keepdims=True), jnp.max(s_n, axis=1, keepdims=True))
        p_p = jnp.exp2(s_p - m)
        p_n = jnp.exp2(s_n - m)
        l = jnp.sum(p_p, axis=1, keepdims=True) + jnp.sum(p_n, axis=1, keepdims=True)
        v_past = vc_ref[0, 0, pl.ds(hd, p, stride=H_A), :].astype(BF16)
        acc = _dot(p_p.astype(BF16), v_past) + _dot(p_n.astype(BF16), vn_all[:, cols])
        outs.append(_diff_combine(acc, l, t, sc_ref[0], lq1_ref[...], lk1_ref[...],
                                  lq2_ref[...], lk2_ref[...], gd_ref[:, cols]))
    o_ref[0] = jnp.concatenate(outs, axis=1)


def _attn_small(q3, cache_kt, cache_v4, layer, kn, vn, bp, bn, lam_init, lq1, lk1, lq2, lk2, gdiff):
    nb, t, _ = q3.shape
    n_cached, p = cache_kt.shape[1], cache_kt.shape[3]
    smem = pl.BlockSpec(memory_space=pltpu.SMEM)
    vec = lambda: _const_spec((1, DQK))
    cached = lambda b: (layer, jnp.minimum(b, n_cached - 1), 0, 0)
    new = pl.BlockSpec((1, t, W_A), lambda b: (b, 0, 0))
    return pl.pallas_call(
        functools.partial(_attn_small_kernel, t=t, n_cached=n_cached),
        grid=(nb,),
        in_specs=[smem, new,
                  pl.BlockSpec((1, 1, W_A, p), cached), pl.BlockSpec((1, 1, p * H_A, DVA), cached),
                  new, new, _const_spec((H_A, t, p)), _const_spec((H_A, t, t)),
                  vec(), vec(), vec(), vec(), _const_spec((1, W_A))],
        out_specs=new,
        out_shape=jax.ShapeDtypeStruct((nb, t, W_A), BF16),
        compiler_params=pltpu.CompilerParams(dimension_semantics=("arbitrary",)),
        name="attn_small",
    )(lam_init, q3, cache_kt, cache_v4, kn, vn, bp, bn, lq1, lk1, lq2, lk2, gdiff)


_MXU_N = 256
_FF_CHUNK = 6 * _MXU_N
_FF_BOUNDS = tuple((lo, min(lo + _FF_CHUNK, D_FF)) for lo in range(0, D_FF, _FF_CHUNK))
assert all((hi - lo) % _MXU_N == 0 for lo, hi in _FF_BOUNDS)


def _ffn_kernel(x_ref, oa_ref, ob_ref, oc_ref, wout_ref, gffn_ref, wup_ref, cw_ref, cb_ref, wdn_ref,
                past_ref, gfin_ref, y_ref, new_ref, halo_scr, *, bb, tt, final_norm):
    rows = bb * tt
    taps = FFN_CONV_W - 1

    @pl.when(pl.program_id(1) == 0)
    def _():
        halo_scr[...] = past_ref[...]

    x1 = (x_ref[...].reshape(rows, D_MODEL)
          + _dot(oa_ref[...].reshape(rows, W_A), wout_ref[0:W_A, :])
          + _dot(ob_ref[...].reshape(rows, C_CONV), wout_ref[W_A:W_A + C_CONV, :])
          + _dot(oc_ref[...].reshape(rows, W_C), wout_ref[W_A + C_CONV:D_MODEL, :]))
    h2 = _rms(x1, gffn_ref[...]).astype(BF16)

    acc = jnp.zeros((rows, D_MODEL), F32)
    for lo, hi in _FF_BOUNDS:
        gate = _dot(h2, wup_ref[:, lo:hi])
        val = _dot(h2, wup_ref[:, D_FF + lo:D_FF + hi])
        gate3 = gate.reshape(bb, tt, hi - lo)
        tpos = lax.broadcasted_iota(I32, gate3.shape, 1)
        conv = cw_ref[taps:taps + 1, lo:hi] * gate + cb_ref[:, lo:hi]
        for d in range(1, taps + 1):
            shifted = pltpu.roll(gate, d, axis=0).reshape(gate3.shape)
            for e in range(d):
                shifted = jnp.where(tpos == e, halo_scr[:, taps - d + e:taps - d + e + 1, lo:hi], shifted)
            conv = conv + cw_ref[taps - d:taps - d + 1, lo:hi] * shifted.reshape(rows, hi - lo)
        halo_scr[:, :, lo:hi] = gate3[:, tt - taps:tt, :]
        gelu = 0.5 * conv * (1.0 + jnp.tanh(math.sqrt(2.0 / math.pi) * (conv + 0.044715 * (conv * conv * conv))))
        acc = acc + _dot((gelu * val).astype(BF16), wdn_ref[lo:hi, :])

    y = x1 + acc
    if final_norm:
        y = _rms(y, gfin_ref[...])
    y_ref[...] = y.reshape(bb, tt, D_MODEL)
    new_ref[...] = halo_scr[...]


def _ffn(x3, oa3, ob3, oc3, wout, gffn, wup, cw, cb, wdn, past, gfin, bb, tt, shared_past, final_norm):
    nb, t, _ = x3.shape
    taps = FFN_CONV_W - 1
    assert tt >= taps
    tok = lambda w: pl.BlockSpec((bb, tt, w), lambda bi, ti: (bi, ti, 0))
    pidx = (lambda bi, ti: (0, 0, 0)) if shared_past else (lambda bi, ti: (bi, 0, 0))
    assert not (shared_past and bb != 1)
    return pl.pallas_call(
        functools.partial(_ffn_kernel, bb=bb, tt=tt, final_norm=final_norm),
        grid=(nb // bb, t // tt),
        in_specs=[tok(D_MODEL), tok(W_A), tok(C_CONV), tok(W_C),
                  _const_spec((D_MODEL, D_MODEL)), _const_spec((1, D_MODEL)),
                  _const_spec((D_MODEL, 2 * D_FF)), _const_spec((FFN_CONV_W, D_FF)), _const_spec((1, D_FF)),
                  _const_spec((D_FF, D_MODEL)),
                  pl.BlockSpec((bb, taps, D_FF), pidx), _const_spec((1, D_MODEL))],
        out_specs=[tok(D_MODEL), pl.BlockSpec((bb, taps, D_FF), lambda bi, ti: (bi, 0, 0))],
        out_shape=[jax.ShapeDtypeStruct((nb, t, D_MODEL), F32),
                   jax.ShapeDtypeStruct((nb, taps, D_FF), F32)],
        scratch_shapes=[pltpu.VMEM((bb, taps, D_FF), F32)],
        compiler_params=pltpu.CompilerParams(dimension_semantics=("arbitrary", "arbitrary"),
                                             vmem_limit_bytes=VMEM_LIMIT),
        name="outproj_ffn",
    )(x3, oa3, ob3, oc3, wout, gffn, wup, cw, cb, wdn, past, gfin)


def _seg_matrix(width, group, value, dtype):
    g = jnp.arange(width, dtype=I32) // group
    return jnp.where(g[:, None] == g[None, :], value, 0.0).astype(dtype)


def _state_to_kernel(s):
    n = s.shape[0]
    eye = jnp.eye(H_C, dtype=s.dtype)
    full = jnp.einsum('nhdv,hg->nhvgd', s, eye)
    return full.reshape(n, W_C, H_C * DK_C)


def _state_from_kernel(sc):
    return jnp.swapaxes(sc.reshape(sc.shape[0], H_C, DV_C, DK_C), 2, 3)


def _state_expand(sc, blockdiag):
    return jnp.tile(sc, (1, 1, H_C)) * blockdiag


def _pick_tile(total, want):
    t = min(total, want)
    assert total % t == 0, (total, want)
    return t


def kernel(x_prompt, x_sample, cache_k, cache_v, state_conv, state_hgrn, state_ffn, meta_tokens, rel_bias, g_mix, w_in, g_q, g_k, lam_q1, lam_k1, lam_q2, lam_k2, g_diff, conv_w, conv_b, ln_g, ln_b, lb_logits, g_hgrn, w_out, g_ffn, w_up, ffn_conv_w, ffn_conv_b, w_down, g_final):
    depth = g_mix.shape[0]
    bp, seq, _ = x_prompt.shape
    bs, ts, _ = x_sample.shape
    past_len = cache_k.shape[2]
    assert ts == N_META == REC_BLOCK and meta_tokens.shape[0] == N_META
    assert seq % CHUNK == 0 and past_len % CHUNK == 0

    lb_all = jnp.cumsum(jax.nn.softmax(lb_logits.astype(F32), axis=0), axis=0)
    lb_all = lb_all - lb_all[0:1]
    lb_consts = jnp.stack([jnp.log(lb_all), jnp.log1p(-lb_all), 1.0 - lb_all]
                          + [jnp.zeros_like(lb_all)] * (SUBLANES - 3), axis=1)

    seg_qk = _seg_matrix(W_A, DQK, 1.0 / DQK, BF16)
    seg_mean = _seg_matrix(W_C, DV_C, 1.0 / DV_C, BF16)
    seg_sum = _seg_matrix(W_C, DK_C, 1.0, BF16)
    blockdiag = _seg_matrix(W_C, DK_C, 1.0, F32)

    tq = _pick_tile(seq, 512)
    tk = _pick_tile(tq, 256)
    bt_main = _attn_bias_tiles(rel_bias, tq, tk)
    bt_meta = _bias_tile(rel_bias, N_META, tq, N_META, 0, True, keys_on_rows=True)
    far_bias = rel_bias[_NB - 1, :] * LOG2E
    bias_past = _bias_tile(rel_bias, ts, past_len, past_len, 0, False)
    bias_new = _bias_tile(rel_bias, ts, ts, past_len, past_len, False)

    cache_kt = jnp.swapaxes(cache_k.reshape(depth, bs, past_len, W_A), 2, 3)
    cache_v4 = cache_v.reshape(depth, bs, past_len * H_A, DVA)

    xs = jnp.concatenate([x_sample, meta_tokens.astype(x_sample.dtype)[None]], axis=0)
    xm = x_prompt
    nbs = bs + 1
    zrow = lambda a: jnp.zeros((1,) + a.shape[1:], a.dtype)

    tm_main = _pick_tile(bp * seq, 512)
    small_rows = max(r for r in range(1, 17) if nbs % r == 0)
    outs = {k: [] for k in ("kp", "vp", "cp", "hp", "fp", "ks", "vs", "cs", "hs", "fs")}
    for l in range(depth):
        last = l == depth - 1
        lam_init = jnp.full((1,), 0.8 - 0.6 * math.exp(-0.3 * l), F32)
        win = w_in[l].astype(BF16)
        wout = w_out[l].astype(BF16)
        wup = w_up[l].astype(BF16)
        wdn = w_down[l].astype(BF16)
        gq = jnp.tile(g_q[l], 2 * H_A)[None]
        gk = jnp.tile(g_k[l], 2 * H_A)[None]
        row = lambda a: a[l][None]
        common_attn = (lam_init, row(lam_q1), row(lam_k1), row(lam_q2), row(lam_k2), row(g_diff))

        def mixer_front(x3, tm, feature_major):
            nb, t, _ = x3.shape
            q, kb, vf4, glu, hraw, k_extra, v_extra = _inproj(
                x3.reshape(nb * t, D_MODEL), row(g_mix), win, gq, gk, seg_qk, tm, t, feature_major)
            r3 = lambda a: a.reshape(nb, t, a.shape[-1])
            return (r3(q), r3(kb), vf4.reshape(nb, t, H_A, DVA), r3(glu), r3(hraw),
                    k_extra if feature_major else r3(k_extra), v_extra if feature_major else r3(v_extra))

        q, kb, vf_s, glu, hraw, kf_s, vb = mixer_front(xs, nbs * ts, False)
        oa = _attn_small(q, cache_kt, cache_v4, l, kb, vb, bias_past, bias_new, *common_attn)
        ob, conv_s = _conv_module(glu, jnp.concatenate([state_conv[l], zrow(state_conv[l])]),
                                  conv_w[l], row(conv_b), row(ln_g), row(ln_b), ts, False)
        oc, hg_s = _hgrn(hraw, lb_consts[l], row(g_hgrn), seg_mean, seg_sum, blockdiag,
                         _state_to_kernel(jnp.concatenate([state_hgrn[l], zrow(state_hgrn[l])]).astype(F32)),
                         small_rows, ts, False)
        xs, ffn_s = _ffn(xs, oa, ob, oc, wout, row(g_ffn), wup, ffn_conv_w[l], row(ffn_conv_b), wdn,
                         jnp.concatenate([state_ffn[l], zrow(state_ffn[l])]), g_final[None],
                         nbs, ts, False, last)
        meta_k, meta_vt = kb[bs:], vb[bs].T
        gdt = jnp.broadcast_to(g_diff[l][:, None], (W_A, LANES))

        q, kb, vf_m, glu, hraw, kft_m, vt = mixer_front(xm, tm_main, True)
        oa = _attn_main(q, kb, vt, meta_k, meta_vt, bt_main, bt_meta, far_bias, *common_attn[:-1], gdt, tq, tk)
        ob, conv_m = _conv_module(glu, conv_s[bs:], conv_w[l], row(conv_b), row(ln_g), row(ln_b),
                                  _pick_tile(seq, 512), True)
        oc, hg_m = _hgrn(hraw, lb_consts[l], row(g_hgrn), seg_mean, seg_sum, blockdiag,
                         _state_expand(hg_s[bs:], blockdiag), 1, _pick_tile(seq, 256), True)
        xm, ffn_m = _ffn(xm, oa, ob, oc, wout, row(g_ffn), wup, ffn_conv_w[l], row(ffn_conv_b), wdn,
                         ffn_s[bs:], g_final[None], 1, _pick_tile(seq, 512), True, last)

        meta_rows = lambda a: jnp.broadcast_to(a[bs:], (bp,) + a.shape[1:])
        kt_p = jnp.concatenate([meta_rows(jnp.swapaxes(kf_s, 1, 2)), kft_m], axis=2)
        outs["kp"].append(jnp.swapaxes(kt_p, 1, 2).reshape(bp, N_META + seq, H_A, 2, DQK))
        outs["vp"].append(jnp.concatenate([meta_rows(vf_s), vf_m], axis=1))
        outs["cp"].append(conv_m)
        outs["hp"].append(_state_from_kernel(hg_m))
        outs["fp"].append(ffn_m)
        outs["ks"].append(kf_s[:bs].reshape(bs, ts, H_A, 2, DQK))
        outs["vs"].append(vf_s[:bs])
        outs["cs"].append(conv_s[:bs])
        outs["hs"].append(_state_from_kernel(hg_s[:bs]))
        outs["fs"].append(ffn_s[:bs])

    st = {k: jnp.stack(v) for k, v in outs.items()}
    return (xm, xs[:bs], st["kp"], st["vp"], st["cp"], st["hp"], st["fp"],
            st["ks"], st["vs"], st["cs"], st["hs"], st["fs"])
```

```python
import functools
import math

import jax
import jax.numpy as jnp
from jax import lax
from jax.experimental import pallas as pl
from jax.experimental.pallas import tpu as pltpu

F32 = jnp.float32
BF16 = jnp.bfloat16
I32 = jnp.int32

D_MODEL = 1024
N_META = 16
CHUNK = 64
DQK = 64
DVA = 2 * DQK
H_A = 4
W_A = H_A * DVA
C_CONV = 256
CONV_W = 31
H_C = 4
DK_C = 64
DV_C = 64
W_C = H_C * DV_C
D_FF = 2816
FFN_CONV_W = 3
N_BUCKETS = 32
MAX_DIST = 128
REC_BLOCK = 16
EPS = 1e-6
NEG = -1e30
LOG2E = math.log2(math.e)
IN_COLS = 2 * W_A + W_A + 2 * C_CONV + 4 * W_C

LANES = 128
SUBLANES = 8
VMEM_LIMIT = 56 * 1024 * 1024

_NB = N_BUCKETS // 2
_MAX_EXACT = _NB // 2
_BUCKET_THR = tuple(
    next(n for n in range(_MAX_EXACT, MAX_DIST + 1)
         if n ** (_NB - _MAX_EXACT) * _MAX_EXACT ** k >= _MAX_EXACT ** (_NB - _MAX_EXACT) * MAX_DIST ** k)
    for k in range(1, _NB - _MAX_EXACT))


def _rms(x, g):
    return x * lax.rsqrt(jnp.mean(x * x, axis=-1, keepdims=True) + EPS) * g


def _dot(a, b):
    return jnp.dot(a, b, preferred_element_type=F32)


def _dot_nt(a, b):
    return lax.dot_general(a, b, (((1,), (1,)), ((), ())), preferred_element_type=F32)


def _dot_tn(a, b):
    return lax.dot_general(a, b, (((0,), (0,)), ((), ())), preferred_element_type=F32)


def _sigmoid(x):
    return jax.nn.sigmoid(x)


def _const_spec(shape):
    nd = len(shape)
    return pl.BlockSpec(shape, lambda *_: (0,) * nd, pipeline_mode=pl.Buffered(1))


def _bias_kernel(tab_ref, o_ref, *, qpos0, kpos0, prompt_chunks, keys_on_rows):
    h = pl.program_id(0)
    shape = o_ref.shape[1:]
    r = lax.broadcasted_iota(I32, shape, 1 if keys_on_rows else 0) + qpos0
    c = lax.broadcasted_iota(I32, shape, 0 if keys_on_rows else 1) + kpos0
    rel = c - r
    n = jnp.abs(rel)
    large = jnp.full(shape, _MAX_EXACT, I32)
    for thr in _BUCKET_THR:
        large = large + (n >= thr).astype(I32)
    bucket = jnp.where(rel > 0, _NB, 0) + jnp.where(n < _MAX_EXACT, n, large)
    out = jnp.zeros(shape, F32)
    for b in range(N_BUCKETS):
        out = jnp.where(bucket == b, tab_ref[b, h] * LOG2E, out)
    shift = CHUNK - N_META if prompt_chunks else 0
    log2_chunk = CHUNK.bit_length() - 1
    visible = ((c + shift) >> log2_chunk) <= ((r + shift) >> log2_chunk)
    o_ref[0] = jnp.where(visible, out, NEG)


def _bias_tile(rel_bias, rows, cols, qpos0, kpos0, prompt_chunks, keys_on_rows=False):
    return pl.pallas_call(
        functools.partial(_bias_kernel, qpos0=qpos0, kpos0=kpos0, prompt_chunks=prompt_chunks,
                          keys_on_rows=keys_on_rows),
        grid=(H_A,),
        in_specs=[pl.BlockSpec(memory_space=pltpu.SMEM)],
        out_specs=pl.BlockSpec((1, rows, cols), lambda h: (h, 0, 0)),
        out_shape=jax.ShapeDtypeStruct((H_A, rows, cols), F32),
        name="relpos_bias",
    )(rel_bias)


def _inproj_kernel(x_ref, gmix_ref, win_ref, gq_ref, gk_ref, seg_ref,
                   q_ref, kb_ref, vf4_ref, glu_ref, hraw_ref, k_extra_ref, v_extra_ref, *, feature_major):
    tm = x_ref.shape[0]
    h = _rms(x_ref[...], gmix_ref[...]).astype(BF16)

    def proj(lo, hi):
        return _dot(h, win_ref[:, lo:hi])

    def group_norm(a, g):
        ms = _dot((a * a).astype(BF16), seg_ref[...])
        return a * lax.rsqrt(ms + EPS) * g

    qa = group_norm(proj(0, W_A), gq_ref[...])
    q_ref[...] = (qa * (DQK ** -0.5 * LOG2E)).astype(BF16)
    ka = group_norm(proj(W_A, 2 * W_A), gk_ref[...])
    kb_ref[...] = ka.astype(BF16)
    va = proj(2 * W_A, 3 * W_A)
    for hd in range(H_A):
        vf4_ref[pl.ds(hd, tm, stride=H_A), :] = va[:, hd * DVA:(hd + 1) * DVA]
    if feature_major:
        k_extra_ref[0] = ka.T
        v_extra_ref[...] = va.T.astype(BF16)
    else:
        k_extra_ref[...] = ka
        v_extra_ref[...] = va.astype(BF16)
    glu = proj(3 * W_A, 3 * W_A + 2 * C_CONV)
    glu_ref[...] = glu[:, :C_CONV] * _sigmoid(glu[:, C_CONV:])
    hraw_ref[...] = proj(3 * W_A + 2 * C_CONV, IN_COLS)


def _inproj(x2d, gmix, win, gq, gk, seg, tm, rows_per_seq, feature_major):
    n = x2d.shape[0]
    row = lambda w: pl.BlockSpec((tm, w), lambda i: (i, 0))
    outs = [(W_A, BF16), (W_A, BF16), None, (C_CONV, F32), (4 * W_C, F32)]
    out_specs = [row(o[0]) if o else pl.BlockSpec((tm * H_A, DVA), lambda i: (i, 0)) for o in outs]
    out_shape = [jax.ShapeDtypeStruct((n, o[0]), o[1]) if o else jax.ShapeDtypeStruct((n * H_A, DVA), F32)
                 for o in outs]
    if feature_major:
        nt = rows_per_seq // tm
        out_specs += [pl.BlockSpec((1, W_A, tm), lambda i: (i // nt, 0, i % nt)),
                      pl.BlockSpec((W_A, tm), lambda i: (0, i))]
        out_shape += [jax.ShapeDtypeStruct((n // rows_per_seq, W_A, rows_per_seq), F32),
                      jax.ShapeDtypeStruct((W_A, n), BF16)]
    else:
        out_specs += [row(W_A), row(W_A)]
        out_shape += [jax.ShapeDtypeStruct((n, W_A), F32), jax.ShapeDtypeStruct((n, W_A), BF16)]
    return pl.pallas_call(
        functools.partial(_inproj_kernel, feature_major=feature_major),
        grid=(n // tm,),
        in_specs=[row(D_MODEL), _const_spec((1, D_MODEL)), _const_spec((D_MODEL, IN_COLS)),
                  _const_spec((1, W_A)), _const_spec((1, W_A)), _const_spec((W_A, W_A))],
        out_specs=out_specs,
        out_shape=out_shape,
        compiler_params=pltpu.CompilerParams(dimension_semantics=("arbitrary",), vmem_limit_bytes=VMEM_LIMIT),
        name="inproj",
    )(x2d, gmix, win, gq, gk, seg)


_CONV_PAD = 32


def _conv_kernel(glu_ref, past_ref, w_ref, b_ref, lng_ref, lnb_ref, ob_ref, new_ref, xc_ref, xr_ref, *, tt, rc):
    halo = CONV_W - 1
    h0 = _CONV_PAD - halo

    @pl.when(pl.program_id(1) == 0)
    def _():
        xc_ref[0:h0, :] = jnp.zeros((h0, C_CONV), F32)
        xc_ref[h0:_CONV_PAD, :] = past_ref[0]

    xc_ref[_CONV_PAD:_CONV_PAD + tt, :] = glu_ref[0]
    span = xr_ref.shape[1]
    for r in range(1, SUBLANES):
        xr_ref[r - 1] = xc_ref[r:r + span, :]
    bias = b_ref[...]
    for c in range(tt // rc):
        acc = jnp.zeros((rc, C_CONV), F32) + bias
        for j in range(CONV_W):
            r = (j + h0) % SUBLANES
            base = c * rc + j + h0 - r
            rows = xc_ref[base:base + rc, :] if r == 0 else xr_ref[r - 1, base:base + rc, :]
            acc = acc + w_ref[j:j + 1, :] * rows
        mu = jnp.mean(acc, axis=-1, keepdims=True)
        xm = acc - mu
        var = jnp.mean(xm * xm, axis=-1, keepdims=True)
        y = xm * lax.rsqrt(var + EPS) * lng_ref[...] + lnb_ref[...]
        ob_ref[0, c * rc:(c + 1) * rc, :] = (y * _sigmoid(y)).astype(BF16)
    new = xc_ref[h0 + tt:_CONV_PAD + tt, :]
    new_ref[0] = new
    xc_ref[h0:_CONV_PAD, :] = new


def _conv_module(glu3, past, w, b, lng, lnb, tt, shared_past):
    nb, t, _ = glu3.shape
    rc = min(tt, 128)
    pidx = (lambda bi, ti: (0, 0, 0)) if shared_past else (lambda bi, ti: (bi, 0, 0))
    return pl.pallas_call(
        functools.partial(_conv_kernel, tt=tt, rc=rc),
        grid=(nb, t // tt),
        in_specs=[pl.BlockSpec((1, tt, C_CONV), lambda bi, ti: (bi, ti, 0)),
                  pl.BlockSpec((1, CONV_W - 1, C_CONV), pidx),
                  _const_spec((CONV_W, C_CONV)), _const_spec((1, C_CONV)),
                  _const_spec((1, C_CONV)), _const_spec((1, C_CONV))],
        out_specs=[pl.BlockSpec((1, tt, C_CONV), lambda bi, ti: (bi, ti, 0)),
                   pl.BlockSpec((1, CONV_W - 1, C_CONV), lambda bi, ti: (bi, 0, 0))],
        out_shape=[jax.ShapeDtypeStruct((nb, t, C_CONV), BF16),
                   jax.ShapeDtypeStruct((nb, CONV_W - 1, C_CONV), F32)],
        scratch_shapes=[pltpu.VMEM((_CONV_PAD + tt, C_CONV), F32),
                        pltpu.VMEM((SUBLANES - 1, _CONV_PAD + tt - SUBLANES, C_CONV), F32)],
        compiler_params=pltpu.CompilerParams(dimension_semantics=("arbitrary", "arbitrary")),
        name="conv_module",
    )(glu3, past, w, b, lng, lnb)


def _hgrn_kernel(hraw_ref, lbc_ref, ghg_ref, segm_ref, segs_ref, bd_ref, rsum_ref, s0_ref,
                 oc_ref, sout_ref, s_scr, *, tc, chained):
    nblk = tc // REC_BLOCK
    blk3 = (nblk, REC_BLOCK, W_C)

    if chained:
        @pl.when(pl.program_id(1) == 0)
        def _():
            s_scr[...] = s0_ref[0]

    hr = hraw_ref[...].reshape(tc, 4 * W_C)
    z = hr[:, 0:W_C]
    v = hr[:, W_C:2 * W_C]
    q = hr[:, 2 * W_C:3 * W_C]
    gc = hr[:, 3 * W_C:4 * W_C]
    log_lb = lbc_ref[0:1, :]
    log_1m_lb = lbc_ref[1:2, :]
    one_m_lb = lbc_ref[2:3, :]

    log_sig = jnp.minimum(z, 0.0) - jnp.log(1.0 + jnp.exp(-jnp.abs(z)))
    b_arg = log_1m_lb + log_sig
    log_f = jnp.maximum(log_lb, b_arg) + jnp.log(1.0 + jnp.exp(-jnp.abs(log_lb - b_arg)))
    kc = one_m_lb * _sigmoid(-z)

    pos = lax.broadcasted_iota(I32, (tc, W_C), 0) & (REC_BLOCK - 1)
    bcum = log_f
    sh = 1
    while sh < REC_BLOCK:
        bcum = bcum + jnp.where(pos >= sh, pltpu.roll(bcum, sh, axis=0), 0.0)
        sh *= 2

    b3 = bcum.reshape(blk3)
    b_last = b3[:, REC_BLOCK - 1:REC_BLOCK, :]
    qe = (q * jnp.exp(bcum)).astype(BF16)
    ke = (kc.reshape(blk3) * jnp.exp(b_last - b3)).reshape(tc, W_C).astype(BF16)
    v_bf = v.astype(BF16)
    dl = jnp.exp(b_last)

    half = REC_BLOCK // 2
    slab = (nblk, 2, half, W_C)
    hrows = nblk * half
    b_lo, b_hi = (bcum.reshape(slab)[:, i] for i in range(2))
    k_lo, k_hi = (kc.reshape(slab)[:, i] for i in range(2))
    q_lo, q_hi = (q.reshape(slab)[:, i] for i in range(2))
    v_lo, v_hi = (v.reshape(slab)[:, i] for i in range(2))
    sidx = lax.broadcasted_iota(I32, (nblk, half, W_C), 1)

    def head_sums(xs):
        stacked = jnp.concatenate([x.reshape(hrows, W_C) for x in xs], axis=0).astype(BF16)
        a = _dot(stacked, segs_ref[...])
        return [a[t * hrows:(t + 1) * hrows, :].reshape(nblk, half, W_C) for t in range(half)]

    def slab_sums(xs):
        side = jnp.concatenate([x.reshape(hrows, W_C) for x in xs], axis=1).astype(BF16)
        r = _dot(rsum_ref[...], side)
        return [r[:, t * W_C:(t + 1) * W_C].reshape(nblk, half, W_C) for t in range(half)]

    def place(rows):
        out = jnp.zeros((nblk, half, W_C), F32)
        for t in range(half):
            out = jnp.where(sidx == t, rows[t], out)
        return out

    row_t = lambda a, t: a[:, t:t + 1, :]
    x_lo = [jnp.where(sidx <= t, jnp.exp(jnp.minimum(row_t(b_lo, t) - b_lo, 0.0)), 0.0) * k_lo * row_t(q_lo, t)
            for t in range(half)]
    o_lo = place(slab_sums([a * v_lo for a in head_sums(x_lo)]))
    x_hl = [jnp.exp(jnp.minimum(row_t(b_hi, t) - b_lo, 0.0)) * k_lo * row_t(q_hi, t) for t in range(half)]
    x_hh = [jnp.where(sidx <= t, jnp.exp(jnp.minimum(row_t(b_hi, t) - b_hi, 0.0)), 0.0) * k_hi * row_t(q_hi, t)
            for t in range(half)]
    o_hi = place(slab_sums([al * v_lo + ah * v_hi for al, ah in zip(head_sums(x_hl), head_sums(x_hh))]))
    o_intra = jnp.stack([o_lo, o_hi], axis=1).reshape(tc, W_C)

    blk = lambda a, j: a[j * REC_BLOCK:(j + 1) * REC_BLOCK, :]
    upd = [_dot_tn(blk(v_bf, j), blk(ke, j)) * bd_ref[...] for j in range(nblk)]

    def compact(s):
        folded = s
        for hd in range(1, H_C):
            folded = folded + pltpu.roll(s, hd * DK_C, axis=1)
        return folded[:, 0:DK_C]

    if chained:
        s = s_scr[...]
        s_before = []
        for j in range(nblk):
            s_before.append(s.astype(BF16))
            s = dl[j] * s + upd[j]
        s_scr[...] = s
    else:
        s_before = [s0_ref[j].astype(BF16) for j in range(nblk)]
        for j in range(nblk):
            sout_ref[j] = compact(dl[j] * s0_ref[j] + upd[j])
    o_inter = jnp.concatenate([_dot_nt(blk(qe, j), s_before[j]) for j in range(nblk)], axis=0)

    o = o_intra + o_inter
    ms = _dot((o * o).astype(BF16), segm_ref[...])
    oc = (o * lax.rsqrt(ms + EPS) * ghg_ref[...] * (gc * _sigmoid(gc))).astype(BF16)
    oc_ref[...] = oc.reshape(oc_ref.shape)

    if chained:
        @pl.when(pl.program_id(1) == pl.num_programs(1) - 1)
        def _():
            sout_ref[0] = compact(s_scr[...])


def _hgrn(hraw3, lbc, ghg, segm, segs, bd, s0, rows, tc, shared_past):
    nb, t, _ = hraw3.shape
    chained = rows == 1
    assert chained or (t == tc == REC_BLOCK and not shared_past)
    sidx = (lambda bi, ti: (0, 0, 0)) if shared_past else (lambda bi, ti: (bi, 0, 0))
    hrows = rows * tc // 2
    rsum = _seg_matrix(hrows, REC_BLOCK // 2, 1.0, BF16)
    return pl.pallas_call(
        functools.partial(_hgrn_kernel, tc=rows * tc, chained=chained),
        grid=(nb // rows, t // tc),
        in_specs=[pl.BlockSpec((rows, tc, 4 * W_C), lambda bi, ti: (bi, ti, 0)),
                  _const_spec((SUBLANES, W_C)), _const_spec((1, W_C)),
                  _const_spec((W_C, W_C)), _const_spec((W_C, W_C)), _const_spec((W_C, W_C)),
                  _const_spec((hrows, hrows)),
                  pl.BlockSpec((rows, W_C, W_C), sidx)],
        out_specs=[pl.BlockSpec((rows, tc, W_C), lambda bi, ti: (bi, ti, 0)),
                   pl.BlockSpec((rows, W_C, DK_C), lambda bi, ti: (bi, 0, 0))],
        out_shape=[jax.ShapeDtypeStruct((nb, t, W_C), BF16),
                   jax.ShapeDtypeStruct((nb, W_C, DK_C), F32)],
        scratch_shapes=[pltpu.VMEM((W_C, W_C), F32)],
        compiler_params=pltpu.CompilerParams(dimension_semantics=("arbitrary", "arbitrary")),
        name="hgrn2",
    )(hraw3, lbc, ghg, segm, segs, bd, rsum, s0)


def _split_halves(q):
    lane = lax.broadcasted_iota(I32, q.shape, 1)
    zero = jnp.zeros_like(q)
    return jnp.concatenate([jnp.where(lane < DQK, q, zero), jnp.where(lane >= DQK, q, zero)], axis=0)


def _twice(b):
    return jnp.concatenate([b, b], axis=0)


def _diff_combine(acc, l, t, lam_init, lq1, lk1, lq2, lk2, gd):
    o = acc / l
    lam = (jnp.exp(jnp.sum(lq1 * lk1, axis=1, keepdims=True))
           - jnp.exp(jnp.sum(lq2 * lk2, axis=1, keepdims=True)) + lam_init)
    a = o[:t] - lam * o[t:]
    return (_rms(a, gd) * (1.0 - lam_init)).astype(BF16)


def _attn_main_kernel(sc_ref, farc_ref, q_ref, k_ref, vt_ref, km_ref, vmt_ref, bt_ref, btm_ref,
                      lq1_ref, lk1_ref, lq2_ref, lk2_ref, gdt_ref, o_ref,
                      acc_scr, s_scr, *, tq, tk):
    h = pl.program_id(1)
    i = pl.program_id(2)
    q = q_ref[0]
    lane = lax.broadcasted_iota(I32, q.shape, 1)
    zero = jnp.zeros_like(q)
    q_half = (jnp.where(lane < DQK, q, zero), jnp.where(lane >= DQK, q, zero))
    n_tiles = (i + 1) * (tq // tk)

    def scores(j, slot):
        k_tile = k_ref[0, pl.ds(pl.multiple_of(j * tk, tk), tk), :]
        for c in range(2):
            s_scr[slot, c] = _dot_nt(k_tile, q_half[c])

    n_var = bt_ref.shape[0]
    far_bias = farc_ref[h]

    ones_rows = jnp.ones((2 * SUBLANES, tk), BF16)

    def softmax_pv(j, slot, stats, extra=None, all_far=False, masked_cols=0):
        bidx = jnp.minimum(n_tiles - 1 - j, n_var - 1)
        v_ones = jnp.concatenate([vt_ref[:, pl.ds(pl.multiple_of(j * tk, tk), tk)], ones_rows], axis=0)
        new_stats = []
        for c in range(2):
            m_prev, l_prev = stats[c]
            strips = []
            for w in range(tq // LANES):
                cols = slice(w * LANES, (w + 1) * LANES)
                visible = (w + 1) * LANES > masked_cols
                m_new = m_prev[:, cols]
                if visible and all_far:
                    s = s_scr[slot, c, :, cols]
                    m_new = jnp.maximum(m_new, jnp.max(s, axis=0, keepdims=True) + far_bias)
                elif visible:
                    s = s_scr[slot, c, :, cols] + bt_ref[bidx, 0, :, cols]
                    m_new = jnp.maximum(m_new, jnp.max(s, axis=0, keepdims=True))
                if extra is not None:
                    s_x = extra[c][:, cols]
                    m_new = jnp.maximum(m_new, jnp.max(s_x, axis=0, keepdims=True))
                shift = m_new - far_bias if all_far else m_new
                alpha = jnp.exp2(m_prev[:, cols] - m_new)
                p = jnp.exp2((s - shift).astype(BF16)) if visible else jnp.zeros((tk, LANES), BF16)
                p_x = l_x = None
                if extra is not None:
                    p_x = jnp.exp2(s_x - m_new)
                    l_x = jnp.sum(p_x, axis=0, keepdims=True)
                    p_x = p_x.astype(BF16)
                strips.append((p, m_new, alpha, p_x, l_x))
            cat = lambda k: jnp.concatenate([st[k] for st in strips], axis=1)
            p_tile, alpha = cat(0), cat(2)
            pv_sum = _dot(v_ones, p_tile)
            pv = pv_sum[0:DVA, :]
            l_new = alpha * l_prev + pv_sum[DVA:DVA + 1, :]
            if extra is not None:
                l_new = l_new + cat(4)
                pv = pv + _dot(vmt_ref[...], cat(3))
            acc_scr[c] = alpha * acc_scr[c] + pv
            new_stats.append((cat(1), l_new))
        return tuple(new_stats)

    def pair(n, stats, all_far):
        scores(2 * n + 1, 1)
        stats = softmax_pv(2 * n, 0, stats, all_far=all_far)
        scores(2 * n + 2, 0)
        return softmax_pv(2 * n + 1, 1, stats, all_far=all_far)

    assert (tq // tk) % 2 == 0
    acc_scr[...] = jnp.zeros(acc_scr.shape, F32)
    stat0 = (jnp.full((1, tq), -jnp.inf, F32), jnp.zeros((1, tq), F32))
    scores(0, 0)
    n_pairs = n_tiles // 2 - 1
    n_far_pairs = jnp.clip((n_tiles + 1 - n_var) // 2, 0, n_pairs)
    stats = lax.fori_loop(0, n_far_pairs, functools.partial(pair, all_far=True), (stat0, stat0))
    stats = lax.fori_loop(n_far_pairs, n_pairs, functools.partial(pair, all_far=False), stats)
    scores(n_tiles - 1, 1)
    stats = softmax_pv(n_tiles - 2, 0, stats)
    bias_meta = jnp.where(i == 0, btm_ref[0], farc_ref[h])
    s_meta = [_dot_nt(km_ref[0], q_half[c]) + bias_meta for c in range(2)]
    stats = softmax_pv(n_tiles - 1, 1, stats, s_meta, masked_cols=tq - tk)
    o = [acc_scr[c] * (1.0 / stats[c][1]) for c in range(2)]

    lam_init = sc_ref[0]
    lam = (jnp.exp(jnp.sum(lq1_ref[...] * lk1_ref[...], axis=1, keepdims=True))
           - jnp.exp(jnp.sum(lq2_ref[...] * lk2_ref[...], axis=1, keepdims=True)) + lam_init)
    a = o[0] - lam * o[1]
    gain = jnp.concatenate([gdt_ref[...]] * (tq // LANES), axis=1)
    y = a * lax.rsqrt(jnp.mean(a * a, axis=0, keepdims=True) + EPS) * gain * (1.0 - lam_init)
    o_ref[0] = y.T.astype(BF16)


def _attn_bias_tiles(rel_bias, tq, tk):
    n_var = -(-(tq + MAX_DIST - 1) // tk) + 1
    base = N_META + n_var * tk
    return jnp.stack([_bias_tile(rel_bias, tk, tq, base, base + tq - (d + 1) * tk, True, keys_on_rows=True)
                      for d in range(n_var)])


def _attn_main(q3, k3, vt, km, vmt, bt, btm, farc, lam_init, lq1, lk1, lq2, lk2, gdt, tq, tk):
    nb, t, _ = q3.shape
    p = km.shape[1]
    assert tq % tk == 0 and tk % CHUNK == 0 and bt.shape[2:] == (tk, tq)
    smem = pl.BlockSpec(memory_space=pltpu.SMEM)
    vec = lambda: _const_spec((1, DQK))
    return pl.pallas_call(
        functools.partial(_attn_main_kernel, tq=tq, tk=tk),
        grid=(nb, H_A, t // tq),
        in_specs=[smem, smem,
                  pl.BlockSpec((1, tq, DVA), lambda b, h, i: (b, i, h)),
                  pl.BlockSpec((1, t, DVA), lambda b, h, i: (b, 0, h)),
                  pl.BlockSpec((DVA, t), lambda b, h, i: (h, b)),
                  pl.BlockSpec((1, p, DVA), lambda b, h, i: (0, 0, h)),
                  pl.BlockSpec((DVA, p), lambda b, h, i: (h, 0)),
                  pl.BlockSpec((bt.shape[0], 1, tk, tq), lambda b, h, i: (0, h, 0, 0)),
                  pl.BlockSpec((1, p, tq), lambda b, h, i: (h, 0, 0)),
                  vec(), vec(), vec(), vec(),
                  pl.BlockSpec((DVA, LANES), lambda b, h, i: (h, 0))],
        out_specs=pl.BlockSpec((1, tq, DVA), lambda b, h, i: (b, i, h)),
        out_shape=jax.ShapeDtypeStruct((nb, t, W_A), BF16),
        scratch_shapes=[pltpu.VMEM((2, DVA, tq), F32), pltpu.VMEM((2, 2, tk, tq), F32)],
        compiler_params=pltpu.CompilerParams(dimension_semantics=("arbitrary",) * 3),
        name="attn_main",
    )(lam_init, farc, q3, k3, vt, km, vmt, bt, btm, lq1, lk1, lq2, lk2, gdt)


def _attn_small_kernel(sc_ref, q_ref, kc_ref, vc_ref, kn_ref, vn_ref, bp_ref, bn_ref,
                       lq1_ref, lk1_ref, lq2_ref, lk2_ref, gd_ref, o_ref, *, t, n_cached):
    b = pl.program_id(0)
    p = vc_ref.shape[2] // H_A
    past_bias = jnp.where(b < n_cached, 0.0, NEG)
    q_all = q_ref[0]
    kn_all = kn_ref[0]
    vn_all = vn_ref[0]
    outs = []
    for hd in range(H_A):
        cols = slice(hd * DVA, (hd + 1) * DVA)
        qq = _split_halves(q_all[:, cols])
        s_p = _dot(qq, kc_ref[0, 0, cols, :].astype(BF16)) + _twice(bp_ref[hd]) + past_bias
        s_n = _dot_nt(qq, kn_all[:, cols]) + _twice(bn_ref[hd])
        m = jnp.maximum(jnp.max(s_p, axis=1, keepdims=True), jnp.max(s_n, axis=1, keepdims=True))
        p_p = jnp.exp2(s_p - m)
        p_n = jnp.exp2(s_n - m)
        l = jnp.sum(p_p, axis=1, keepdims=True) + jnp.sum(p_n, axis=1, keepdims=True)
        v_past = vc_ref[0, 0, pl.ds(hd, p, stride=H_A), :].astype(BF16)
        acc = _dot(p_p.astype(BF16), v_past) + _dot(p_n.astype(BF16), vn_all[:, cols])
        outs.append(_diff_combine(acc, l, t, sc_ref[0], lq1_ref[...], lk1_ref[...],
                                  lq2_ref[...], lk2_ref[...], gd_ref[:, cols]))
    o_ref[0] = jnp.concatenate(outs, axis=1)


def _attn_small(q3, cache_kt, cache_v4, layer, kn, vn, bp, bn, lam_init, lq1, lk1, lq2, lk2, gdiff):
    nb, t, _ = q3.shape
    n_cached, p = cache_kt.shape[1], cache_kt.shape[3]
    smem = pl.BlockSpec(memory_space=pltpu.SMEM)
    vec = lambda: _const_spec((1, DQK))
    cached = lambda b: (layer, jnp.minimum(b, n_cached - 1), 0, 0)
    new = pl.BlockSpec((1, t, W_A), lambda b: (b, 0, 0))
    return pl.pallas_call(
        functools.partial(_attn_small_kernel, t=t, n_cached=n_cached),
        grid=(nb,),
        in_specs=[smem, new,
                  pl.BlockSpec((1, 1, W_A, p), cached), pl.BlockSpec((1, 1, p * H_A, DVA), cached),
                  new, new, _const_spec((H_A, t, p)), _const_spec((H_A, t, t)),
                  vec(), vec(), vec(), vec(), _const_spec((1, W_A))],
        out_specs=new,
        out_shape=jax.ShapeDtypeStruct((nb, t, W_A), BF16),
        compiler_params=pltpu.CompilerParams(dimension_semantics=("arbitrary",)),
        name="attn_small",
    )(lam_init, q3, cache_kt, cache_v4, kn, vn, bp, bn, lq1, lk1, lq2, lk2, gdiff)


_MXU_N = 256
_FF_CHUNK = 6 * _MXU_N
_FF_BOUNDS = tuple((lo, min(lo + _FF_CHUNK, D_FF)) for lo in range(0, D_FF, _FF_CHUNK))
assert all((hi - lo) % _MXU_N == 0 for lo, hi in _FF_BOUNDS)


def _ffn_kernel(x_ref, oa_ref, ob_ref, oc_ref, wout_ref, gffn_ref, wup_ref, cw_ref, cb_ref, wdn_ref,
                past_ref, gfin_ref, y_ref, new_ref, halo_scr, *, bb, tt, final_norm):
    rows = bb * tt
    taps = FFN_CONV_W - 1

    @pl.when(pl.program_id(1) == 0)
    def _():
        halo_scr[...] = past_ref[...]

    x1 = (x_ref[...].reshape(rows, D_MODEL)
          + _dot(oa_ref[...].reshape(rows, W_A), wout_ref[0:W_A, :])
          + _dot(ob_ref[...].reshape(rows, C_CONV), wout_ref[W_A:W_A + C_CONV, :])
          + _dot(oc_ref[...].reshape(rows, W_C), wout_ref[W_A + C_CONV:D_MODEL, :]))
    h2 = _rms(x1, gffn_ref[...]).astype(BF16)

    acc = jnp.zeros((rows, D_MODEL), F32)
    for lo, hi in _FF_BOUNDS:
        gate = _dot(h2, wup_ref[:, lo:hi])
        val = _dot(h2, wup_ref[:, D_FF + lo:D_FF + hi])
        gate3 = gate.reshape(bb, tt, hi - lo)
        tpos = lax.broadcasted_iota(I32, gate3.shape, 1)
        conv = cw_ref[taps:taps + 1, lo:hi] * gate + cb_ref[:, lo:hi]
        for d in range(1, taps + 1):
            shifted = pltpu.roll(gate, d, axis=0).reshape(gate3.shape)
            for e in range(d):
                shifted = jnp.where(tpos == e, halo_scr[:, taps - d + e:taps - d + e + 1, lo:hi], shifted)
            conv = conv + cw_ref[taps - d:taps - d + 1, lo:hi] * shifted.reshape(rows, hi - lo)
        halo_scr[:, :, lo:hi] = gate3[:, tt - taps:tt, :]
        gelu = 0.5 * conv * (1.0 + jnp.tanh(math.sqrt(2.0 / math.pi) * (conv + 0.044715 * (conv * conv * conv))))
        acc = acc + _dot((gelu * val).astype(BF16), wdn_ref[lo:hi, :])

    y = x1 + acc
    if final_norm:
        y = _rms(y, gfin_ref[...])
    y_ref[...] = y.reshape(bb, tt, D_MODEL)
    new_ref[...] = halo_scr[...]


def _ffn(x3, oa3, ob3, oc3, wout, gffn, wup, cw, cb, wdn, past, gfin, bb, tt, shared_past, final_norm):
    nb, t, _ = x3.shape
    taps = FFN_CONV_W - 1
    assert tt >= taps
    tok = lambda w: pl.BlockSpec((bb, tt, w), lambda bi, ti: (bi, ti, 0))
    pidx = (lambda bi, ti: (0, 0, 0)) if shared_past else (lambda bi, ti: (bi, 0, 0))
    assert not (shared_past and bb != 1)
    return pl.pallas_call(
        functools.partial(_ffn_kernel, bb=bb, tt=tt, final_norm=final_norm),
        grid=(nb // bb, t // tt),
        in_specs=[tok(D_MODEL), tok(W_A), tok(C_CONV), tok(W_C),
                  _const_spec((D_MODEL, D_MODEL)), _const_spec((1, D_MODEL)),
                  _const_spec((D_MODEL, 2 * D_FF)), _const_spec((FFN_CONV_W, D_FF)), _const_spec((1, D_FF)),
                  _const_spec((D_FF, D_MODEL)),
                  pl.BlockSpec((bb, taps, D_FF), pidx), _const_spec((1, D_MODEL))],
        out_specs=[tok(D_MODEL), pl.BlockSpec((bb, taps, D_FF), lambda bi, ti: (bi, 0, 0))],
        out_shape=[jax.ShapeDtypeStruct((nb, t, D_MODEL), F32),
                   jax.ShapeDtypeStruct((nb, taps, D_FF), F32)],
        scratch_shapes=[pltpu.VMEM((bb, taps, D_FF), F32)],
        compiler_params=pltpu.CompilerParams(dimension_semantics=("arbitrary", "arbitrary"),
                                             vmem_limit_bytes=VMEM_LIMIT),
        name="outproj_ffn",
    )(x3, oa3, ob3, oc3, wout, gffn, wup, cw, cb, wdn, past, gfin)


def _seg_matrix(width, group, value, dtype):
    g = jnp.arange(width, dtype=I32) // group
    return jnp.where(g[:, None] == g[None, :], value, 0.0).astype(dtype)


def _state_to_kernel(s):
    n = s.shape[0]
    eye = jnp.eye(H_C, dtype=s.dtype)
    full = jnp.einsum('nhdv,hg->nhvgd', s, eye)
    return full.reshape(n, W_C, H_C * DK_C)


def _state_from_kernel(sc):
    return jnp.swapaxes(sc.reshape(sc.shape[0], H_C, DV_C, DK_C), 2, 3)


def _state_expand(sc, blockdiag):
    return jnp.tile(sc, (1, 1, H_C)) * blockdiag


def _pick_tile(total, want):
    t = min(total, want)
    assert total % t == 0, (total, want)
    return t


def kernel(x_prompt, x_sample, cache_k, cache_v, state_conv, state_hgrn, state_ffn, meta_tokens, rel_bias, g_mix, w_in, g_q, g_k, lam_q1, lam_k1, lam_q2, lam_k2, g_diff, conv_w, conv_b, ln_g, ln_b, lb_logits, g_hgrn, w_out, g_ffn, w_up, ffn_conv_w, ffn_conv_b, w_down, g_final):
    depth = g_mix.shape[0]
    bp, seq, _ = x_prompt.shape
    bs, ts, _ = x_sample.shape
    past_len = cache_k.shape[2]
    assert ts == N_META == REC_BLOCK and meta_tokens.shape[0] == N_META
    assert seq % CHUNK == 0 and past_len % CHUNK == 0

    lb_all = jnp.cumsum(jax.nn.softmax(lb_logits.astype(F32), axis=0), axis=0)
    lb_all = lb_all - lb_all[0:1]
    lb_consts = jnp.stack([jnp.log(lb_all), jnp.log1p(-lb_all), 1.0 - lb_all]
                          + [jnp.zeros_like(lb_all)] * (SUBLANES - 3), axis=1)

    seg_qk = _seg_matrix(W_A, DQK, 1.0 / DQK, BF16)
    seg_mean = _seg_matrix(W_C, DV_C, 1.0 / DV_C, BF16)
    seg_sum = _seg_matrix(W_C, DK_C, 1.0, BF16)
    blockdiag = _seg_matrix(W_C, DK_C, 1.0, F32)

    tq = _pick_tile(seq, 512)
    tk = _pick_tile(tq, 256)
    bt_main = _attn_bias_tiles(rel_bias, tq, tk)
    bt_meta = _bias_tile(rel_bias, N_META, tq, N_META, 0, True, keys_on_rows=True)
    far_bias = rel_bias[_NB - 1, :] * LOG2E
    bias_past = _bias_tile(rel_bias, ts, past_len, past_len, 0, False)
    bias_new = _bias_tile(rel_bias, ts, ts, past_len, past_len, False)

    cache_kt = jnp.swapaxes(cache_k.reshape(depth, bs, past_len, W_A), 2, 3)
    cache_v4 = cache_v.reshape(depth, bs, past_len * H_A, DVA)

    xs = jnp.concatenate([x_sample, meta_tokens.astype(x_sample.dtype)[None]], axis=0)
    xm = x_prompt
    nbs = bs + 1
    zrow = lambda a: jnp.zeros((1,) + a.shape[1:], a.dtype)

    tm_main = _pick_tile(bp * seq, 512)
    small_rows = max(r for r in range(1, 17) if nbs % r == 0)
    outs = {k: [] for k in ("kp", "vp", "cp", "hp", "fp", "ks", "vs", "cs", "hs", "fs")}
    for l in range(depth):
        last = l == depth - 1
        lam_init = jnp.full((1,), 0.8 - 0.6 * math.exp(-0.3 * l), F32)
        win = w_in[l].astype(BF16)
        wout = w_out[l].astype(BF16)
        wup = w_up[l].astype(BF16)
        wdn = w_down[l].astype(BF16)
        gq = jnp.tile(g_q[l], 2 * H_A)[None]
        gk = jnp.tile(g_k[l], 2 * H_A)[None]
        row = lambda a: a[l][None]
        common_attn = (lam_init, row(lam_q1), row(lam_k1), row(lam_q2), row(lam_k2), row(g_diff))

        def mixer_front(x3, tm, feature_major):
            nb, t, _ = x3.shape
            q, kb, vf4, glu, hraw, k_extra, v_extra = _inproj(
                x3.reshape(nb * t, D_MODEL), row(g_mix), win, gq, gk, seg_qk, tm, t, feature_major)
            r3 = lambda a: a.reshape(nb, t, a.shape[-1])
            return (r3(q), r3(kb), vf4.reshape(nb, t, H_A, DVA), r3(glu), r3(hraw),
                    k_extra if feature_major else r3(k_extra), v_extra if feature_major else r3(v_extra))

        q, kb, vf_s, glu, hraw, kf_s, vb = mixer_front(xs, nbs * ts, False)
        oa = _attn_small(q, cache_kt, cache_v4, l, kb, vb, bias_past, bias_new, *common_attn)
        ob, conv_s = _conv_module(glu, jnp.concatenate([state_conv[l], zrow(state_conv[l])]),
                                  conv_w[l], row(conv_b), row(ln_g), row(ln_b), ts, False)
        oc, hg_s = _hgrn(hraw, lb_consts[l], row(g_hgrn), seg_mean, seg_sum, blockdiag,
                         _state_to_kernel(jnp.concatenate([state_hgrn[l], zrow(state_hgrn[l])]).astype(F32)),
                         small_rows, ts, False)
        xs, ffn_s = _ffn(xs, oa, ob, oc, wout, row(g_ffn), wup, ffn_conv_w[l], row(ffn_conv_b), wdn,
                         jnp.concatenate([state_ffn[l], zrow(state_ffn[l])]), g_final[None],
                         nbs, ts, False, last)
        meta_k, meta_vt = kb[bs:], vb[bs].T
        gdt = jnp.broadcast_to(g_diff[l][:, None], (W_A, LANES))

        q, kb, vf_m, glu, hraw, kft_m, vt = mixer_front(xm, tm_main, True)
        oa = _attn_main(q, kb, vt, meta_k, meta_vt, bt_main, bt_meta, far_bias, *common_attn[:-1], gdt, tq, tk)
        ob, conv_m = _conv_module(glu, conv_s[bs:], conv_w[l], row(conv_b), row(ln_g), row(ln_b),
                                  _pick_tile(seq, 512), True)
        oc, hg_m = _hgrn(hraw, lb_consts[l], row(g_hgrn), seg_mean, seg_sum, blockdiag,
                         _state_expand(hg_s[bs:], blockdiag), 1, _pick_tile(seq, 256), True)
        xm, ffn_m = _ffn(xm, oa, ob, oc, wout, row(g_ffn), wup, ffn_conv_w[l], row(ffn_conv_b), wdn,
                         ffn_s[bs:], g_final[None], 1, _pick_tile(seq, 512), True, last)

        meta_rows = lambda a: jnp.broadcast_to(a[bs:], (bp,) + a.shape[1:])
        kt_p = jnp.concatenate([meta_rows(jnp.swapaxes(kf_s, 1, 2)), kft_m], axis=2)
        outs["kp"].append(jnp.swapaxes(kt_p, 1, 2).reshape(bp, N_META + seq, H_A, 2, DQK))
        outs["vp"].append(jnp.concatenate([meta_rows(vf_s), vf_m], axis=1))
        outs["cp"].append(conv_m)
        outs["hp"].append(_state_from_kernel(hg_m))
        outs["fp"].append(ffn_m)
        outs["ks"].append(kf_s[:bs].reshape(bs, ts, H_A, 2, DQK))
        outs["vs"].append(vf_s[:bs])
        outs["cs"].append(conv_s[:bs])
        outs["hs"].append(_state_from_kernel(hg_s[:bs]))
        outs["fs"].append(ffn_s[:bs])

    st = {k: jnp.stack(v) for k, v in outs.items()}
    return (xm, xs[:bs], st["kp"], st["vp"], st["cp"], st["hp"], st["fp"],
            st["ks"], st["vs"], st["cs"], st["hs"], st["fs"])
```
